```python
import jax, jax.numpy as jnp
from jax import lax
import numpy as np

D_MODEL = 1024
BATCH = 8
SEQ = 2048
DEPTH = 4
DEC_BATCH = 128
DEC_SEQ = 8
PAST_LEN = 2048
PAGE_SIZE = 128

N_MIXERS = 4
N_RET_LAYERS = (DEPTH + 3) // 4
N_HGRN_LAYERS = (DEPTH + 2) // 4
N_DIL_LAYERS = (DEPTH + 1) // 4
N_NSA_LAYERS = DEPTH // 4
N_DENSE_LAYERS = (DEPTH + 1) // 2
N_MOE_LAYERS = DEPTH // 2

ROPE_THETA = 10000.0
EPS = 1e-6
NEG = -1e30
HEAD_DIM = 128

RET_DK = 256
RET_HEADS = D_MODEL // RET_DK
RET_DV = 2 * RET_DK
RET_CHUNK = 128

HG_DK = 128
HG_HEADS = D_MODEL // HG_DK
HG_DV = D_MODEL // HG_HEADS
HG_CHUNK = 64

DIL_HEADS = 4
DIL_GROUPS = ((128, 1), (512, 4), (2048, 16))
DIL_QBLOCK = 128

NSA_HEADS = D_MODEL // HEAD_DIM
NSA_KV_HEADS = 2
NSA_GROUP = NSA_HEADS // NSA_KV_HEADS
NSA_CMP_LEN = 32
NSA_CMP_STRIDE = 16
NSA_CMP_HIDDEN = HEAD_DIM
NSA_SLC_LEN = 64
NSA_TOP_N = 16
NSA_WINDOW = 512
NSA_QBLOCK = 64

D_FF = 2816
N_EXPERTS = 8
TOP_K = 2
D_FF_EXPERT = 3584
MOE_BLOCK = 128

kernel_name = 'hybrid_retnet_hgrn2_dilated_nsa_step'


def rms_norm(x, g):
    xf = x.astype(jnp.float32)
    y = xf * lax.rsqrt(jnp.mean(xf * xf, axis=-1, keepdims=True) + EPS)
    return (y * g.astype(jnp.float32)).astype(x.dtype)


def rotary(x, pos):
    half = x.shape[-1] // 2
    inv = ROPE_THETA ** (-jnp.arange(half, dtype=jnp.float32) / half)
    ang = pos.astype(jnp.float32)[:, None] * inv[None, :]
    cos = jnp.cos(ang)[:, None, :]
    sin = jnp.sin(ang)[:, None, :]
    xf = x.astype(jnp.float32)
    x1, x2 = xf[..., :half], xf[..., half:]
    return jnp.concatenate([x1 * cos - x2 * sin, x2 * cos + x1 * sin], axis=-1).astype(x.dtype)


def chunk_len(T, c):
    return c if T % c == 0 else T


def to_chunks(a, c):
    B, T = a.shape[:2]
    return jnp.swapaxes(a.reshape((B, T // c, c) + a.shape[2:]), 0, 1)


def from_chunks(a):
    a = jnp.swapaxes(a, 0, 1)
    return a.reshape((a.shape[0], a.shape[1] * a.shape[2]) + a.shape[3:])


def masked_softmax(s, valid):
    s = jnp.where(valid, s, NEG)
    p = jnp.exp(s - jnp.max(s, axis=-1, keepdims=True)) * valid
    return p / jnp.maximum(jnp.sum(p, axis=-1, keepdims=True), 1e-30)


def swiglu(x, w_up, w_down):
    g, u = jnp.split(x @ w_up, 2, axis=-1)
    return (jax.nn.silu(g) * u) @ w_down


def retention_chunk(S, chunk, log_gamma):
    q, k, v = chunk
    C = q.shape[1]
    t = jnp.arange(C, dtype=jnp.float32)
    diff = t[:, None] - t[None, :]
    decay = jnp.where(diff >= 0, jnp.exp(log_gamma[:, None, None] * jnp.maximum(diff, 0.0)), 0.0)
    scores = jnp.einsum('bthd,bshd->bhts', q, k) * decay[None]
    o = jnp.einsum('bhts,bshv->bthv', scores, v)
    q_dec = q * jnp.exp(log_gamma[None, :] * (t[:, None] + 1.0))[None, :, :, None]
    o = o + jnp.einsum('bthd,bhdv->bthv', q_dec, S)
    k_dec = k * jnp.exp(log_gamma[None, :] * (C - 1.0 - t[:, None]))[None, :, :, None]
    S_new = jnp.exp(log_gamma * C)[None, :, None, None] * S + jnp.einsum('bshd,bshv->bhdv', k_dec, v)
    return S_new, o


def retention_mixer(h, start, S0, w_in, gn_g, gn_b, w_out):
    B, T, _ = h.shape
    pos = start + jnp.arange(T, dtype=jnp.int32)
    qk_w, v_w = RET_HEADS * RET_DK, RET_HEADS * RET_DV
    proj = h @ w_in
    q = proj[..., :qk_w].reshape(B, T, RET_HEADS, RET_DK)
    k = proj[..., qk_w:2 * qk_w].reshape(B, T, RET_HEADS, RET_DK)
    v = proj[..., 2 * qk_w:2 * qk_w + v_w].reshape(B, T, RET_HEADS, RET_DV).astype(jnp.float32)
    g = proj[..., 2 * qk_w + v_w:]
    q = rotary(q, pos).astype(jnp.float32)
    k = rotary(k, pos).astype(jnp.float32) * RET_DK ** -0.5
    log_gamma = jnp.log(1.0 - 2.0 ** (-5.0 - jnp.arange(RET_HEADS, dtype=jnp.float32)))
    c = chunk_len(T, RET_CHUNK)
    S_fin, o = lax.scan(lambda S, ch: retention_chunk(S, ch, log_gamma), S0.astype(jnp.float32),
                        (to_chunks(q, c), to_chunks(k, c), to_chunks(v, c)))
    o = from_chunks(o)
    mu = jnp.mean(o, axis=-1, keepdims=True)
    var = jnp.mean(jnp.square(o - mu), axis=-1, keepdims=True)
    o = ((o - mu) * lax.rsqrt(var + EPS)).reshape(B, T, v_w) * gn_g.astype(jnp.float32) + gn_b.astype(jnp.float32)
    y = (jax.nn.silu(g.astype(jnp.float32)) * o).astype(h.dtype) @ w_out
    return y, S_fin


def gla_chunk(S, chunk):
    q, k, v, log_f = chunk
    C = q.shape[1]
    b = jnp.cumsum(log_f, axis=1)
    causal = jnp.tril(jnp.ones((C, C), dtype=bool))
    rel = jnp.where(causal[None, :, :, None, None], b[:, :, None] - b[:, None, :], -jnp.inf)
    scores = jnp.einsum('bthd,bshd,btshd->bhts', q, k, jnp.exp(rel))
    o = jnp.einsum('bhts,bshv->bthv', scores, v) + jnp.einsum('bthd,bhdv->bthv', q * jnp.exp(b), S)
    b_last = b[:, -1]
    S_new = jnp.exp(b_last)[..., None] * S + jnp.einsum('bshd,bshv->bhdv', k * jnp.exp(b_last[:, None] - b), v)
    return S_new, o


def hgrn2_mixer(h, S0, w_in, lower_bound, gn_g, w_out):
    B, T, _ = h.shape
    q, fz, i_in, g = jnp.split(h @ w_in, 4, axis=-1)
    q = jax.nn.silu(q.astype(jnp.float32)).reshape(B, T, HG_HEADS, HG_DK)
    f = lower_bound + (1.0 - lower_bound) * jax.nn.sigmoid(fz.astype(jnp.float32))
    log_f = jnp.log(f).reshape(B, T, HG_HEADS, HG_DK)
    k = (1.0 - f).reshape(B, T, HG_HEADS, HG_DK)
    v = i_in.astype(jnp.float32).reshape(B, T, HG_HEADS, HG_DV)
    c = chunk_len(T, HG_CHUNK)
    S_fin, o = lax.scan(gla_chunk, S0.astype(jnp.float32),
                        (to_chunks(q, c), to_chunks(k, c), to_chunks(v, c), to_chunks(log_f, c)))
    o = from_chunks(o)
    o = (o * lax.rsqrt(jnp.mean(o * o, axis=-1, keepdims=True) + EPS)).reshape(B, T, HG_HEADS * HG_DV)
    y = (o * gn_g.astype(jnp.float32) * jax.nn.sigmoid(g.astype(jnp.float32))).astype(h.dtype) @ w_out
    return y, S_fin


def dilated_group_attend(q, pos, k_all, v_all, k_pos0, window, dil):
    B, T, H, D = q.shape
    n_keys = window // dil + 1
    offs = dil * jnp.arange(n_keys, dtype=jnp.int32)
    blk = chunk_len(T, DIL_QBLOCK)
    scale = D ** -0.5

    def one(args):
        qb, pb = args
        idx = pb[:, None] - offs[None, :] - k_pos0
        valid = idx >= 0
        idx = jnp.maximum(idx, 0)
        kb = k_all[:, idx]
        vb = v_all[:, idx]
        s = jnp.einsum('bqhd,bqjhd->bhqj', qb, kb).astype(jnp.float32) * scale
        s = jnp.where(valid[None, None], s, NEG)
        m = jnp.max(s, axis=-1, keepdims=True)
        p = jnp.exp(s - m)
        l = jnp.sum(p, axis=-1)
        o = jnp.einsum('bhqj,bqjhd->bqhd', p, vb.astype(jnp.float32)) / jnp.swapaxes(l, 1, 2)[..., None]
        lse = jnp.swapaxes(m[..., 0] + jnp.log(l), 1, 2)
        return o, lse

    o, lse = lax.map(one, (to_chunks(q, blk), pos.reshape(T // blk, blk)))
    return from_chunks(o), from_chunks(lse)


def dilated_mixer(h, start, bufs, w_in, qn_g, kn_g, w_out):
    B, T, _ = h.shape
    pos = start + jnp.arange(T, dtype=jnp.int32)
    proj = (h @ w_in).reshape(B, T, len(DIL_GROUPS), 3, DIL_HEADS, HEAD_DIM)
    outs, lses, new_bufs = [], [], []
    for gi, (window, dil) in enumerate(DIL_GROUPS):
        q = rotary(rms_norm(proj[:, :, gi, 0], qn_g[gi]), pos)
        k = rotary(rms_norm(proj[:, :, gi, 1], kn_g[gi]), pos)
        kv = jnp.stack([k, proj[:, :, gi, 2]], axis=2)
        if bufs is None:
            kv_all, k_pos0, keep = kv, start, min(window, T)
        else:
            buf = bufs[gi]
            kv_all = jnp.concatenate([buf.astype(kv.dtype), kv], axis=1)
            k_pos0, keep = start - buf.shape[1], buf.shape[1]
        o, lse = dilated_group_attend(q, pos, kv_all[:, :, 0], kv_all[:, :, 1], k_pos0, window, dil)
        outs.append(o)
        lses.append(lse)
        new_bufs.append(kv_all[:, kv_all.shape[1] - keep:])
    wts = jax.nn.softmax(jnp.stack(lses, axis=0), axis=0)
    o = jnp.sum(wts[..., None] * jnp.stack(outs, axis=0), axis=0)
    y = o.reshape(B, T, DIL_HEADS * HEAD_DIM).astype(h.dtype) @ w_out
    return y, new_bufs


def nsa_compress(x, pe, w1, w2):
    B, L, KVH, D = x.shape
    n_cmp = (L - NSA_CMP_LEN) // NSA_CMP_STRIDE + 1
    idx = NSA_CMP_STRIDE * jnp.arange(n_cmp, dtype=jnp.int32)[:, None] + jnp.arange(NSA_CMP_LEN, dtype=jnp.int32)[None, :]
    blocks = x[:, idx] + pe[None, None, :, None, :]
    blocks = jnp.moveaxis(blocks, 3, 2).reshape(B, n_cmp, KVH, NSA_CMP_LEN * D)
    return jax.nn.silu(blocks @ w1) @ w2


def nsa_cmp_to_slc(n_cmp, n_slc):
    i = np.arange(n_cmp)[:, None] * NSA_CMP_STRIDE
    j = np.arange(n_slc)[None, :] * NSA_SLC_LEN
    ov = np.clip(np.minimum(i + NSA_CMP_LEN, j + NSA_SLC_LEN) - np.maximum(i, j), 0, None)
    return jnp.asarray(ov / NSA_CMP_LEN, dtype=jnp.float32)


def nsa_mixer(h, start, kv_past, win_buf, w_in, w_gate, qn_g, kn_g, cmp_pe, cmp_w1, cmp_w2, w_out):
    B, T, _ = h.shape
    pos = start + jnp.arange(T, dtype=jnp.int32)
    qw = NSA_HEADS * HEAD_DIM
    proj = h @ w_in
    q = rotary(rms_norm(proj[..., :qw].reshape(B, T, NSA_HEADS, HEAD_DIM), qn_g), pos)
    kv = proj[..., qw:].reshape(B, T, 6, NSA_KV_HEADS, HEAD_DIM)
    k_cmp = rotary(rms_norm(kv[:, :, 0], kn_g[0]), pos)
    k_slc = rotary(rms_norm(kv[:, :, 2], kn_g[1]), pos)
    k_win = rotary(rms_norm(kv[:, :, 4], kn_g[2]), pos)
    new_rows = jnp.stack([k_cmp, kv[:, :, 1], k_slc, kv[:, :, 3]], axis=2)
    win_rows = jnp.stack([k_win, kv[:, :, 5]], axis=2)
    if kv_past is None:
        full = new_rows
        win_all, win_pos0, keep = win_rows, start, min(NSA_WINDOW, T)
    else:
        full = jnp.concatenate([kv_past.astype(new_rows.dtype), new_rows], axis=1)
        win_all = jnp.concatenate([win_buf.astype(win_rows.dtype), win_rows], axis=1)
        win_pos0, keep = start - win_buf.shape[1], win_buf.shape[1]
    L = full.shape[1]
    kc = nsa_compress(full[:, :, 0], cmp_pe[0], cmp_w1[0], cmp_w2[0])
    vc = nsa_compress(full[:, :, 1], cmp_pe[1], cmp_w1[1], cmp_w2[1])
    n_cmp = kc.shape[1]
    cmp_end = NSA_CMP_STRIDE * jnp.arange(n_cmp, dtype=jnp.int32) + NSA_CMP_LEN - 1
    n_slc = -(-L // NSA_SLC_LEN)
    cmp_to_slc = nsa_cmp_to_slc(n_cmp, n_slc)
    top_n = min(NSA_TOP_N, n_slc)
    k_slc_all, v_slc_all = full[:, :, 2], full[:, :, 3]
    win_pad = jnp.pad(win_all, ((0, 0), (NSA_WINDOW, 0), (0, 0), (0, 0), (0, 0)))
    b_ix = jnp.arange(B)[:, None, None, None]
    h_ix = jnp.arange(NSA_KV_HEADS)[None, None, :, None]
    scale = HEAD_DIM ** -0.5
    blk = chunk_len(T, NSA_QBLOCK)

    def one(args):
        qb, pb = args
        s = jnp.einsum('bqkgd,bnkd->bkgqn', qb, kc).astype(jnp.float32) * scale
        p_cmp = masked_softmax(s, cmp_end[None, :] <= pb[:, None])
        o_cmp = jnp.einsum('bkgqn,bnkd->bqkgd', p_cmp, vc.astype(jnp.float32))
        imp = jnp.einsum('bkgqn,nj->bqkj', p_cmp, cmp_to_slc)
        cur = pb // NSA_SLC_LEN
        jb = jnp.arange(n_slc, dtype=jnp.int32)[None, :]
        forced = (jb == 0) | (jb == cur[:, None]) | (jb == cur[:, None] - 1)
        avail = jb <= cur[:, None]
        score = jnp.where(forced[None, :, None, :], 1e4, jnp.where(avail[None, :, None, :], imp, -1.0))
        _, sel = lax.top_k(score, top_n)
        tok = (sel[..., None] * NSA_SLC_LEN + jnp.arange(NSA_SLC_LEN, dtype=jnp.int32)).reshape(
            B, blk, NSA_KV_HEADS, top_n * NSA_SLC_LEN)
        tok_idx = jnp.minimum(tok, L - 1)
        ks = k_slc_all[b_ix, tok_idx, h_ix]
        vs = v_slc_all[b_ix, tok_idx, h_ix]
        s = jnp.einsum('bqkgd,bqknd->bkgqn', qb, ks).astype(jnp.float32) * scale
        valid = jnp.swapaxes(tok <= pb[None, :, None, None], 1, 2)[:, :, None]
        o_slc = jnp.einsum('bkgqn,bqknd->bqkgd', masked_softmax(s, valid), vs.astype(jnp.float32))
        kvw = lax.dynamic_slice_in_dim(win_pad, pb[0] - win_pos0, NSA_WINDOW + blk, axis=1)
        kpos = pb[0] - NSA_WINDOW + jnp.arange(NSA_WINDOW + blk, dtype=jnp.int32)
        dist = pb[:, None] - kpos[None, :]
        valid = (dist >= 0) & (dist <= NSA_WINDOW) & (kpos[None, :] >= win_pos0)
        s = jnp.einsum('bqkgd,bnkd->bkgqn', qb, kvw[:, :, 0]).astype(jnp.float32) * scale
        o_win = jnp.einsum('bkgqn,bnkd->bqkgd', masked_softmax(s, valid), kvw[:, :, 1].astype(jnp.float32))
        return o_cmp, o_slc, o_win

    qg = q.reshape(B, T, NSA_KV_HEADS, NSA_GROUP, HEAD_DIM)
    o_cmp, o_slc, o_win = lax.map(one, (to_chunks(qg, blk), pos.reshape(T // blk, blk)))
    gates = jax.nn.sigmoid((h @ w_gate).astype(jnp.float32)).reshape(B, T, NSA_KV_HEADS, NSA_GROUP, 3)
    o = (gates[..., 0:1] * from_chunks(o_cmp) + gates[..., 1:2] * from_chunks(o_slc)
         + gates[..., 2:3] * from_chunks(o_win))
    y = o.reshape(B, T, qw).astype(h.dtype) @ w_out
    return y, new_rows, win_all[:, win_all.shape[1] - keep:]


def moe_ffn(x, router, w_up, w_down):
    B, T, D = x.shape
    N = B * T
    xf = x.reshape(N, D)
    logits = (xf @ router).astype(jnp.float32)
    top_val, top_idx = lax.top_k(logits, TOP_K)
    gates = jax.nn.softmax(top_val, axis=-1)
    A = N * TOP_K
    e = top_idx.reshape(A).astype(jnp.int32)
    tok = jnp.arange(A, dtype=jnp.int32) // TOP_K
    order = jnp.argsort(e)
    e_s, tok_s, g_s = e[order], tok[order], gates.reshape(A)[order]
    counts = jnp.bincount(e, length=N_EXPERTS)
    start = jnp.cumsum(counts) - counts
    padded = (counts + MOE_BLOCK - 1) // MOE_BLOCK * MOE_BLOCK
    pend = jnp.cumsum(padded)
    pstart = pend - padded
    dest = pstart[e_s] + jnp.arange(A, dtype=jnp.int32) - start[e_s]
    n_blocks = (A + N_EXPERTS * (MOE_BLOCK - 1) + MOE_BLOCK - 1) // MOE_BLOCK
    R = n_blocks * MOE_BLOCK
    row_tok = jnp.full((R,), N, jnp.int32).at[dest].set(tok_s)
    row_gate = jnp.zeros((R,), jnp.float32).at[dest].set(g_s)
    block_expert = jnp.minimum(
        jnp.searchsorted(pend, jnp.arange(n_blocks, dtype=jnp.int32) * MOE_BLOCK, side='right'), N_EXPERTS - 1)
    xb = jnp.concatenate([xf, jnp.zeros((1, D), xf.dtype)], axis=0)[row_tok].reshape(n_blocks, MOE_BLOCK, D)

    def expert_block(args):
        xe, ei = args
        return swiglu(xe, w_up[ei], w_down[ei])

    out = lax.map(expert_block, (xb, block_expert)).reshape(R, D)
    y = jnp.zeros((N + 1, D), jnp.float32).at[row_tok].add(out.astype(jnp.float32) * row_gate[:, None])
    return y[:N].reshape(B, T, D).astype(x.dtype)


def setup_inputs(seed: int = 0) -> dict:
    key = jax.random.key(seed)
    ks = iter(jax.random.split(key, 64))
    D = D_MODEL

    def nrm(shape, scale):
        return jax.random.normal(next(ks), shape, jnp.float32) * scale

    def gain(shape):
        return 1.0 + 0.02 * jax.random.normal(next(ks), shape, jnp.float32)

    n_pages = PAST_LEN // PAGE_SIZE
    n_pool = (DEC_BATCH * n_pages * 5) // 4
    ret_in_w = 2 * RET_HEADS * RET_DK + 2 * RET_HEADS * RET_DV
    dil_in_w = len(DIL_GROUPS) * 3 * DIL_HEADS * HEAD_DIM
    nsa_in_w = NSA_HEADS * HEAD_DIM + 6 * NSA_KV_HEADS * HEAD_DIM
    page_table = jax.random.permutation(next(ks), n_pool)[:DEC_BATCH * n_pages].reshape(
        DEC_BATCH, n_pages).astype(jnp.int32)
    return {
        'x_prompt': nrm((BATCH, SEQ, D), 1.0),
        'x_sample': nrm((DEC_BATCH, DEC_SEQ, D), 1.0),
        'state_ret': nrm((N_RET_LAYERS, DEC_BATCH, RET_HEADS, RET_DK, RET_DV), 0.05),
        'state_hgrn': nrm((N_HGRN_LAYERS, DEC_BATCH, HG_HEADS, HG_DK, HG_DV), 0.1),
        'cache_dil_w128': nrm((N_DIL_LAYERS, DEC_BATCH, min(DIL_GROUPS[0][0], PAST_LEN), 2, DIL_HEADS, HEAD_DIM), 1.0),
        'cache_dil_w512': nrm((N_DIL_LAYERS, DEC_BATCH, min(DIL_GROUPS[1][0], PAST_LEN), 2, DIL_HEADS, HEAD_DIM), 1.0),
        'cache_dil_w2048': nrm((N_DIL_LAYERS, DEC_BATCH, min(DIL_GROUPS[2][0], PAST_LEN), 2, DIL_HEADS, HEAD_DIM), 1.0),
        'cache_nsa_kv': nrm((N_NSA_LAYERS, n_pool, PAGE_SIZE, 4, NSA_KV_HEADS, HEAD_DIM), 1.0),
        'cache_nsa_win': nrm((N_NSA_LAYERS, DEC_BATCH, min(NSA_WINDOW, PAST_LEN), 2, NSA_KV_HEADS, HEAD_DIM), 1.0),
        'page_table': page_table,
        'norm_mix': gain((DEPTH, D)),
        'norm_ffn': gain((DEPTH, D)),
        'ret_w_in': nrm((N_RET_LAYERS, D, ret_in_w), D ** -0.5),
        'ret_gn_g': gain((N_RET_LAYERS, RET_HEADS * RET_DV)),
        'ret_gn_b': nrm((N_RET_LAYERS, RET_HEADS * RET_DV), 0.02),
        'ret_w_out': nrm((N_RET_LAYERS, RET_HEADS * RET_DV, D), (RET_HEADS * RET_DV) ** -0.5),
        'hg_w_in': nrm((N_HGRN_LAYERS, D, 4 * D), D ** -0.5),
        'hg_lower_bounds': nrm((DEPTH, HG_HEADS * HG_DK), 0.1),
        'hg_gn_g': gain((N_HGRN_LAYERS, HG_HEADS * HG_DV)),
        'hg_w_out': nrm((N_HGRN_LAYERS, HG_HEADS * HG_DV, D), (HG_HEADS * HG_DV) ** -0.5),
        'dil_w_in': nrm((N_DIL_LAYERS, D, dil_in_w), D ** -0.5),
        'dil_qn_g': gain((N_DIL_LAYERS, len(DIL_GROUPS), HEAD_DIM)),
        'dil_kn_g': gain((N_DIL_LAYERS, len(DIL_GROUPS), HEAD_DIM)),
        'dil_w_out': nrm((N_DIL_LAYERS, DIL_HEADS * HEAD_DIM, D), (DIL_HEADS * HEAD_DIM) ** -0.5),
        'nsa_w_in': nrm((N_NSA_LAYERS, D, nsa_in_w), D ** -0.5),
        'nsa_w_gate': nrm((N_NSA_LAYERS, D, NSA_HEADS * 3), D ** -0.5),
        'nsa_qn_g': gain((N_NSA_LAYERS, HEAD_DIM)),
        'nsa_kn_g': gain((N_NSA_LAYERS, 3, HEAD_DIM)),
        'nsa_cmp_pe': nrm((N_NSA_LAYERS, 2, NSA_CMP_LEN, HEAD_DIM), 0.02),
        'nsa_cmp_w1': nrm((N_NSA_LAYERS, 2, NSA_CMP_LEN * HEAD_DIM, NSA_CMP_HIDDEN), (NSA_CMP_LEN * HEAD_DIM) ** -0.5),
        'nsa_cmp_w2': nrm((N_NSA_LAYERS, 2, NSA_CMP_HIDDEN, HEAD_DIM), NSA_CMP_HIDDEN ** -0.5),
        'nsa_w_out': nrm((N_NSA_LAYERS, NSA_HEADS * HEAD_DIM, D), (NSA_HEADS * HEAD_DIM) ** -0.5),
        'ffn_w_up': nrm((N_DENSE_LAYERS, D, 2 * D_FF), D ** -0.5),
        'ffn_w_down': nrm((N_DENSE_LAYERS, D_FF, D), D_FF ** -0.5),
        'moe_router': nrm((N_MOE_LAYERS, D, N_EXPERTS), D ** -0.5),
        'moe_w_up': nrm((N_MOE_LAYERS, N_EXPERTS, D, 2 * D_FF_EXPERT), D ** -0.5),
        'moe_w_down': nrm((N_MOE_LAYERS, N_EXPERTS, D_FF_EXPERT, D), D_FF_EXPERT ** -0.5),
    }


def reference(x_prompt, x_sample, state_ret, state_hgrn, cache_dil_w128, cache_dil_w512, cache_dil_w2048,
              cache_nsa_kv, cache_nsa_win, page_table, norm_mix, norm_ffn,
              ret_w_in, ret_gn_g, ret_gn_b, ret_w_out,
              hg_w_in, hg_lower_bounds, hg_gn_g, hg_w_out,
              dil_w_in, dil_qn_g, dil_kn_g, dil_w_out,
              nsa_w_in, nsa_w_gate, nsa_qn_g, nsa_kn_g, nsa_cmp_pe, nsa_cmp_w1, nsa_cmp_w2, nsa_w_out,
              ffn_w_up, ffn_w_down, moe_router, moe_w_up, moe_w_down):
    xp, xs = x_prompt, x_sample
    Bp, Bs = xp.shape[0], xs.shape[0]
    dil_caches = (cache_dil_w128, cache_dil_w512, cache_dil_w2048)
    lb_all = jnp.cumsum(jax.nn.softmax(hg_lower_bounds.astype(jnp.float32), axis=0), axis=0)
    lb_all = lb_all - lb_all[0:1]
    ret_p, ret_s, hg_p, hg_s = [], [], [], []
    dil_p, dil_s = [[], [], []], [[], [], []]
    nsa_kv_p, nsa_kv_s, nsa_win_p, nsa_win_s = [], [], [], []
    for i in range(DEPTH):
        j = i // N_MIXERS
        hp = rms_norm(xp, norm_mix[i])
        hs = rms_norm(xs, norm_mix[i])
        kind = i % N_MIXERS
        if kind == 0:
            zero = jnp.zeros((Bp, RET_HEADS, RET_DK, RET_DV), jnp.float32)
            yp, sp = retention_mixer(hp, 0, zero, ret_w_in[j], ret_gn_g[j], ret_gn_b[j], ret_w_out[j])
            ys, ss = retention_mixer(hs, PAST_LEN, state_ret[j], ret_w_in[j], ret_gn_g[j], ret_gn_b[j], ret_w_out[j])
            ret_p.append(sp.astype(state_ret.dtype))
            ret_s.append(ss.astype(state_ret.dtype))
        elif kind == 1:
            zero = jnp.zeros((Bp, HG_HEADS, HG_DK, HG_DV), jnp.float32)
            yp, sp = hgrn2_mixer(hp, zero, hg_w_in[j], lb_all[i], hg_gn_g[j], hg_w_out[j])
            ys, ss = hgrn2_mixer(hs, state_hgrn[j], hg_w_in[j], lb_all[i], hg_gn_g[j], hg_w_out[j])
            hg_p.append(sp.astype(state_hgrn.dtype))
            hg_s.append(ss.astype(state_hgrn.dtype))
        elif kind == 2:
            yp, bp = dilated_mixer(hp, 0, None, dil_w_in[j], dil_qn_g[j], dil_kn_g[j], dil_w_out[j])
            ys, bs = dilated_mixer(hs, PAST_LEN, tuple(c[j] for c in dil_caches),
                                   dil_w_in[j], dil_qn_g[j], dil_kn_g[j], dil_w_out[j])
            for gi in range(len(DIL_GROUPS)):
                dil_p[gi].append(bp[gi])
                dil_s[gi].append(bs[gi])
        else:
            yp, kvp, wp = nsa_mixer(hp, 0, None, None, nsa_w_in[j], nsa_w_gate[j], nsa_qn_g[j], nsa_kn_g[j],
                                    nsa_cmp_pe[j], nsa_cmp_w1[j], nsa_cmp_w2[j], nsa_w_out[j])
            past = cache_nsa_kv[j][page_table]
            past = past.reshape((Bs, page_table.shape[1] * PAGE_SIZE) + past.shape[3:])
            ys, kvs, ws = nsa_mixer(hs, PAST_LEN, past, cache_nsa_win[j], nsa_w_in[j], nsa_w_gate[j], nsa_qn_g[j],
                                    nsa_kn_g[j], nsa_cmp_pe[j], nsa_cmp_w1[j], nsa_cmp_w2[j], nsa_w_out[j])
            nsa_kv_p.append(kvp)
            nsa_kv_s.append(kvs)
            nsa_win_p.append(wp)
            nsa_win_s.append(ws)
        xp = xp + yp
        xs = xs + ys
        hp = rms_norm(xp, norm_ffn[i])
        hs = rms_norm(xs, norm_ffn[i])
        if i % 2 == 0:
            xp = xp + swiglu(hp, ffn_w_up[i // 2], ffn_w_down[i // 2])
            xs = xs + swiglu(hs, ffn_w_up[i // 2], ffn_w_down[i // 2])
        else:
            xp = xp + moe_ffn(hp, moe_router[i // 2], moe_w_up[i // 2], moe_w_down[i // 2])
            xs = xs + moe_ffn(hs, moe_router[i // 2], moe_w_up[i // 2], moe_w_down[i // 2])
    return (xp, xs,
            jnp.stack(ret_p), jnp.stack(ret_s),
            jnp.stack(hg_p), jnp.stack(hg_s),
            jnp.stack(dil_p[0]), jnp.stack(dil_s[0]),
            jnp.stack(dil_p[1]), jnp.stack(dil_s[1]),
            jnp.stack(dil_p[2]), jnp.stack(dil_s[2]),
            jnp.stack(nsa_kv_p), jnp.stack(nsa_kv_s),
            jnp.stack(nsa_win_p), jnp.stack(nsa_win_s))
```

```python
import functools
import math

import numpy as np
import jax
import jax.numpy as jnp
from jax import lax
from jax.experimental import pallas as pl
from jax.experimental.pallas import tpu as pltpu

F32 = jnp.float32
BF16 = jnp.bfloat16
I32 = jnp.int32

EPS = 1e-6
NEG = -1e30
ROPE_THETA = 10000.0
LANES = 128
VMEM_LIMIT = 56 * 1024 * 1024

HEAD_DIM = 128
RET_HEADS, RET_DK, RET_DV, RET_CHUNK = 4, 256, 512, 128
HG_HEADS, HG_DK, HG_CHUNK = 8, 128, 64
DIL_HEADS = 4
DIL_GROUPS = ((128, 1), (512, 4), (2048, 16))
DIL_QBLOCK = 128
NSA_HEADS, NSA_KV_HEADS, NSA_GROUP = 8, 2, 4
NSA_CMP_LEN, NSA_CMP_STRIDE = 32, 16
NSA_SLC_LEN, NSA_TOP_N, NSA_WINDOW = 64, 16, 512
PAGE = 128
N_EXPERTS = 8
MOE_ROWS = 512


def _cp(*sem):
    return pltpu.CompilerParams(dimension_semantics=sem, vmem_limit_bytes=VMEM_LIMIT)


def _tile(n, pref):
    t = pref
    while n % t:
        t //= 2
    return t


def _dot(a, b):
    return jnp.dot(a, b, preferred_element_type=F32)


def _dot_nt(a, b):
    return lax.dot_general(a, b, (((1,), (1,)), ((), ())), preferred_element_type=F32)


def _dot_tn(a, b):
    return lax.dot_general(a, b, (((0,), (0,)), ((), ())), preferred_element_type=F32)


def _split3(x):
    hi = x.astype(BF16)
    r = x - hi.astype(F32)
    mid = r.astype(BF16)
    lo = (r - mid.astype(F32)).astype(BF16)
    return hi, mid, lo


def _rms(x, g):
    return x * lax.rsqrt(jnp.mean(x * x, axis=-1, keepdims=True) + EPS) * g


def _sigmoid(x):
    return 1.0 / (1.0 + jnp.exp(-x))


def _silu(x):
    return x * _sigmoid(x)


def _norm_proj_kernel(x_ref, g_ref, w_ref, o_ref, xn_ref):
    @pl.when(pl.program_id(1) == 0)
    def _():
        xn_ref[...] = _rms(x_ref[...], g_ref[...]).astype(BF16)

    o_ref[...] = _dot(xn_ref[...], w_ref[...]).astype(o_ref.dtype)


def norm_proj(x, g, w, tn):
    M, D = x.shape
    N = w.shape[1]
    tm = _tile(M, 1024)
    return pl.pallas_call(
        _norm_proj_kernel,
        grid=(M // tm, N // tn),
        in_specs=[pl.BlockSpec((tm, D), lambda i, j: (i, 0)),
                  pl.BlockSpec((1, D), lambda i, j: (0, 0)),
                  pl.BlockSpec((D, tn), lambda i, j: (0, j))],
        out_specs=pl.BlockSpec((tm, tn), lambda i, j: (i, j)),
        out_shape=jax.ShapeDtypeStruct((M, N), F32),
        scratch_shapes=[pltpu.VMEM((tm, D), BF16)],
        compiler_params=_cp("parallel", "arbitrary"),
        name="norm_proj",
    )(x, g.reshape(1, D), w)


def _out_proj_kernel(a_ref, w_ref, x_ref, o_ref):
    o_ref[...] = x_ref[...] + _dot(a_ref[...].astype(BF16), w_ref[...])


def out_proj(a, w, x):
    M, K = a.shape
    D = w.shape[1]
    tm = _tile(M, 512)
    return pl.pallas_call(
        _out_proj_kernel,
        grid=(M // tm,),
        in_specs=[pl.BlockSpec((tm, K), lambda i: (i, 0)),
                  pl.BlockSpec((K, D), lambda i: (0, 0)),
                  pl.BlockSpec((tm, D), lambda i: (i, 0))],
        out_specs=pl.BlockSpec((tm, D), lambda i: (i, 0)),
        out_shape=jax.ShapeDtypeStruct((M, D), F32),
        compiler_params=_cp("parallel"),
        name="out_proj",
    )(a, w, x)


def _ffn_dense_kernel(x_ref, g_ref, wg_ref, wu_ref, wd_ref, o_ref, xn_ref, acc_ref, *, n_f):
    f = pl.program_id(1)

    @pl.when(f == 0)
    def _():
        xn_ref[...] = _rms(x_ref[...], g_ref[...]).astype(BF16)
        acc_ref[...] = jnp.zeros_like(acc_ref)

    xn = xn_ref[...]
    hg = _dot(xn, wg_ref[...])
    hu = _dot(xn, wu_ref[...])
    acc_ref[...] += _dot((_silu(hg) * hu).astype(BF16), wd_ref[...])

    @pl.when(f == n_f - 1)
    def _():
        o_ref[...] = x_ref[...] + acc_ref[...]


def ffn_dense(x, g, w_up, w_down):
    M, D = x.shape
    F = w_down.shape[0]
    tm = _tile(M, 512)
    tf = F // 2 if (F // 2) % LANES == 0 else F
    n_f = F // tf
    return pl.pallas_call(
        functools.partial(_ffn_dense_kernel, n_f=n_f),
        grid=(M // tm, n_f),
        in_specs=[pl.BlockSpec((tm, D), lambda i, f: (i, 0)),
                  pl.BlockSpec((1, D), lambda i, f: (0, 0)),
                  pl.BlockSpec((D, tf), lambda i, f: (0, f)),
                  pl.BlockSpec((D, tf), lambda i, f: (0, n_f + f)),
                  pl.BlockSpec((tf, D), lambda i, f: (f, 0))],
        out_specs=pl.BlockSpec((tm, D), lambda i, f: (i, 0)),
        out_shape=jax.ShapeDtypeStruct((M, D), F32),
        scratch_shapes=[pltpu.VMEM((tm, D), BF16), pltpu.VMEM((tm, D), F32)],
        compiler_params=_cp("parallel", "arbitrary"),
        name="ffn_dense",
    )(x, g.reshape(1, D), w_up, w_up, w_down)


def _ret_kernel(*refs, C, has_s0, has_prev):
    q_ref, k_ref, v_ref, g_ref, cos_ref, sin_ref, gng_ref, gnb_ref = refs[:8]
    n = 8
    s0_ref = None
    if has_s0:
        s0_ref = refs[n]
        n += 1
    if has_prev:
        n += 1
    y_ref, s_ref = refs[n], refs[n + 1]
    c = pl.program_id(1)

    @pl.when(c == 0)
    def _():
        if has_s0:
            s_ref[...] = s0_ref[...]
        else:
            s_ref[...] = jnp.zeros_like(s_ref)

    cos, sin = cos_ref[...], sin_ref[...]
    half = RET_DK // 2
    t = lax.broadcasted_iota(I32, (C, 1), 0).astype(F32)
    diff = (lax.broadcasted_iota(I32, (C, C), 0) - lax.broadcasted_iota(I32, (C, C), 1)).astype(F32)

    def rot(x):
        x1, x2 = x[:, :half], x[:, half:]
        return jnp.concatenate([x1 * cos - x2 * sin, x2 * cos + x1 * sin], axis=1)

    for h in range(RET_HEADS):
        lg = math.log(1.0 - 2.0 ** (-5.0 - h))
        qr = rot(q_ref[:, h * RET_DK:(h + 1) * RET_DK])
        kr = rot(k_ref[:, h * RET_DK:(h + 1) * RET_DK]) * RET_DK ** -0.5
        v = v_ref[:, h * RET_DV:(h + 1) * RET_DV].astype(BF16)
        decay = jnp.where(diff >= 0, jnp.exp(lg * jnp.maximum(diff, 0.0)), 0.0)
        sc = _dot_nt(qr.astype(BF16), kr.astype(BF16)) * decay
        S = s_ref[0, h]
        o = _dot(sc.astype(BF16), v) + _dot((qr * jnp.exp(lg * (t + 1.0))).astype(BF16), S.astype(BF16))
        kd = (kr * jnp.exp(lg * (C - 1.0 - t))).astype(BF16)
        s_ref[0, h] = math.exp(lg * C) * S + _dot_tn(kd, v)
        mu = jnp.mean(o, axis=-1, keepdims=True)
        var = jnp.mean(jnp.square(o - mu), axis=-1, keepdims=True)
        sl = slice(h * RET_DV, (h + 1) * RET_DV)
        on = (o - mu) * lax.rsqrt(var + EPS) * gng_ref[:, sl] + gnb_ref[:, sl]
        y_ref[:, sl] = _silu(g_ref[:, sl]) * on


def ret_core(proj, B, T, row0, cos, sin, tab0, gng, gnb, s0, y_prev):
    M = proj.shape[0]
    C = RET_CHUNK if T % RET_CHUNK == 0 else T
    nC = T // C
    r0, t0 = row0 // C, tab0 // C
    qkw, vw = RET_HEADS * RET_DK, RET_HEADS * RET_DV
    row = lambda b, c: r0 + b * nC + c
    in_specs = [pl.BlockSpec((C, qkw), lambda b, c: (row(b, c), 0)),
                pl.BlockSpec((C, qkw), lambda b, c: (row(b, c), 1)),
                pl.BlockSpec((C, vw), lambda b, c: (row(b, c), 2 * qkw // vw)),
                pl.BlockSpec((C, vw), lambda b, c: (row(b, c), 2 * qkw // vw + 1)),
                pl.BlockSpec((C, RET_DK // 2), lambda b, c: (t0 + c, 0)),
                pl.BlockSpec((C, RET_DK // 2), lambda b, c: (t0 + c, 0)),
                pl.BlockSpec((1, vw), lambda b, c: (0, 0)),
                pl.BlockSpec((1, vw), lambda b, c: (0, 0))]
    args = [proj, proj, proj, proj, cos, sin, gng.reshape(1, vw), gnb.reshape(1, vw)]
    sblock = (1, RET_HEADS, RET_DK, RET_DV)
    if s0 is not None:
        in_specs.append(pl.BlockSpec(sblock, lambda b, c: (b, 0, 0, 0)))
        args.append(s0)
    aliases = {}
    if y_prev is not None:
        in_specs.append(pl.BlockSpec(memory_space=pl.ANY))
        aliases = {len(args): 0}
        args.append(y_prev)
    return pl.pallas_call(
        functools.partial(_ret_kernel, C=C, has_s0=s0 is not None, has_prev=y_prev is not None),
        grid=(B, nC),
        in_specs=in_specs,
        out_specs=[pl.BlockSpec((C, vw), lambda b, c: (row(b, c), 0)),
                   pl.BlockSpec(sblock, lambda b, c: (b, 0, 0, 0))],
        out_shape=[jax.ShapeDtypeStruct((M, vw), F32),
                   jax.ShapeDtypeStruct((B,) + sblock[1:], F32)],
        input_output_aliases=aliases,
        compiler_params=_cp("parallel", "arbitrary"),
        name="ret_core",
    )(*args)


def _hgrn_levels(C):
    out, m = [], C // 2
    while m >= 1:
        out.append(m)
        m //= 2
    return out


def _hgrn_mats(C):
    t = np.arange(C)[:, None]
    u = np.arange(C)[None, :]
    mats = [(u <= t).astype(np.float32)]
    for m in _hgrn_levels(C):
        r = 2 * m * (t // (2 * m)) + m - 1
        a = ((u > r) & (u <= t)).astype(np.float32) - ((u > t) & (u <= r)).astype(np.float32)
        mats.append(a)
    return np.stack(mats)


def _hgrn_kernel(*refs, C, has_s0, has_prev):
    q_ref, f_ref, i_ref, g_ref, lb_ref, gng_ref, a_ref = refs[:7]
    n = 7
    s0_ref = None
    if has_s0:
        s0_ref = refs[n]
        n += 1
    if has_prev:
        n += 1
    y_ref, s_ref = refs[n], refs[n + 1]
    c = pl.program_id(1)

    @pl.when(c == 0)
    def _():
        if has_s0:
            s_ref[...] = s0_ref[...]
        else:
            s_ref[...] = jnp.zeros_like(s_ref)

    lb = lb_ref[...]
    f = lb + (1.0 - lb) * _sigmoid(f_ref[...])
    logf = jnp.log(f)
    kk = 1.0 - f
    q = _silu(q_ref[...])
    v = i_ref[...].astype(BF16)
    parts = _split3(logf)

    def amul(idx):
        a = a_ref[idx]
        return _dot(a, parts[0]) + _dot(a, parts[1]) + _dot(a, parts[2])

    bcum = amul(0)
    blast = bcum[C - 1:C, :]
    qb = (q * jnp.exp(bcum)).astype(BF16)
    kst = (kk * jnp.exp(blast - bcum)).astype(BF16)
    eb = jnp.exp(blast)

    ti = lax.broadcasted_iota(I32, (C, C), 0)
    si = lax.broadcasted_iota(I32, (C, C), 1)
    terms = [(q.astype(BF16), kk.astype(BF16), ti == si)]
    for li, m in enumerate(_hgrn_levels(C)):
        dq = amul(1 + li)
        qt = (q * jnp.exp(jnp.minimum(dq, 0.0))).astype(BF16)
        kt = (kk * jnp.exp(jnp.minimum(-dq, 0.0))).astype(BF16)
        blk = 2 * m
        mask = ((ti // blk) == (si // blk)) & ((ti % blk) >= m) & ((si % blk) < m)
        terms.append((qt, kt, mask))

    eye = (lax.broadcasted_iota(I32, (HG_DK, HG_DK), 0) == lax.broadcasted_iota(I32, (HG_DK, HG_DK), 1))
    for h in range(HG_HEADS):
        sl = slice(h * HG_DK, (h + 1) * HG_DK)
        sc = jnp.zeros((C, C), F32)
        for qt, kt, mask in terms:
            sc = sc + jnp.where(mask, _dot_nt(qt[:, sl], kt[:, sl]), 0.0)
        S = s_ref[0, h]
        vh = v[:, sl]
        o = _dot(sc.astype(BF16), vh) + _dot(qb[:, sl], S.astype(BF16))
        ecol = jnp.sum(jnp.where(eye, eb[:, sl], 0.0), axis=1, keepdims=True)
        s_ref[0, h] = ecol * S + _dot_tn(kst[:, sl], vh)
        on = o * lax.rsqrt(jnp.mean(o * o, axis=-1, keepdims=True) + EPS)
        y_ref[:, sl] = on * gng_ref[:, sl] * _sigmoid(g_ref[:, sl])


def hgrn_core(proj, B, T, row0, lb, gng, s0, y_prev):
    M = proj.shape[0]
    W = HG_HEADS * HG_DK
    C = HG_CHUNK if T % HG_CHUNK == 0 else T
    nC = T // C
    r0 = row0 // C
    row = lambda b, c: r0 + b * nC + c
    amats = jnp.asarray(_hgrn_mats(C), BF16)
    col = lambda j: pl.BlockSpec((C, W), lambda b, c: (row(b, c), j))
    in_specs = [col(j) for j in range(4)]
    in_specs += [pl.BlockSpec((1, W), lambda b, c: (0, 0)),
                 pl.BlockSpec((1, W), lambda b, c: (0, 0)),
                 pl.BlockSpec(amats.shape, lambda b, c: (0, 0, 0))]
    args = [proj, proj, proj, proj, lb.reshape(1, W), gng.reshape(1, W), amats]
    sblock = (1, HG_HEADS, HG_DK, HG_DK)
    if s0 is not None:
        in_specs.append(pl.BlockSpec(sblock, lambda b, c: (b, 0, 0, 0)))
        args.append(s0)
    aliases = {}
    if y_prev is not None:
        in_specs.append(pl.BlockSpec(memory_space=pl.ANY))
        aliases = {len(args): 0}
        args.append(y_prev)
    return pl.pallas_call(
        functools.partial(_hgrn_kernel, C=C, has_s0=s0 is not None, has_prev=y_prev is not None),
        grid=(B, nC),
        in_specs=in_specs,
        out_specs=[pl.BlockSpec((C, W), lambda b, c: (row(b, c), 0)),
                   pl.BlockSpec(sblock, lambda b, c: (b, 0, 0, 0))],
        out_shape=[jax.ShapeDtypeStruct((M, W), F32),
                   jax.ShapeDtypeStruct((B,) + sblock[1:], F32)],
        input_output_aliases=aliases,
        compiler_params=_cp("parallel", "arbitrary"),
        name="hgrn_core",
    )(*args)


def _prep_kernel(x_ref, cos_ref, sin_ref, gain_ref, *o_refs, flags, outs):
    cos, sin = cos_ref[...], sin_ref[...]
    for (lo, hi), o_ref in zip(outs, o_refs):
        for j in range(lo // LANES, hi // LANES):
            xj = x_ref[:, j * LANES:(j + 1) * LANES]
            if flags[j]:
                y = _rms(xj, gain_ref[j:j + 1, :])
                xj = y * cos + pltpu.roll(y, LANES // 2, 1) * sin
            o_ref[:, j * LANES - lo:(j + 1) * LANES - lo] = xj.astype(o_ref.dtype)


def prep_heads(x, cos, sin, gains, flags, outs):
    M, W = x.shape
    tm = _tile(M, 256)
    nb = W // LANES
    return pl.pallas_call(
        functools.partial(_prep_kernel, flags=tuple(flags), outs=tuple(outs)),
        grid=(M // tm,),
        in_specs=[pl.BlockSpec((tm, W), lambda i: (i, 0)),
                  pl.BlockSpec((tm, LANES), lambda i: (i, 0)),
                  pl.BlockSpec((tm, LANES), lambda i: (i, 0)),
                  pl.BlockSpec((nb, LANES), lambda i: (0, 0))],
        out_specs=[pl.BlockSpec((tm, hi - lo), lambda i: (i, 0)) for lo, hi in outs],
        out_shape=[jax.ShapeDtypeStruct((M, hi - lo), F32) for lo, hi in outs],
        compiler_params=_cp("parallel"),
        name="prep_heads",
    )(x, cos, sin, gains)


def _softmax_head(s, valid, v):
    s = jnp.where(valid, s, NEG)
    m = jnp.max(s, axis=-1, keepdims=True)
    p = jnp.exp(s - m)
    l = jnp.sum(p, axis=-1, keepdims=True)
    o = _dot(p.astype(BF16), v) / l
    return o, m + jnp.log(l)


def _dil_prompt_kernel(q_ref, kc_ref, kp_ref, vc_ref, vp_ref, o_ref, l_ref, *, tq, reach):
    qi = pl.program_id(2)
    scale = HEAD_DIM ** -0.5
    qpos = qi * tq + lax.broadcasted_iota(I32, (tq, 2 * tq), 0)
    kpos = (qi - 1) * tq + lax.broadcasted_iota(I32, (tq, 2 * tq), 1)
    delta = qpos - kpos
    valid = (delta >= 0) & (delta <= reach) & (kpos >= 0)
    lane = lax.broadcasted_iota(I32, (tq, LANES), 1)
    lse_all = jnp.zeros((tq, LANES), F32)
    for h in range(DIL_HEADS):
        sl = slice(h * HEAD_DIM, (h + 1) * HEAD_DIM)
        q = q_ref[:, sl].astype(BF16)
        k = jnp.concatenate([kp_ref[:, sl], kc_ref[:, sl]], axis=0).astype(BF16)
        v = jnp.concatenate([vp_ref[:, sl], vc_ref[:, sl]], axis=0).astype(BF16)
        o, lse = _softmax_head(_dot_nt(q, k) * scale, valid, v)
        o_ref[:, sl] = o
        lse_all = jnp.where(lane == h, lse, lse_all)
    l_ref[...] = lse_all


def dil_prompt(pp, gi, B, T):
    window, d = DIL_GROUPS[gi]
    M, W = pp.shape
    HW = DIL_HEADS * HEAD_DIM
    Tc = T // d
    tq = min(DIL_QBLOCK, Tc)
    nq = Tc // tq
    cpr = W // HW
    ppv = pp.reshape(M // d, d * W)

    def blk(off, prev):
        return pl.BlockSpec(
            (tq, HW),
            lambda b, r, qi: (b * nq + (jnp.maximum(qi - 1, 0) if prev else qi), r * cpr + gi * 3 + off))

    o, l = pl.pallas_call(
        functools.partial(_dil_prompt_kernel, tq=tq, reach=window // d),
        grid=(B, d, nq),
        in_specs=[blk(0, False), blk(1, False), blk(1, True), blk(2, False), blk(2, True)],
        out_specs=[pl.BlockSpec((tq, HW), lambda b, r, qi: (b * nq + qi, r)),
                   pl.BlockSpec((tq, LANES), lambda b, r, qi: (b * nq + qi, r))],
        out_shape=[jax.ShapeDtypeStruct((M // d, d * HW), F32),
                   jax.ShapeDtypeStruct((M // d, d * LANES), F32)],
        compiler_params=_cp("parallel", "parallel", "arbitrary"),
        name=f"dil_prompt_{gi}",
    )(ppv, ppv, ppv, ppv, ppv)
    return o.reshape(M, HW), l.reshape(M, LANES)


def _dil_decode_kernel(q_ref, kn_ref, vn_ref, buf_ref, op_ref, lp_ref, o_ref, l_ref, nb_ref, *, L, Ts, d, window):
    del op_ref, lp_ref
    HW = DIL_HEADS * HEAD_DIM
    scale = HEAD_DIM ** -0.5
    t = lax.broadcasted_iota(I32, (Ts, L + Ts), 0)
    i = lax.broadcasted_iota(I32, (Ts, L + Ts), 1)
    delta = L + t - i
    valid = (delta >= 0) & (delta % d == 0) & (delta <= window)
    lane = lax.broadcasted_iota(I32, (Ts, LANES), 1)
    lse_all = jnp.zeros((Ts, LANES), F32)
    for h in range(DIL_HEADS):
        sl = slice(h * HEAD_DIM, (h + 1) * HEAD_DIM)
        sv = slice(HW + h * HEAD_DIM, HW + (h + 1) * HEAD_DIM)
        q = q_ref[:, sl].astype(BF16)
        k = jnp.concatenate([buf_ref[0, :, sl], kn_ref[:, sl]], axis=0).astype(BF16)
        v = jnp.concatenate([buf_ref[0, :, sv], vn_ref[:, sl]], axis=0).astype(BF16)
        o, lse = _softmax_head(_dot_nt(q, k) * scale, valid, v)
        o_ref[:, sl] = o
        lse_all = jnp.where(lane == h, lse, lse_all)
    l_ref[...] = lse_all
    nb_ref[0, 0:L - Ts, :] = buf_ref[0, Ts:L, :]
    nb_ref[0, L - Ts:L, 0:HW] = kn_ref[...]
    nb_ref[0, L - Ts:L, HW:2 * HW] = vn_ref[...]


def dil_decode(pp, gi, B, Ts, row0, buf, o_prev, l_prev):
    window, d = DIL_GROUPS[gi]
    M, W = pp.shape
    HW = DIL_HEADS * HEAD_DIM
    L = buf.shape[1]
    r0 = row0 // Ts
    col = lambda off: pl.BlockSpec((Ts, HW), lambda b: (r0 + b, gi * 3 + off))
    return pl.pallas_call(
        functools.partial(_dil_decode_kernel, L=L, Ts=Ts, d=d, window=window),
        grid=(B,),
        in_specs=[col(0), col(1), col(2),
                  pl.BlockSpec((1, L, 2 * HW), lambda b: (b, 0, 0)),
                  pl.BlockSpec(memory_space=pl.ANY), pl.BlockSpec(memory_space=pl.ANY)],
        out_specs=[pl.BlockSpec((Ts, HW), lambda b: (r0 + b, 0)),
                   pl.BlockSpec((Ts, LANES), lambda b: (r0 + b, 0)),
                   pl.BlockSpec((1, L, 2 * HW), lambda b: (b, 0, 0))],
        out_shape=[jax.ShapeDtypeStruct((M, HW), F32),
                   jax.ShapeDtypeStruct((M, LANES), F32),
                   jax.ShapeDtypeStruct((B, L, 2 * HW), F32)],
        input_output_aliases={4: 0, 5: 1},
        compiler_params=_cp("parallel"),
        name=f"dil_decode_{gi}",
    )(pp, pp, pp, buf, o_prev, l_prev)


def _dil_merge_kernel(o0, o1, o2, l0, l1, l2, out_ref):
    ls = [l0[...], l1[...], l2[...]]
    m = jnp.maximum(jnp.maximum(ls[0], ls[1]), ls[2])
    es = [jnp.exp(l - m) for l in ls]
    den = es[0] + es[1] + es[2]
    ws = [e / den for e in es]
    os_ = [o0, o1, o2]
    for h in range(DIL_HEADS):
        sl = slice(h * HEAD_DIM, (h + 1) * HEAD_DIM)
        acc = ws[0][:, h:h + 1] * os_[0][:, sl]
        for g in (1, 2):
            acc = acc + ws[g][:, h:h + 1] * os_[g][:, sl]
        out_ref[:, sl] = acc.astype(BF16)


def dil_merge(os_, ls):
    M, HW = os_[0].shape
    tm = _tile(M, 512)
    return pl.pallas_call(
        _dil_merge_kernel,
        grid=(M // tm,),
        in_specs=[pl.BlockSpec((tm, HW), lambda i: (i, 0))] * 3 + [pl.BlockSpec((tm, LANES), lambda i: (i, 0))] * 3,
        out_specs=pl.BlockSpec((tm, HW), lambda i: (i, 0)),
        out_shape=jax.ShapeDtypeStruct((M, HW), BF16),
        compiler_params=_cp("parallel"),
        name="dil_merge",
    )(*os_, *ls)


def _nsa_compress_kernel(tab_ref, pg_ref, pe_ref, w1_ref, w2_ref, o_ref, hs_ref, *, n_pages, ncmp):
    del tab_ref
    p = pl.program_id(1)
    hb = PAGE // NSA_CMP_STRIDE
    ncol = 2 * NSA_KV_HEADS
    rowlen = 4 * NSA_KV_HEADS * HEAD_DIM
    for c in range(ncol):
        for j in range(NSA_CMP_STRIDE):
            hs_ref[c, pl.ds(pl.multiple_of(p * hb, hb), hb), j * HEAD_DIM:(j + 1) * HEAD_DIM] = (
                pg_ref[0, :, j * rowlen + c * HEAD_DIM:j * rowlen + (c + 1) * HEAD_DIM])

    @pl.when(p == n_pages - 1)
    def _():
        half = NSA_CMP_STRIDE * HEAD_DIM
        R = n_pages * hb
        rowi = lax.broadcasted_iota(I32, (R, HEAD_DIM), 0)
        for c in range(ncol):
            s = c // NSA_KV_HEADS
            H = hs_ref[c]
            a = _dot((H + pe_ref[s, 0:1, :]).astype(BF16), w1_ref[s, 0:half, :])
            bm = _dot((H + pe_ref[s, 1:2, :]).astype(BF16), w1_ref[s, half:2 * half, :])
            pre = a + pltpu.roll(bm, R - 1, 0)
            out = _dot(_silu(pre).astype(BF16), w2_ref[s])
            o_ref[0, c] = jnp.where(rowi < ncmp, out, 0.0).astype(BF16)


def nsa_compress(pages, table, B, n_pages, pe, w1, w2):
    hb = PAGE // NSA_CMP_STRIDE
    R = n_pages * hb
    ncmp = (n_pages * PAGE - NSA_CMP_LEN) // NSA_CMP_STRIDE + 1
    half = NSA_CMP_STRIDE * HEAD_DIM
    pv = pages.reshape(pages.shape[0], hb, NSA_CMP_STRIDE * pages.shape[2])
    grid_spec = pltpu.PrefetchScalarGridSpec(
        num_scalar_prefetch=1,
        grid=(B, n_pages),
        in_specs=[pl.BlockSpec((1,) + pv.shape[1:], lambda b, p, tab: (tab[b * n_pages + p], 0, 0)),
                  pl.BlockSpec((2, 2, half), lambda b, p, tab: (0, 0, 0)),
                  pl.BlockSpec(w1.shape, lambda b, p, tab: (0, 0, 0)),
                  pl.BlockSpec(w2.shape, lambda b, p, tab: (0, 0, 0))],
        out_specs=pl.BlockSpec((1, 4, R, HEAD_DIM), lambda b, p, tab: (b, 0, 0, 0)),
        scratch_shapes=[pltpu.VMEM((4, R, half), F32)])
    return pl.pallas_call(
        functools.partial(_nsa_compress_kernel, n_pages=n_pages, ncmp=ncmp),
        grid_spec=grid_spec,
        out_shape=jax.ShapeDtypeStruct((B, 4, R, HEAD_DIM), BF16),
        compiler_params=_cp("parallel", "arbitrary"),
        name="nsa_compress",
    )(table, pv, pe.reshape(2, 2, half), w1, w2)


def _flash_update(m_ref, l_ref, a_ref, k, s, valid, v):
    s = jnp.where(valid, s, NEG)
    m_old = m_ref[k]
    m_new = jnp.maximum(m_old, jnp.max(s, axis=-1, keepdims=True))
    p = jnp.where(valid, jnp.exp(s - m_new), 0.0)
    alpha = jnp.exp(m_old - m_new)
    l_ref[k] = alpha * l_ref[k] + jnp.sum(p, axis=-1, keepdims=True)
    a_ref[k] = alpha * a_ref[k] + _dot(p.astype(BF16), v)
    m_ref[k] = m_new


def _nsa_attn_kernel(*refs, tq, sample, n_pages, n_wp, n_slc, ncmp, past):
    tab_ref, q_ref, gate_ref, kc_ref, c2s_ref, pg_ref, wpg_ref = refs[:7]
    n = 7
    xn_ref = xw_ref = None
    if sample:
        xn_ref, xw_ref = refs[7], refs[8]
        n = 10
    o_ref = refs[n]
    ocmp, sel, m_s, l_s, a_s, m_w, l_w, a_w = refs[n + 1:]
    del tab_ref
    qi, p = pl.program_id(1), pl.program_id(2)
    G, KVH, D = NSA_GROUP, NSA_KV_HEADS, HEAD_DIM
    R = G * tq
    scale = D ** -0.5
    pos0 = past if sample else qi * tq
    pos_r = pos0 + lax.broadcasted_iota(I32, (R, 1), 0) % tq
    pos_q = pos0 + lax.broadcasted_iota(I32, (tq, 1), 0)
    plast = n_pages - 1 if sample else (qi * tq + tq - 1) // PAGE

    def qstack(k):
        return jnp.concatenate([q_ref[:, (k * G + g) * D:(k * G + g + 1) * D] for g in range(G)],
                               axis=0).astype(BF16)

    def tile_rows(x):
        return jnp.concatenate([x] * G, axis=0)

    @pl.when(p == 0)
    def _():
        ncr = kc_ref.shape[2]
        nidx = lax.broadcasted_iota(I32, (R, ncr), 1)
        j = lax.broadcasted_iota(I32, (tq, LANES), 1)
        cur = pos_q // NSA_SLC_LEN
        forced = (j == 0) | (j == cur) | (j == cur - 1)
        avail = j <= cur
        c2s = c2s_ref[...]
        for k in range(KVH):
            s = _dot_nt(qstack(k), kc_ref[0, k]) * scale
            valid = (NSA_CMP_STRIDE * nidx + NSA_CMP_LEN - 1 <= pos_r) & (nidx < ncmp)
            s = jnp.where(valid, s, NEG)
            pr = jnp.where(valid, jnp.exp(s - jnp.max(s, axis=-1, keepdims=True)), 0.0)
            pc = pr / jnp.maximum(jnp.sum(pr, axis=-1, keepdims=True), 1e-30)
            ocmp[k] = _dot(pc.astype(BF16), kc_ref[0, KVH + k])
            psum = pc[0:tq]
            for g in range(1, G):
                psum = psum + pc[g * tq:(g + 1) * tq]
            ph, pm, plo = _split3(psum)
            imp = _dot(ph, c2s) + _dot(pm, c2s) + _dot(plo, c2s)
            score = jnp.where(forced, 1e4, jnp.where(avail, imp, -1.0))
            score = jnp.where(j < n_slc, score, -2.0)
            cnt = jnp.zeros((tq, LANES), F32)
            for jp in range(n_slc):
                col = score[:, jp:jp + 1]
                cnt = cnt + ((col > score) | ((col == score) & (jp < j))).astype(F32)
            sel[k] = (cnt < float(min(NSA_TOP_N, n_slc))).astype(F32)
        for ref in (m_s, m_w):
            ref[...] = jnp.full(ref.shape, NEG, F32)
        for ref in (l_s, a_s, l_w, a_w):
            ref[...] = jnp.zeros(ref.shape, F32)

    kpos = p * PAGE + lax.broadcasted_iota(I32, (R, PAGE), 1)
    dist = pos_r - kpos

    @pl.when(p <= plast)
    def _():
        j = lax.broadcasted_iota(I32, (tq, LANES), 1)
        lane = lax.broadcasted_iota(I32, (tq, PAGE), 1)
        per = PAGE // NSA_SLC_LEN
        for k in range(KVH):
            ks = pg_ref[0, :, (2 * KVH + k) * D:(2 * KVH + k + 1) * D].astype(BF16)
            vs = pg_ref[0, :, (3 * KVH + k) * D:(3 * KVH + k + 1) * D].astype(BF16)
            selk = sel[k]
            selrow = jnp.zeros((tq, PAGE), F32)
            for u in range(per):
                colu = jnp.sum(jnp.where(j == per * p + u, selk, 0.0), axis=-1, keepdims=True)
                selrow = jnp.where(lane // NSA_SLC_LEN == u, colu, selrow)
            valid = (tile_rows(selrow) > 0.5) & (dist >= 0)
            _flash_update(m_s, l_s, a_s, k, _dot_nt(qstack(k), ks) * scale, valid, vs)

    if sample:
        win_on = p >= n_pages - n_wp
    else:
        win_on = (p <= plast) & (p >= plast - n_wp)

    @pl.when(win_on)
    def _():
        valid = (dist >= 0) & (dist <= NSA_WINDOW)
        for k in range(KVH):
            kw = wpg_ref[0, :, k * D:(k + 1) * D].astype(BF16)
            vw = wpg_ref[0, :, (KVH + k) * D:(KVH + k + 1) * D].astype(BF16)
            _flash_update(m_w, l_w, a_w, k, _dot_nt(qstack(k), kw) * scale, valid, vw)

    @pl.when(p == n_pages - 1)
    def _():
        if sample:
            Ts = xn_ref.shape[0]
            jn = lax.broadcasted_iota(I32, (R, Ts), 1)
            causal = past + jn <= pos_r
            blk = past // NSA_SLC_LEN
            for k in range(KVH):
                ks = xn_ref[:, (2 * KVH + k) * D:(2 * KVH + k + 1) * D].astype(BF16)
                vs = xn_ref[:, (3 * KVH + k) * D:(3 * KVH + k + 1) * D].astype(BF16)
                selc = tile_rows(sel[k][:, blk:blk + 1])
                _flash_update(m_s, l_s, a_s, k, _dot_nt(qstack(k), ks) * scale, (selc > 0.5) & causal, vs)
                kw = xw_ref[:, k * D:(k + 1) * D].astype(BF16)
                vw = xw_ref[:, (KVH + k) * D:(KVH + k + 1) * D].astype(BF16)
                _flash_update(m_w, l_w, a_w, k, _dot_nt(qstack(k), kw) * scale, causal, vw)
        gs = _sigmoid(gate_ref[...])
        for k in range(KVH):
            o_s = a_s[k] / jnp.maximum(l_s[k], 1e-30)
            o_w = a_w[k] / jnp.maximum(l_w[k], 1e-30)
            o_c = ocmp[k]
            for g in range(G):
                hd = k * G + g
                rows = slice(g * tq, (g + 1) * tq)
                o = (gs[:, 3 * hd:3 * hd + 1] * o_c[rows] + gs[:, 3 * hd + 1:3 * hd + 2] * o_s[rows]
                     + gs[:, 3 * hd + 2:3 * hd + 3] * o_w[rows])
                o_ref[:, hd * D:(hd + 1) * D] = o


def _cmp_to_slc(nrows, ncmp, n_slc):
    i = np.arange(nrows)[:, None] * NSA_CMP_STRIDE
    j = np.arange(LANES)[None, :] * NSA_SLC_LEN
    ov = np.clip(np.minimum(i + NSA_CMP_LEN, j + NSA_SLC_LEN) - np.maximum(i, j), 0, None) / NSA_CMP_LEN
    ov = np.where((np.arange(nrows)[:, None] < ncmp) & (np.arange(LANES)[None, :] < n_slc), ov, 0.0)
    return jnp.asarray(ov, BF16)


def nsa_attn(q, gate, kc, pages, table, wpages, B, T, row0, past, xnew, xwin, o_prev):
    sample = xnew is not None
    M = q.shape[0]
    QW = NSA_HEADS * HEAD_DIM
    tq = T if sample else min(PAGE, T)
    nqb = T // tq
    n_pages = (past if sample else T) // PAGE
    L = past + T if sample else T
    n_slc = -(-L // NSA_SLC_LEN)
    ncmp = (L - NSA_CMP_LEN) // NSA_CMP_STRIDE + 1
    if sample:
        assert past % PAGE == 0 and T < NSA_CMP_STRIDE
        n_wp = wpages.shape[0] // B
    else:
        n_wp = NSA_WINDOW // PAGE
    r0 = row0 // tq
    R = NSA_GROUP * tq
    c2s = _cmp_to_slc(kc.shape[2], ncmp, n_slc)
    qrow = lambda b, qi: r0 + b * nqb + qi
    if sample:
        pidx = lambda b, qi, p, tab: (tab[b * n_pages + p], 0, 0)
        widx = lambda b, qi, p, tab: (b * n_wp + jnp.clip(p - (n_pages - n_wp), 0, n_wp - 1), 0, 0)
    else:
        plast = lambda qi: (qi * tq + tq - 1) // PAGE
        pidx = lambda b, qi, p, tab: (tab[b * n_pages + jnp.minimum(p, plast(qi))], 0, 0)
        widx = lambda b, qi, p, tab: (
            b * n_pages + jnp.clip(p, jnp.maximum(plast(qi) - n_wp, 0), plast(qi)), 0, 0)
    in_specs = [pl.BlockSpec((tq, QW), lambda b, qi, p, tab: (qrow(b, qi), 0)),
                pl.BlockSpec((tq, LANES), lambda b, qi, p, tab: (qrow(b, qi), 0)),
                pl.BlockSpec((1,) + kc.shape[1:], lambda b, qi, p, tab: (b, 0, 0, 0)),
                pl.BlockSpec(c2s.shape, lambda b, qi, p, tab: (0, 0)),
                pl.BlockSpec((1,) + pages.shape[1:], pidx),
                pl.BlockSpec((1,) + wpages.shape[1:], widx)]
    args = [table, q, gate, kc, c2s, pages, wpages]
    aliases = {}
    if sample:
        in_specs += [pl.BlockSpec((tq, xnew.shape[1]), lambda b, qi, p, tab: (qrow(b, qi), 0)),
                     pl.BlockSpec((tq, xwin.shape[1]), lambda b, qi, p, tab: (qrow(b, qi), 0)),
                     pl.BlockSpec(memory_space=pl.ANY)]
        args += [xnew, xwin, o_prev]
        aliases = {9: 0}
    vec = lambda: pltpu.VMEM((NSA_KV_HEADS, R, 1), F32)
    mat = lambda: pltpu.VMEM((NSA_KV_HEADS, R, HEAD_DIM), F32)
    grid_spec = pltpu.PrefetchScalarGridSpec(
        num_scalar_prefetch=1,
        grid=(B, nqb, n_pages),
        in_specs=in_specs,
        out_specs=pl.BlockSpec((tq, QW), lambda b, qi, p, tab: (qrow(b, qi), 0)),
        scratch_shapes=[mat(), pltpu.VMEM((NSA_KV_HEADS, tq, LANES), F32),
                        vec(), vec(), mat(), vec(), vec(), mat()])
    return pl.pallas_call(
        functools.partial(_nsa_attn_kernel, tq=tq, sample=sample, n_pages=n_pages, n_wp=n_wp,
                          n_slc=n_slc, ncmp=ncmp, past=past),
        grid_spec=grid_spec,
        out_shape=jax.ShapeDtypeStruct((M, QW), F32),
        input_output_aliases=aliases,
        compiler_params=_cp("parallel", "parallel", "arbitrary"),
        name="nsa_attn_sample" if sample else "nsa_attn_prompt",
    )(*args)


def _router_kernel(x_ref, g_ref, wr_ref, h_ref, e_ref, p_ref):
    xn = _rms(x_ref[...], g_ref[...])
    h_ref[...] = xn
    xh, xm, xl = _split3(xn)
    wh, wm, wl = _split3(wr_ref[...])
    logits = (_dot(xh, wh) + _dot(xh, wm) + _dot(xm, wh)
              + _dot(xh, wl) + _dot(xl, wh) + _dot(xm, wm))
    lane = lax.broadcasted_iota(I32, logits.shape, 1)
    lg = jnp.where(lane < N_EXPERTS, logits, -jnp.inf)
    m1 = jnp.max(lg, axis=-1, keepdims=True)
    i1 = jnp.min(jnp.where(lg == m1, lane, LANES), axis=-1, keepdims=True)
    lg2 = jnp.where(lane == i1, -jnp.inf, lg)
    m2 = jnp.max(lg2, axis=-1, keepdims=True)
    i2 = jnp.min(jnp.where(lg2 == m2, lane, LANES), axis=-1, keepdims=True)
    e = jnp.exp(m2 - m1)
    e_ref[...] = jnp.where(lane == 0, i1, jnp.where(lane == 1, i2, 0))
    p_ref[...] = jnp.where(lane == 0, 1.0 / (1.0 + e), jnp.where(lane == 1, e / (1.0 + e), 0.0))


def moe_router(x, g, router):
    M, D = x.shape
    tm = _tile(M, 512)
    wr = jnp.zeros((D, LANES), F32).at[:, :N_EXPERTS].set(router)
    return pl.pallas_call(
        _router_kernel,
        grid=(M // tm,),
        in_specs=[pl.BlockSpec((tm, D), lambda i: (i, 0)),
                  pl.BlockSpec((1, D), lambda i: (0, 0)),
                  pl.BlockSpec((D, LANES), lambda i: (0, 0))],
        out_specs=[pl.BlockSpec((tm, D), lambda i: (i, 0)),
                   pl.BlockSpec((tm, LANES), lambda i: (i, 0)),
                   pl.BlockSpec((tm, LANES), lambda i: (i, 0))],
        out_shape=[jax.ShapeDtypeStruct((M, D), F32),
                   jax.ShapeDtypeStruct((M, LANES), I32),
                   jax.ShapeDtypeStruct((M, LANES), F32)],
        compiler_params=_cp("parallel"),
        name="moe_router",
    )(x, g.reshape(1, D), wr)


def _row_copy(src_ref, dst_ref, sem, src_row, dst_row):
    return pltpu.make_async_copy(src_ref.at[pl.ds(src_row, 1)], dst_ref.at[pl.ds(dst_row, 1)], sem)


def _moe_ffn_kernel(be_ref, rt_ref, nu_ref, h_ref, wg_ref, wu_ref, wd_ref, y_ref,
                    xbuf, xb16, acc, sem, *, tm, n_f):
    del be_ref
    r, f = pl.program_id(0), pl.program_id(1)
    used = r < nu_ref[0]

    @pl.when(used & (f == 0))
    def _():
        def issue(i, carry):
            _row_copy(h_ref, xbuf, sem, rt_ref[r * tm + i], i).start()
            return carry

        lax.fori_loop(0, tm, issue, 0)

        def drain(i, carry):
            _row_copy(h_ref, xbuf, sem, 0, i).wait()
            return carry

        lax.fori_loop(0, tm, drain, 0)
        xb16[...] = xbuf[...].astype(BF16)
        acc[...] = jnp.zeros_like(acc)

    @pl.when(used)
    def _():
        x = xb16[...]
        hg = _dot(x, wg_ref[0])
        hu = _dot(x, wu_ref[0])
        acc[...] += _dot((_silu(hg) * hu).astype(BF16), wd_ref[0])

    @pl.when(used & (f == n_f - 1))
    def _():
        y_ref[...] = acc[...]

    @pl.when(jnp.logical_not(used) & (f == n_f - 1))
    def _():
        y_ref[...] = jnp.zeros_like(y_ref)


def moe_ffn(h, block_expert, row_tok, n_used, w_up, w_down):
    D = h.shape[1]
    F = w_down.shape[1]
    tm = MOE_ROWS
    nb = block_expert.shape[0]
    tf = 512
    n_f = F // tf
    live = lambda r, f, nu: jnp.where(r < nu[0], f, 0)
    grid_spec = pltpu.PrefetchScalarGridSpec(
        num_scalar_prefetch=3,
        grid=(nb, n_f),
        in_specs=[pl.BlockSpec(memory_space=pl.ANY),
                  pl.BlockSpec((1, D, tf), lambda r, f, be, rt, nu: (be[r], 0, live(r, f, nu))),
                  pl.BlockSpec((1, D, tf), lambda r, f, be, rt, nu: (be[r], 0, n_f + live(r, f, nu))),
                  pl.BlockSpec((1, tf, D), lambda r, f, be, rt, nu: (be[r], live(r, f, nu), 0))],
        out_specs=pl.BlockSpec((tm, D), lambda r, f, be, rt, nu: (r, 0)),
        scratch_shapes=[pltpu.VMEM((tm, D), F32), pltpu.VMEM((tm, D), BF16), pltpu.VMEM((tm, D), F32),
                        pltpu.SemaphoreType.DMA(())])
    return pl.pallas_call(
        functools.partial(_moe_ffn_kernel, tm=tm, n_f=n_f),
        grid_spec=grid_spec,
        out_shape=jax.ShapeDtypeStruct((nb * tm, D), F32),
        compiler_params=_cp("arbitrary", "arbitrary"),
        name="moe_ffn",
    )(block_expert, row_tok, n_used, h, w_up, w_up, w_down)


def _moe_combine_kernel(p0_ref, p1_ref, x_ref, g_ref, y_ref, o_ref, ybuf, sem, *, tm):
    i = pl.program_id(0)

    def issue(t, carry):
        _row_copy(y_ref, ybuf.at[0], sem, p0_ref[i * tm + t], t).start()
        _row_copy(y_ref, ybuf.at[1], sem, p1_ref[i * tm + t], t).start()
        return carry

    lax.fori_loop(0, tm, issue, 0)

    def drain(t, carry):
        _row_copy(y_ref, ybuf.at[0], sem, 0, t).wait()
        _row_copy(y_ref, ybuf.at[1], sem, 0, t).wait()
        return carry

    lax.fori_loop(0, tm, drain, 0)
    g = g_ref[...]
    o_ref[...] = x_ref[...] + (g[:, 0:1] * ybuf[0] + g[:, 1:2] * ybuf[1])


def moe_combine(x, gates, y, pos0, pos1):
    M, D = x.shape
    tm = _tile(M, 256)
    grid_spec = pltpu.PrefetchScalarGridSpec(
        num_scalar_prefetch=2,
        grid=(M // tm,),
        in_specs=[pl.BlockSpec((tm, D), lambda i, a, b: (i, 0)),
                  pl.BlockSpec((tm, LANES), lambda i, a, b: (i, 0)),
                  pl.BlockSpec(memory_space=pl.ANY)],
        out_specs=pl.BlockSpec((tm, D), lambda i, a, b: (i, 0)),
        scratch_shapes=[pltpu.VMEM((2, tm, D), F32), pltpu.SemaphoreType.DMA(())])
    return pl.pallas_call(
        functools.partial(_moe_combine_kernel, tm=tm),
        grid_spec=grid_spec,
        out_shape=jax.ShapeDtypeStruct((M, D), F32),
        compiler_params=_cp("arbitrary"),
        name="moe_combine",
    )(pos0, pos1, x, gates, y)


def moe_layer(x, g, router, w_up, w_down):
    M = x.shape[0]
    h, eidx, gates = moe_router(x, g, router)
    A = 2 * M
    e = eidx[:, :2].reshape(A)
    onehot = (e[:, None] == jnp.arange(N_EXPERTS, dtype=I32)[None, :]).astype(I32)
    csum = jnp.cumsum(onehot, axis=0)
    rank = jnp.sum(csum * onehot, axis=1) - 1
    counts = csum[-1]
    padded = (counts + MOE_ROWS - 1) // MOE_ROWS * MOE_ROWS
    pend = jnp.cumsum(padded)
    dest = ((pend - padded)[e] + rank).astype(I32)
    nb = (A + N_EXPERTS * (MOE_ROWS - 1) + MOE_ROWS - 1) // MOE_ROWS
    row_tok = jnp.zeros((nb * MOE_ROWS,), I32).at[dest].set(jnp.arange(A, dtype=I32) // 2)
    block_expert = jnp.minimum(
        jnp.searchsorted(pend, jnp.arange(nb, dtype=I32) * MOE_ROWS, side='right'), N_EXPERTS - 1).astype(I32)
    n_used = (pend[-1:] // MOE_ROWS).astype(I32)
    y = moe_ffn(h, block_expert, row_tok, n_used, w_up, w_down)
    dest2 = dest.reshape(M, 2)
    return moe_combine(x, gates, y, dest2[:, 0], dest2[:, 1])


def _rope_tables(pos, half):
    inv = ROPE_THETA ** (-jnp.arange(half, dtype=F32) / half)
    ang = pos.astype(F32)[:, None] * inv[None, :]
    return jnp.cos(ang), jnp.sin(ang)


def _row_rope_tables(Bp, T, Bs, Ts, past):
    pos = jnp.concatenate([jnp.tile(jnp.arange(T, dtype=I32), Bp),
                           jnp.tile(past + jnp.arange(Ts, dtype=I32), Bs)])
    cos, sin = _rope_tables(pos, HEAD_DIM // 2)
    return jnp.concatenate([cos, cos], axis=1), jnp.concatenate([-sin, sin], axis=1)


def kernel(x_prompt, x_sample, state_ret, state_hgrn, cache_dil_w128, cache_dil_w512, cache_dil_w2048,
           cache_nsa_kv, cache_nsa_win, page_table, norm_mix, norm_ffn,
           ret_w_in, ret_gn_g, ret_gn_b, ret_w_out,
           hg_w_in, hg_lower_bounds, hg_gn_g, hg_w_out,
           dil_w_in, dil_qn_g, dil_kn_g, dil_w_out,
           nsa_w_in, nsa_w_gate, nsa_qn_g, nsa_kn_g, nsa_cmp_pe, nsa_cmp_w1, nsa_cmp_w2, nsa_w_out,
           ffn_w_up, ffn_w_down, moe_router_w, moe_w_up, moe_w_down):
    Bp, T, D = x_prompt.shape
    Bs, Ts, _ = x_sample.shape
    Np, Ns = Bp * T, Bs * Ts
    M = Np + Ns
    n_pages = page_table.shape[1]
    past = n_pages * PAGE
    bf = lambda w: w.astype(BF16)

    x = jnp.concatenate([x_prompt.reshape(Np, D), x_sample.reshape(Ns, D)], axis=0)
    lb_all = jnp.cumsum(jax.nn.softmax(hg_lower_bounds.astype(F32), axis=0), axis=0)
    lb_all = lb_all - lb_all[0:1]

    proj = norm_proj(x, norm_mix[0], bf(ret_w_in[0]), 1024)
    cos_r, sin_r = _rope_tables(jnp.arange(max(T, past + Ts), dtype=I32), RET_DK // 2)
    y, ret_p = ret_core(proj, Bp, T, 0, cos_r, sin_r, 0, ret_gn_g[0], ret_gn_b[0], None, None)
    y, ret_s = ret_core(proj, Bs, Ts, Np, cos_r, sin_r, past, ret_gn_g[0], ret_gn_b[0], state_ret[0], y)
    x = out_proj(y, bf(ret_w_out[0]), x)
    x = ffn_dense(x, norm_ffn[0], bf(ffn_w_up[0]), bf(ffn_w_down[0]))

    proj = norm_proj(x, norm_mix[1], bf(hg_w_in[0]), 1024)
    y, hg_p = hgrn_core(proj, Bp, T, 0, lb_all[1], hg_gn_g[0], None, None)
    y, hg_s = hgrn_core(proj, Bs, Ts, Np, lb_all[1], hg_gn_g[0], state_hgrn[0], y)
    x = out_proj(y, bf(hg_w_out[0]), x)
    x = moe_layer(x, norm_ffn[1], moe_router_w[0], bf(moe_w_up[0]), bf(moe_w_down[0]))

    cos_h, sin_h = _row_rope_tables(Bp, T, Bs, Ts, past)
    HW = DIL_HEADS * HEAD_DIM
    dil_in_w = dil_w_in.shape[2]
    proj = norm_proj(x, norm_mix[2], bf(dil_w_in[0]), 3 * HW)
    ones = jnp.ones((DIL_HEADS, HEAD_DIM), F32)
    gains = jnp.concatenate([jnp.concatenate([ones * dil_qn_g[0, gi], ones * dil_kn_g[0, gi], ones], axis=0)
                             for gi in range(len(DIL_GROUPS))], axis=0)
    flags = ([True] * (2 * DIL_HEADS) + [False] * DIL_HEADS) * len(DIL_GROUPS)
    (pp,) = prep_heads(proj, cos_h, sin_h, gains, flags, [(0, dil_in_w)])
    caches = (cache_dil_w128, cache_dil_w512, cache_dil_w2048)
    os_, ls, dil_p, dil_s = [], [], [], []
    for gi, (window, _) in enumerate(DIL_GROUPS):
        o_g, l_g = dil_prompt(pp, gi, Bp, T)
        buf = caches[gi][0]
        L = buf.shape[1]
        o_g, l_g, nbuf = dil_decode(pp, gi, Bs, Ts, Np, buf.reshape(Bs, L, 2 * HW), o_g, l_g)
        os_.append(o_g)
        ls.append(l_g)
        keep = min(window, T)
        kv = pp[:Np, gi * 3 * HW + HW:(gi + 1) * 3 * HW].reshape(Bp, T, 2, DIL_HEADS, HEAD_DIM)
        dil_p.append(kv[:, T - keep:][None])
        dil_s.append(nbuf.reshape(buf.shape)[None])
    x = out_proj(dil_merge(os_, ls), bf(dil_w_out[0]), x)
    x = ffn_dense(x, norm_ffn[2], bf(ffn_w_up[1]), bf(ffn_w_down[1]))

    QW = NSA_HEADS * HEAD_DIM
    KW = NSA_KV_HEADS * HEAD_DIM
    w_in = jnp.concatenate([nsa_w_in[0], nsa_w_gate[0],
                            jnp.zeros((D, LANES - nsa_w_gate.shape[2]), F32)], axis=1)
    proj = norm_proj(x, norm_mix[3], bf(w_in), w_in.shape[1] // 3)
    ones_q = jnp.ones((NSA_HEADS, HEAD_DIM), F32)
    ones_k = jnp.ones((NSA_KV_HEADS, HEAD_DIM), F32)
    gains = jnp.concatenate([ones_q * nsa_qn_g[0]]
                            + [blk for s in range(3) for blk in (ones_k * nsa_kn_g[0, s], ones_k)]
                            + [jnp.ones((1, HEAD_DIM), F32)], axis=0)
    flags = [True] * NSA_HEADS + ([True] * NSA_KV_HEADS + [False] * NSA_KV_HEADS) * 3 + [False]
    q, new, win, gate = prep_heads(proj, cos_h, sin_h, gains, flags,
                                   [(0, QW), (QW, QW + 4 * KW), (QW + 4 * KW, QW + 6 * KW),
                                    (QW + 6 * KW, QW + 6 * KW + LANES)])
    w1, w2 = bf(nsa_cmp_w1[0]), bf(nsa_cmp_w2[0])
    ppages = new.reshape(M // PAGE, PAGE, 4 * KW)
    ptab = jnp.arange(Bp * (T // PAGE), dtype=I32)
    kc_p = nsa_compress(ppages, ptab, Bp, T // PAGE, nsa_cmp_pe[0], w1, w2)
    spages = cache_nsa_kv[0].reshape(cache_nsa_kv.shape[1], PAGE, 4 * KW)
    stab = page_table.reshape(-1).astype(I32)
    kc_s = nsa_compress(spages, stab, Bs, n_pages, nsa_cmp_pe[0], w1, w2)
    o = nsa_attn(q, gate, kc_p, ppages, ptab, win.reshape(M // PAGE, PAGE, 2 * KW),
                 Bp, T, 0, 0, None, None, None)
    wbuf = cache_nsa_win[0]
    Lw = wbuf.shape[1]
    o = nsa_attn(q, gate, kc_s, spages, stab, wbuf.reshape(Bs * (Lw // PAGE), PAGE, 2 * KW),
                 Bs, Ts, Np, past, new, win, o)
    x = out_proj(o, bf(nsa_w_out[0]), x)
    x = moe_layer(x, norm_ffn[3], moe_router_w[1], bf(moe_w_up[1]), bf(moe_w_down[1]))

    keep = min(NSA_WINDOW, T)
    win_p = win[:Np].reshape(Bp, T, 2, NSA_KV_HEADS, HEAD_DIM)[:, T - keep:]
    win_s = jnp.concatenate([wbuf, win[Np:].reshape(Bs, Ts, 2, NSA_KV_HEADS, HEAD_DIM)], axis=1)[:, Ts:]
    return (x[:Np].reshape(Bp, T, D), x[Np:].reshape(Bs, Ts, D),
            ret_p[None], ret_s[None], hg_p[None], hg_s[None],
            dil_p[0], dil_s[0], dil_p[1], dil_s[1], dil_p[2], dil_s[2],
            new[:Np].reshape(Bp, T, 4, NSA_KV_HEADS, HEAD_DIM)[None],
            new[Np:].reshape(Bs, Ts, 4, NSA_KV_HEADS, HEAD_DIM)[None],
            win_p[None], win_s[None])
```

```python
import functools
import math

import numpy as np
import jax
import jax.numpy as jnp
from jax import lax
from jax.experimental import pallas as pl
from jax.experimental.pallas import tpu as pltpu

F32 = jnp.float32
BF16 = jnp.bfloat16
I32 = jnp.int32

EPS = 1e-6
NEG = -1e30
ROPE_THETA = 10000.0
LANES = 128
VMEM_LIMIT = 56 * 1024 * 1024

HEAD_DIM = 128
RET_HEADS, RET_DK, RET_DV, RET_CHUNK = 4, 256, 512, 128
HG_HEADS, HG_DK, HG_CHUNK = 8, 128, 64
DIL_HEADS = 4
DIL_GROUPS = ((128, 1), (512, 4), (2048, 16))
DIL_QBLOCK = 128
NSA_HEADS, NSA_KV_HEADS, NSA_GROUP = 8, 2, 4
NSA_CMP_LEN, NSA_CMP_STRIDE = 32, 16
NSA_SLC_LEN, NSA_TOP_N, NSA_WINDOW = 64, 16, 512
PAGE = 128
N_EXPERTS = 8
MOE_ROWS = 512


def _cp(*sem):
    return pltpu.CompilerParams(dimension_semantics=sem, vmem_limit_bytes=VMEM_LIMIT)


def _tile(n, pref):
    t = pref
    while n % t:
        t //= 2
    return t


def _dot(a, b):
    return jnp.dot(a, b, preferred_element_type=F32)


def _dot_nt(a, b):
    return lax.dot_general(a, b, (((1,), (1,)), ((), ())), preferred_element_type=F32)


def _dot_tn(a, b):
    return lax.dot_general(a, b, (((0,), (0,)), ((), ())), preferred_element_type=F32)


def _split3(x):
    hi = x.astype(BF16)
    r = x - hi.astype(F32)
    mid = r.astype(BF16)
    lo = (r - mid.astype(F32)).astype(BF16)
    return hi, mid, lo


def _rms(x, g):
    return x * lax.rsqrt(jnp.mean(x * x, axis=-1, keepdims=True) + EPS) * g


def _sigmoid(x):
    return 1.0 / (1.0 + jnp.exp(-x))


def _silu(x):
    return x * _sigmoid(x)


def _norm_proj_kernel(x_ref, g_ref, w_ref, o_ref, xn_ref):
    @pl.when(pl.program_id(1) == 0)
    def _():
        xn_ref[...] = _rms(x_ref[...], g_ref[...]).astype(BF16)

    o_ref[...] = _dot(xn_ref[...], w_ref[...]).astype(o_ref.dtype)


def norm_proj(x, g, w, tn):
    M, D = x.shape
    N = w.shape[1]
    tm = _tile(M, 1024)
    return pl.pallas_call(
        _norm_proj_kernel,
        grid=(M // tm, N // tn),
        in_specs=[pl.BlockSpec((tm, D), lambda i, j: (i, 0)),
                  pl.BlockSpec((1, D), lambda i, j: (0, 0)),
                  pl.BlockSpec((D, tn), lambda i, j: (0, j))],
        out_specs=pl.BlockSpec((tm, tn), lambda i, j: (i, j)),
        out_shape=jax.ShapeDtypeStruct((M, N), F32),
        scratch_shapes=[pltpu.VMEM((tm, D), BF16)],
        compiler_params=_cp("parallel", "arbitrary"),
        name="norm_proj",
    )(x, g.reshape(1, D), w)


def _out_proj_kernel(a_ref, w_ref, x_ref, o_ref):
    o_ref[...] = x_ref[...] + _dot(a_ref[...].astype(BF16), w_ref[...])


def out_proj(a, w, x):
    M, K = a.shape
    D = w.shape[1]
    tm = _tile(M, 512)
    return pl.pallas_call(
        _out_proj_kernel,
        grid=(M // tm,),
        in_specs=[pl.BlockSpec((tm, K), lambda i: (i, 0)),
                  pl.BlockSpec((K, D), lambda i: (0, 0)),
                  pl.BlockSpec((tm, D), lambda i: (i, 0))],
        out_specs=pl.BlockSpec((tm, D), lambda i: (i, 0)),
        out_shape=jax.ShapeDtypeStruct((M, D), F32),
        compiler_params=_cp("parallel"),
        name="out_proj",
    )(a, w, x)


def _ffn_dense_kernel(x_ref, g_ref, wg_ref, wu_ref, wd_ref, o_ref, xn_ref, acc_ref, *, n_f):
    f = pl.program_id(1)

    @pl.when(f == 0)
    def _():
        xn_ref[...] = _rms(x_ref[...], g_ref[...]).astype(BF16)
        acc_ref[...] = jnp.zeros_like(acc_ref)

    xn = xn_ref[...]
    hg = _dot(xn, wg_ref[...])
    hu = _dot(xn, wu_ref[...])
    acc_ref[...] += _dot((_silu(hg) * hu).astype(BF16), wd_ref[...])

    @pl.when(f == n_f - 1)
    def _():
        o_ref[...] = x_ref[...] + acc_ref[...]


def ffn_dense(x, g, w_up, w_down):
    M, D = x.shape
    F = w_down.shape[0]
    tm = _tile(M, 512)
    tf = F // 2 if (F // 2) % LANES == 0 else F
    n_f = F // tf
    return pl.pallas_call(
        functools.partial(_ffn_dense_kernel, n_f=n_f),
        grid=(M // tm, n_f),
        in_specs=[pl.BlockSpec((tm, D), lambda i, f: (i, 0)),
                  pl.BlockSpec((1, D), lambda i, f: (0, 0)),
                  pl.BlockSpec((D, tf), lambda i, f: (0, f)),
                  pl.BlockSpec((D, tf), lambda i, f: (0, n_f + f)),
                  pl.BlockSpec((tf, D), lambda i, f: (f, 0))],
        out_specs=pl.BlockSpec((tm, D), lambda i, f: (i, 0)),
        out_shape=jax.ShapeDtypeStruct((M, D), F32),
        scratch_shapes=[pltpu.VMEM((tm, D), BF16), pltpu.VMEM((tm, D), F32)],
        compiler_params=_cp("parallel", "arbitrary"),
        name="ffn_dense",
    )(x, g.reshape(1, D), w_up, w_up, w_down)


def _ret_kernel(*refs, C, has_s0, has_prev):
    q_ref, k_ref, v_ref, g_ref, cos_ref, sin_ref, gng_ref, gnb_ref = refs[:8]
    n = 8
    s0_ref = None
    if has_s0:
        s0_ref = refs[n]
        n += 1
    if has_prev:
        n += 1
    y_ref, s_ref = refs[n], refs[n + 1]
    c = pl.program_id(1)

    @pl.when(c == 0)
    def _():
        if has_s0:
            s_ref[...] = s0_ref[...]
        else:
            s_ref[...] = jnp.zeros_like(s_ref)

    cos, sin = cos_ref[...], sin_ref[...]
    half = RET_DK // 2
    t = lax.broadcasted_iota(I32, (C, 1), 0).astype(F32)
    diff = (lax.broadcasted_iota(I32, (C, C), 0) - lax.broadcasted_iota(I32, (C, C), 1)).astype(F32)

    def rot(x):
        x1, x2 = x[:, :half], x[:, half:]
        return jnp.concatenate([x1 * cos - x2 * sin, x2 * cos + x1 * sin], axis=1)

    for h in range(RET_HEADS):
        lg = math.log(1.0 - 2.0 ** (-5.0 - h))
        qr = rot(q_ref[:, h * RET_DK:(h + 1) * RET_DK])
        kr = rot(k_ref[:, h * RET_DK:(h + 1) * RET_DK]) * RET_DK ** -0.5
        v = v_ref[:, h * RET_DV:(h + 1) * RET_DV].astype(BF16)
        decay = jnp.where(diff >= 0, jnp.exp(lg * jnp.maximum(diff, 0.0)), 0.0)
        sc = _dot_nt(qr.astype(BF16), kr.astype(BF16)) * decay
        S = s_ref[0, h]
        o = _dot(sc.astype(BF16), v) + _dot((qr * jnp.exp(lg * (t + 1.0))).astype(BF16), S.astype(BF16))
        kd = (kr * jnp.exp(lg * (C - 1.0 - t))).astype(BF16)
        s_ref[0, h] = math.exp(lg * C) * S + _dot_tn(kd, v)
        mu = jnp.mean(o, axis=-1, keepdims=True)
        var = jnp.mean(jnp.square(o - mu), axis=-1, keepdims=True)
        sl = slice(h * RET_DV, (h + 1) * RET_DV)
        on = (o - mu) * lax.rsqrt(var + EPS) * gng_ref[:, sl] + gnb_ref[:, sl]
        y_ref[:, sl] = _silu(g_ref[:, sl]) * on


def ret_core(proj, B, T, row0, cos, sin, tab0, gng, gnb, s0, y_prev):
    M = proj.shape[0]
    C = RET_CHUNK if T % RET_CHUNK == 0 else T
    nC = T // C
    r0, t0 = row0 // C, tab0 // C
    qkw, vw = RET_HEADS * RET_DK, RET_HEADS * RET_DV
    row = lambda b, c: r0 + b * nC + c
    in_specs = [pl.BlockSpec((C, qkw), lambda b, c: (row(b, c), 0)),
                pl.BlockSpec((C, qkw), lambda b, c: (row(b, c), 1)),
                pl.BlockSpec((C, vw), lambda b, c: (row(b, c), 2 * qkw // vw)),
                pl.BlockSpec((C, vw), lambda b, c: (row(b, c), 2 * qkw // vw + 1)),
                pl.BlockSpec((C, RET_DK // 2), lambda b, c: (t0 + c, 0)),
                pl.BlockSpec((C, RET_DK // 2), lambda b, c: (t0 + c, 0)),
                pl.BlockSpec((1, vw), lambda b, c: (0, 0)),
                pl.BlockSpec((1, vw), lambda b, c: (0, 0))]
    args = [proj, proj, proj, proj, cos, sin, gng.reshape(1, vw), gnb.reshape(1, vw)]
    sblock = (1, RET_HEADS, RET_DK, RET_DV)
    if s0 is not None:
        in_specs.append(pl.BlockSpec(sblock, lambda b, c: (b, 0, 0, 0)))
        args.append(s0)
    aliases = {}
    if y_prev is not None:
        in_specs.append(pl.BlockSpec(memory_space=pl.ANY))
        aliases = {len(args): 0}
        args.append(y_prev)
    return pl.pallas_call(
        functools.partial(_ret_kernel, C=C, has_s0=s0 is not None, has_prev=y_prev is not None),
        grid=(B, nC),
        in_specs=in_specs,
        out_specs=[pl.BlockSpec((C, vw), lambda b, c: (row(b, c), 0)),
                   pl.BlockSpec(sblock, lambda b, c: (b, 0, 0, 0))],
        out_shape=[jax.ShapeDtypeStruct((M, vw), F32),
                   jax.ShapeDtypeStruct((B,) + sblock[1:], F32)],
        input_output_aliases=aliases,
        compiler_params=_cp("parallel", "arbitrary"),
        name="ret_core",
    )(*args)


def _hgrn_levels(C):
    out, m = [], C // 2
    while m >= 1:
        out.append(m)
        m //= 2
    return out


def _hgrn_mats(C):
    t = np.arange(C)[:, None]
    u = np.arange(C)[None, :]
    mats = [(u <= t).astype(np.float32)]
    for m in _hgrn_levels(C):
        r = 2 * m * (t // (2 * m)) + m - 1
        a = ((u > r) & (u <= t)).astype(np.float32) - ((u > t) & (u <= r)).astype(np.float32)
        mats.append(a)
    return np.stack(mats)


def _hgrn_kernel(*refs, C, has_s0, has_prev):
    q_ref, f_ref, i_ref, g_ref, lb_ref, gng_ref, a_ref = refs[:7]
    n = 7
    s0_ref = None
    if has_s0:
        s0_ref = refs[n]
        n += 1
    if has_prev:
        n += 1
    y_ref, s_ref = refs[n], refs[n + 1]
    c = pl.program_id(1)

    @pl.when(c == 0)
    def _():
        if has_s0:
            s_ref[...] = s0_ref[...]
        else:
            s_ref[...] = jnp.zeros_like(s_ref)

    lb = lb_ref[...]
    f = lb + (1.0 - lb) * _sigmoid(f_ref[...])
    logf = jnp.log(f)
    kk = 1.0 - f
    q = _silu(q_ref[...])
    v = i_ref[...].astype(BF16)
    parts = _split3(logf)

    def amul(idx):
        a = a_ref[idx]
        return _dot(a, parts[0]) + _dot(a, parts[1]) + _dot(a, parts[2])

    bcum = amul(0)
    blast = bcum[C - 1:C, :]
    qb = (q * jnp.exp(bcum)).astype(BF16)
    kst = (kk * jnp.exp(blast - bcum)).astype(BF16)
    eb = jnp.exp(blast)

    ti = lax.broadcasted_iota(I32, (C, C), 0)
    si = lax.broadcasted_iota(I32, (C, C), 1)
    terms = [(q.astype(BF16), kk.astype(BF16), ti == si)]
    for li, m in enumerate(_hgrn_levels(C)):
        dq = amul(1 + li)
        qt = (q * jnp.exp(jnp.minimum(dq, 0.0))).astype(BF16)
        kt = (kk * jnp.exp(jnp.minimum(-dq, 0.0))).astype(BF16)
        blk = 2 * m
        mask = ((ti // blk) == (si // blk)) & ((ti % blk) >= m) & ((si % blk) < m)
        terms.append((qt, kt, mask))

    eye = (lax.broadcasted_iota(I32, (HG_DK, HG_DK), 0) == lax.broadcasted_iota(I32, (HG_DK, HG_DK), 1))
    for h in range(HG_HEADS):
        sl = slice(h * HG_DK, (h + 1) * HG_DK)
        sc = jnp.zeros((C, C), F32)
        for qt, kt, mask in terms:
            sc = sc + jnp.where(mask, _dot_nt(qt[:, sl], kt[:, sl]), 0.0)
        S = s_ref[0, h]
        vh = v[:, sl]
        o = _dot(sc.astype(BF16), vh) + _dot(qb[:, sl], S.astype(BF16))
        ecol = jnp.sum(jnp.where(eye, eb[:, sl], 0.0), axis=1, keepdims=True)
        s_ref[0, h] = ecol * S + _dot_tn(kst[:, sl], vh)
        on = o * lax.rsqrt(jnp.mean(o * o, axis=-1, keepdims=True) + EPS)
        y_ref[:, sl] = on * gng_ref[:, sl] * _sigmoid(g_ref[:, sl])


def hgrn_core(proj, B, T, row0, lb, gng, s0, y_prev):
    M = proj.shape[0]
    W = HG_HEADS * HG_DK
    C = HG_CHUNK if T % HG_CHUNK == 0 else T
    nC = T // C
    r0 = row0 // C
    row = lambda b, c: r0 + b * nC + c
    amats = jnp.asarray(_hgrn_mats(C), BF16)
    col = lambda j: pl.BlockSpec((C, W), lambda b, c: (row(b, c), j))
    in_specs = [col(j) for j in range(4)]
    in_specs += [pl.BlockSpec((1, W), lambda b, c: (0, 0)),
                 pl.BlockSpec((1, W), lambda b, c: (0, 0)),
                 pl.BlockSpec(amats.shape, lambda b, c: (0, 0, 0))]
    args = [proj, proj, proj, proj, lb.reshape(1, W), gng.reshape(1, W), amats]
    sblock = (1, HG_HEADS, HG_DK, HG_DK)
    if s0 is not None:
        in_specs.append(pl.BlockSpec(sblock, lambda b, c: (b, 0, 0, 0)))
        args.append(s0)
    aliases = {}
    if y_prev is not None:
        in_specs.append(pl.BlockSpec(memory_space=pl.ANY))
        aliases = {len(args): 0}
        args.append(y_prev)
    return pl.pallas_call(
        functools.partial(_hgrn_kernel, C=C, has_s0=s0 is not None, has_prev=y_prev is not None),
        grid=(B, nC),
        in_specs=in_specs,
        out_specs=[pl.BlockSpec((C, W), lambda b, c: (row(b, c), 0)),
                   pl.BlockSpec(sblock, lambda b, c: (b, 0, 0, 0))],
        out_shape=[jax.ShapeDtypeStruct((M, W), F32),
                   jax.ShapeDtypeStruct((B,) + sblock[1:], F32)],
        input_output_aliases=aliases,
        compiler_params=_cp("parallel", "arbitrary"),
        name="hgrn_core",
    )(*args)


def _prep_kernel(x_ref, cos_ref, sin_ref, gain_ref, *o_refs, flags, outs):
    cos, sin = cos_ref[...], sin_ref[...]
    for (lo, hi), o_ref in zip(outs, o_refs):
        for j in range(lo // LANES, hi // LANES):
            xj = x_ref[:, j * LANES:(j + 1) * LANES]
            if flags[j]:
                y = _rms(xj, gain_ref[j:j + 1, :])
                xj = y * cos + pltpu.roll(y, LANES // 2, 1) * sin
            o_ref[:, j * LANES - lo:(j + 1) * LANES - lo] = xj.astype(o_ref.dtype)


def prep_heads(x, cos, sin, gains, flags, outs):
    M, W = x.shape
    tm = _tile(M, 256)
    nb = W // LANES
    return pl.pallas_call(
        functools.partial(_prep_kernel, flags=tuple(flags), outs=tuple(outs)),
        grid=(M // tm,),
        in_specs=[pl.BlockSpec((tm, W), lambda i: (i, 0)),
                  pl.BlockSpec((tm, LANES), lambda i: (i, 0)),
                  pl.BlockSpec((tm, LANES), lambda i: (i, 0)),
                  pl.BlockSpec((nb, LANES), lambda i: (0, 0))],
        out_specs=[pl.BlockSpec((tm, hi - lo), lambda i: (i, 0)) for lo, hi in outs],
        out_shape=[jax.ShapeDtypeStruct((M, hi - lo), F32) for lo, hi in outs],
        compiler_params=_cp("parallel"),
        name="prep_heads",
    )(x, cos, sin, gains)


def _softmax_head(s, valid, v):
    s = jnp.where(valid, s, NEG)
    m = jnp.max(s, axis=-1, keepdims=True)
    p = jnp.exp(s - m)
    l = jnp.sum(p, axis=-1, keepdims=True)
    o = _dot(p.astype(BF16), v) / l
    return o, m + jnp.log(l)


def _dil_prompt_kernel(q_ref, kc_ref, kp_ref, vc_ref, vp_ref, o_ref, l_ref, *, tq, reach):
    qi = pl.program_id(2)
    scale = HEAD_DIM ** -0.5
    qpos = qi * tq + lax.broadcasted_iota(I32, (tq, 2 * tq), 0)
    kpos = (qi - 1) * tq + lax.broadcasted_iota(I32, (tq, 2 * tq), 1)
    delta = qpos - kpos
    valid = (delta >= 0) & (delta <= reach) & (kpos >= 0)
    lane = lax.broadcasted_iota(I32, (tq, LANES), 1)
    lse_all = jnp.zeros((tq, LANES), F32)
    for h in range(DIL_HEADS):
        sl = slice(h * HEAD_DIM, (h + 1) * HEAD_DIM)
        q = q_ref[:, sl].astype(BF16)
        k = jnp.concatenate([kp_ref[:, sl], kc_ref[:, sl]], axis=0).astype(BF16)
        v = jnp.concatenate([vp_ref[:, sl], vc_ref[:, sl]], axis=0).astype(BF16)
        o, lse = _softmax_head(_dot_nt(q, k) * scale, valid, v)
        o_ref[:, sl] = o
        lse_all = jnp.where(lane == h, lse, lse_all)
    l_ref[...] = lse_all


def dil_prompt(pp, gi, B, T):
    window, d = DIL_GROUPS[gi]
    M, W = pp.shape
    HW = DIL_HEADS * HEAD_DIM
    Tc = T // d
    tq = min(DIL_QBLOCK, Tc)
    nq = Tc // tq
    cpr = W // HW
    ppv = pp.reshape(M // d, d * W)

    def blk(off, prev):
        return pl.BlockSpec(
            (tq, HW),
            lambda b, r, qi: (b * nq + (jnp.maximum(qi - 1, 0) if prev else qi), r * cpr + gi * 3 + off))

    o, l = pl.pallas_call(
        functools.partial(_dil_prompt_kernel, tq=tq, reach=window // d),
        grid=(B, d, nq),
        in_specs=[blk(0, False), blk(1, False), blk(1, True), blk(2, False), blk(2, True)],
        out_specs=[pl.BlockSpec((tq, HW), lambda b, r, qi: (b * nq + qi, r)),
                   pl.BlockSpec((tq, LANES), lambda b, r, qi: (b * nq + qi, r))],
        out_shape=[jax.ShapeDtypeStruct((M // d, d * HW), F32),
                   jax.ShapeDtypeStruct((M // d, d * LANES), F32)],
        compiler_params=_cp("parallel", "parallel", "arbitrary"),
        name=f"dil_prompt_{gi}",
    )(ppv, ppv, ppv, ppv, ppv)
    return o.reshape(M, HW), l.reshape(M, LANES)


def _dil_decode_kernel(q_ref, kn_ref, vn_ref, buf_ref, op_ref, lp_ref, o_ref, l_ref, nb_ref, *, L, Ts, d, window):
    del op_ref, lp_ref
    H = DIL_HEADS
    rpp = 2 * H
    scale = HEAD_DIM ** -0.5
    t = lax.broadcasted_iota(I32, (Ts, L + Ts), 0)
    i = lax.broadcasted_iota(I32, (Ts, L + Ts), 1)
    delta = L + t - i
    valid = (delta >= 0) & (delta % d == 0) & (delta <= window)
    lane = lax.broadcasted_iota(I32, (Ts, LANES), 1)
    lse_all = jnp.zeros((Ts, LANES), F32)
    for h in range(H):
        sl = slice(h * HEAD_DIM, (h + 1) * HEAD_DIM)
        q = q_ref[:, sl].astype(BF16)
        k = jnp.concatenate([buf_ref[0, pl.ds(h, L, stride=rpp), :], kn_ref[:, sl]], axis=0).astype(BF16)
        v = jnp.concatenate([buf_ref[0, pl.ds(H + h, L, stride=rpp), :], vn_ref[:, sl]], axis=0).astype(BF16)
        o, lse = _softmax_head(_dot_nt(q, k) * scale, valid, v)
        o_ref[:, sl] = o
        lse_all = jnp.where(lane == h, lse, lse_all)
        nb_ref[0, pl.ds((L - Ts) * rpp + h, Ts, stride=rpp), :] = kn_ref[:, sl]
        nb_ref[0, pl.ds((L - Ts) * rpp + H + h, Ts, stride=rpp), :] = vn_ref[:, sl]
    l_ref[...] = lse_all
    nb_ref[0, 0:(L - Ts) * rpp, :] = buf_ref[0, Ts * rpp:L * rpp, :]


def dil_decode(pp, gi, B, Ts, row0, buf, o_prev, l_prev):
    window, d = DIL_GROUPS[gi]
    M, W = pp.shape
    HW = DIL_HEADS * HEAD_DIM
    rows = buf.shape[1]
    L = rows // (2 * DIL_HEADS)
    r0 = row0 // Ts
    col = lambda off: pl.BlockSpec((Ts, HW), lambda b: (r0 + b, gi * 3 + off))
    return pl.pallas_call(
        functools.partial(_dil_decode_kernel, L=L, Ts=Ts, d=d, window=window),
        grid=(B,),
        in_specs=[col(0), col(1), col(2),
                  pl.BlockSpec((1, rows, HEAD_DIM), lambda b: (b, 0, 0)),
                  pl.BlockSpec(memory_space=pl.ANY), pl.BlockSpec(memory_space=pl.ANY)],
        out_specs=[pl.BlockSpec((Ts, HW), lambda b: (r0 + b, 0)),
                   pl.BlockSpec((Ts, LANES), lambda b: (r0 + b, 0)),
                   pl.BlockSpec((1, rows, HEAD_DIM), lambda b: (b, 0, 0))],
        out_shape=[jax.ShapeDtypeStruct((M, HW), F32),
                   jax.ShapeDtypeStruct((M, LANES), F32),
                   jax.ShapeDtypeStruct((B, rows, HEAD_DIM), F32)],
        input_output_aliases={4: 0, 5: 1},
        compiler_params=_cp("parallel"),
        name=f"dil_decode_{gi}",
    )(pp, pp, pp, buf, o_prev, l_prev)


def _dil_merge_kernel(o0, o1, o2, l0, l1, l2, out_ref):
    ls = [l0[...], l1[...], l2[...]]
    m = jnp.maximum(jnp.maximum(ls[0], ls[1]), ls[2])
    es = [jnp.exp(l - m) for l in ls]
    den = es[0] + es[1] + es[2]
    ws = [e / den for e in es]
    os_ = [o0, o1, o2]
    for h in range(DIL_HEADS):
        sl = slice(h * HEAD_DIM, (h + 1) * HEAD_DIM)
        acc = ws[0][:, h:h + 1] * os_[0][:, sl]
        for g in (1, 2):
            acc = acc + ws[g][:, h:h + 1] * os_[g][:, sl]
        out_ref[:, sl] = acc.astype(BF16)


def dil_merge(os_, ls):
    M, HW = os_[0].shape
    tm = _tile(M, 512)
    return pl.pallas_call(
        _dil_merge_kernel,
        grid=(M // tm,),
        in_specs=[pl.BlockSpec((tm, HW), lambda i: (i, 0))] * 3 + [pl.BlockSpec((tm, LANES), lambda i: (i, 0))] * 3,
        out_specs=pl.BlockSpec((tm, HW), lambda i: (i, 0)),
        out_shape=jax.ShapeDtypeStruct((M, HW), BF16),
        compiler_params=_cp("parallel"),
        name="dil_merge",
    )(*os_, *ls)


def _nsa_compress_kernel(tab_ref, pg_ref, pe_ref, w1_ref, w2_ref, o_ref, hs_ref, *, n_pages, ncmp):
    del tab_ref
    p = pl.program_id(1)
    hb = PAGE // NSA_CMP_STRIDE
    ncol = 2 * NSA_KV_HEADS
    rowlen = 4 * NSA_KV_HEADS * HEAD_DIM
    for c in range(ncol):
        for j in range(NSA_CMP_STRIDE):
            hs_ref[c, pl.ds(pl.multiple_of(p * hb, hb), hb), j * HEAD_DIM:(j + 1) * HEAD_DIM] = (
                pg_ref[0, :, j * rowlen + c * HEAD_DIM:j * rowlen + (c + 1) * HEAD_DIM])

    @pl.when(p == n_pages - 1)
    def _():
        half = NSA_CMP_STRIDE * HEAD_DIM
        R = n_pages * hb
        rowi = lax.broadcasted_iota(I32, (R, HEAD_DIM), 0)
        for c in range(ncol):
            s = c // NSA_KV_HEADS
            H = hs_ref[c]
            a = _dot((H + pe_ref[s, 0:1, :]).astype(BF16), w1_ref[s, 0:half, :])
            bm = _dot((H + pe_ref[s, 1:2, :]).astype(BF16), w1_ref[s, half:2 * half, :])
            pre = a + pltpu.roll(bm, R - 1, 0)
            out = _dot(_silu(pre).astype(BF16), w2_ref[s])
            o_ref[0, c] = jnp.where(rowi < ncmp, out, 0.0).astype(BF16)


def nsa_compress(pages, table, B, n_pages, pe, w1, w2):
    hb = PAGE // NSA_CMP_STRIDE
    R = n_pages * hb
    ncmp = (n_pages * PAGE - NSA_CMP_LEN) // NSA_CMP_STRIDE + 1
    half = NSA_CMP_STRIDE * HEAD_DIM
    pv = pages.reshape(pages.shape[0], hb, NSA_CMP_STRIDE * pages.shape[2])
    grid_spec = pltpu.PrefetchScalarGridSpec(
        num_scalar_prefetch=1,
        grid=(B, n_pages),
        in_specs=[pl.BlockSpec((1,) + pv.shape[1:], lambda b, p, tab: (tab[b * n_pages + p], 0, 0)),
                  pl.BlockSpec((2, 2, half), lambda b, p, tab: (0, 0, 0)),
                  pl.BlockSpec(w1.shape, lambda b, p, tab: (0, 0, 0)),
                  pl.BlockSpec(w2.shape, lambda b, p, tab: (0, 0, 0))],
        out_specs=pl.BlockSpec((1, 4, R, HEAD_DIM), lambda b, p, tab: (b, 0, 0, 0)),
        scratch_shapes=[pltpu.VMEM((4, R, half), F32)])
    return pl.pallas_call(
        functools.partial(_nsa_compress_kernel, n_pages=n_pages, ncmp=ncmp),
        grid_spec=grid_spec,
        out_shape=jax.ShapeDtypeStruct((B, 4, R, HEAD_DIM), BF16),
        compiler_params=_cp("parallel", "arbitrary"),
        name="nsa_compress",
    )(table, pv, pe.reshape(2, 2, half), w1, w2)


def _flash_update(m_ref, l_ref, a_ref, k, s, valid, v):
    s = jnp.where(valid, s, NEG)
    m_old = m_ref[k]
    m_new = jnp.maximum(m_old, jnp.max(s, axis=-1, keepdims=True))
    p = jnp.where(valid, jnp.exp(s - m_new), 0.0)
    alpha = jnp.exp(m_old - m_new)
    l_ref[k] = alpha * l_ref[k] + jnp.sum(p, axis=-1, keepdims=True)
    a_ref[k] = alpha * a_ref[k] + _dot(p.astype(BF16), v)
    m_ref[k] = m_new


def _nsa_qstack(q_ref, k):
    G, D = NSA_GROUP, HEAD_DIM
    return jnp.concatenate([q_ref[:, (k * G + g) * D:(k * G + g + 1) * D] for g in range(G)],
                           axis=0).astype(BF16)


def _nsa_cmp_select(qs, kc, vc, c2s, pos_r, pos_q, n_slc, ncmp):
    R, tq = qs.shape[0], pos_q.shape[0]
    nidx = lax.broadcasted_iota(I32, (R, kc.shape[0]), 1)
    s = _dot_nt(qs, kc) * HEAD_DIM ** -0.5
    valid = (NSA_CMP_STRIDE * nidx + NSA_CMP_LEN - 1 <= pos_r) & (nidx < ncmp)
    s = jnp.where(valid, s, NEG)
    pr = jnp.where(valid, jnp.exp(s - jnp.max(s, axis=-1, keepdims=True)), 0.0)
    pc = pr / jnp.maximum(jnp.sum(pr, axis=-1, keepdims=True), 1e-30)
    o_cmp = _dot(pc.astype(BF16), vc)
    psum = pc[0:tq]
    for g in range(1, NSA_GROUP):
        psum = psum + pc[g * tq:(g + 1) * tq]
    ph, pm, plo = _split3(psum)
    imp = _dot(ph, c2s) + _dot(pm, c2s) + _dot(plo, c2s)
    j = lax.broadcasted_iota(I32, (tq, LANES), 1)
    cur = pos_q // NSA_SLC_LEN
    forced = (j == 0) | (j == cur) | (j == cur - 1)
    score = jnp.where(forced, 1e4, jnp.where(j <= cur, imp, -1.0))
    score = jnp.where(j < n_slc, score, -2.0)
    cnt = jnp.zeros((tq, LANES), F32)
    for jp in range(n_slc):
        col = score[:, jp:jp + 1]
        cnt = cnt + ((col > score) | ((col == score) & (jp < j))).astype(F32)
    return o_cmp, (cnt < float(min(NSA_TOP_N, n_slc))).astype(F32)


def _nsa_attn_kernel(tab_ref, q_ref, gate_ref, kc_ref, c2s_ref, pg_ref, wpg_ref, o_ref,
                     ocmp, sel, m_s, l_s, a_s, m_w, l_w, a_w, *, tq, n_pages, n_wp, n_slc, ncmp):
    del tab_ref
    qi, p = pl.program_id(1), pl.program_id(2)
    G, KVH, D = NSA_GROUP, NSA_KV_HEADS, HEAD_DIM
    R = G * tq
    scale = D ** -0.5
    pos_r = qi * tq + lax.broadcasted_iota(I32, (R, 1), 0) % tq
    pos_q = qi * tq + lax.broadcasted_iota(I32, (tq, 1), 0)
    plast = (qi * tq + tq - 1) // PAGE
    qstack = functools.partial(_nsa_qstack, q_ref)

    def tile_rows(x):
        return jnp.concatenate([x] * G, axis=0)

    @pl.when(p == 0)
    def _():
        for k in range(KVH):
            ocmp[k], sel[k] = _nsa_cmp_select(qstack(k), kc_ref[0, k], kc_ref[0, KVH + k], c2s_ref[...],
                                              pos_r, pos_q, n_slc, ncmp)
        for ref in (m_s, m_w):
            ref[...] = jnp.full(ref.shape, NEG, F32)
        for ref in (l_s, a_s, l_w, a_w):
            ref[...] = jnp.zeros(ref.shape, F32)

    kpos = p * PAGE + lax.broadcasted_iota(I32, (R, PAGE), 1)
    dist = pos_r - kpos

    @pl.when(p <= plast)
    def _():
        j = lax.broadcasted_iota(I32, (tq, LANES), 1)
        lane = lax.broadcasted_iota(I32, (tq, PAGE), 1)
        per = PAGE // NSA_SLC_LEN
        for k in range(KVH):
            ks = pg_ref[0, :, (2 * KVH + k) * D:(2 * KVH + k + 1) * D].astype(BF16)
            vs = pg_ref[0, :, (3 * KVH + k) * D:(3 * KVH + k + 1) * D].astype(BF16)
            selk = sel[k]
            selrow = jnp.zeros((tq, PAGE), F32)
            for u in range(per):
                colu = jnp.sum(jnp.where(j == per * p + u, selk, 0.0), axis=-1, keepdims=True)
                selrow = jnp.where(lane // NSA_SLC_LEN == u, colu, selrow)
            valid = (tile_rows(selrow) > 0.5) & (dist >= 0)
            _flash_update(m_s, l_s, a_s, k, _dot_nt(qstack(k), ks) * scale, valid, vs)

    @pl.when((p <= plast) & (p >= plast - n_wp))
    def _():
        valid = (dist >= 0) & (dist <= NSA_WINDOW)
        for k in range(KVH):
            kw = wpg_ref[0, :, k * D:(k + 1) * D].astype(BF16)
            vw = wpg_ref[0, :, (KVH + k) * D:(KVH + k + 1) * D].astype(BF16)
            _flash_update(m_w, l_w, a_w, k, _dot_nt(qstack(k), kw) * scale, valid, vw)

    @pl.when(p == n_pages - 1)
    def _():
        gs = _sigmoid(gate_ref[...])
        for k in range(KVH):
            _nsa_gated_out(o_ref, gs, k, tq, ocmp[k], a_s[k] / jnp.maximum(l_s[k], 1e-30),
                           a_w[k] / jnp.maximum(l_w[k], 1e-30))


def _nsa_gated_out(o_ref, gs, k, tq, o_c, o_s, o_w):
    for g in range(NSA_GROUP):
        hd = k * NSA_GROUP + g
        rows = slice(g * tq, (g + 1) * tq)
        o = (gs[:, 3 * hd:3 * hd + 1] * o_c[rows] + gs[:, 3 * hd + 1:3 * hd + 2] * o_s[rows]
             + gs[:, 3 * hd + 2:3 * hd + 3] * o_w[rows])
        o_ref[:, hd * HEAD_DIM:(hd + 1) * HEAD_DIM] = o


def _softmax_two(s1, ok1, v1, s2, ok2, v2):
    s1 = jnp.where(ok1, s1, NEG)
    s2 = jnp.where(ok2, s2, NEG)
    m = jnp.maximum(jnp.max(s1, axis=-1, keepdims=True), jnp.max(s2, axis=-1, keepdims=True))
    p1 = jnp.where(ok1, jnp.exp(s1 - m), 0.0)
    p2 = jnp.where(ok2, jnp.exp(s2 - m), 0.0)
    l = jnp.sum(p1, axis=-1, keepdims=True) + jnp.sum(p2, axis=-1, keepdims=True)
    return (_dot(p1.astype(BF16), v1) + _dot(p2.astype(BF16), v2)) / jnp.maximum(l, 1e-30)


def _nsa_sample_kernel(*refs, Ts, n_pages, Lw, n_slc, ncmp, past):
    (tab_ref, q_ref, gate_ref, xn_ref, xw_ref, wb_ref, pe_ref, w1_ref, w2_ref, c2s_ref, exp_ref) = refs[:11]
    pg_refs = refs[11:11 + n_pages]
    o_ref, nwb_ref, hs_ref = refs[12 + n_pages:]
    del tab_ref
    G, KVH, D = NSA_GROUP, NSA_KV_HEADS, HEAD_DIM
    R = G * Ts
    scale = D ** -0.5
    rpt = 4 * KVH
    hb = PAGE // NSA_CMP_STRIDE
    half = NSA_CMP_STRIDE * D
    pos_r = past + lax.broadcasted_iota(I32, (R, 1), 0) % Ts
    pos_q = past + lax.broadcasted_iota(I32, (Ts, 1), 0)

    def tile_rows(x):
        return jnp.concatenate([x] * G, axis=0)

    for u in range(n_pages):
        for c in range(2 * KVH):
            for j in range(NSA_CMP_STRIDE):
                hs_ref[c, u * hb:(u + 1) * hb, j * D:(j + 1) * D] = (
                    pg_refs[u][0, pl.ds(j * rpt + c, hb, stride=NSA_CMP_STRIDE * rpt), :])
    nrow = n_pages * hb
    rowi = lax.broadcasted_iota(I32, (nrow, D), 0)
    cmp = []
    for c in range(2 * KVH):
        s = c // KVH
        H = hs_ref[c]
        a = _dot((H + pe_ref[s, 0:1, :]).astype(BF16), w1_ref[s, 0:half, :])
        bm = _dot((H + pe_ref[s, 1:2, :]).astype(BF16), w1_ref[s, half:2 * half, :])
        out = _dot(_silu(a + pltpu.roll(bm, nrow - 1, 0)).astype(BF16), w2_ref[s])
        cmp.append(jnp.where(rowi < ncmp, out, 0.0).astype(BF16))

    gs = _sigmoid(gate_ref[...])
    jn = lax.broadcasted_iota(I32, (R, Ts), 1)
    causal = past + jn <= pos_r
    wpos = past - Lw + lax.broadcasted_iota(I32, (R, Lw), 1)
    wdist = pos_r - wpos
    wvalid = (wdist >= 0) & (wdist <= NSA_WINDOW)
    blk = past // NSA_SLC_LEN
    for k in range(KVH):
        qs = _nsa_qstack(q_ref, k)
        o_c, selk = _nsa_cmp_select(qs, cmp[k], cmp[KVH + k], c2s_ref[...], pos_r, pos_q, n_slc, ncmp)
        ks = jnp.concatenate([pg[0, pl.ds(2 * KVH + k, PAGE, stride=rpt), :] for pg in pg_refs],
                             axis=0).astype(BF16)
        vs = jnp.concatenate([pg[0, pl.ds(3 * KVH + k, PAGE, stride=rpt), :] for pg in pg_refs],
                             axis=0).astype(BF16)
        ok_past = tile_rows(_dot(selk.astype(BF16), exp_ref[...])) > 0.5
        ok_new = (tile_rows(selk[:, blk:blk + 1]) > 0.5) & causal
        kn = xn_ref[:, (2 * KVH + k) * D:(2 * KVH + k + 1) * D].astype(BF16)
        vn = xn_ref[:, (3 * KVH + k) * D:(3 * KVH + k + 1) * D].astype(BF16)
        o_s = _softmax_two(_dot_nt(qs, ks) * scale, ok_past, vs, _dot_nt(qs, kn) * scale, ok_new, vn)
        kw = wb_ref[0, pl.ds(k, Lw, stride=2 * KVH), :].astype(BF16)
        vw = wb_ref[0, pl.ds(KVH + k, Lw, stride=2 * KVH), :].astype(BF16)
        kwn = xw_ref[:, k * D:(k + 1) * D]
        vwn = xw_ref[:, (KVH + k) * D:(KVH + k + 1) * D]
        o_w = _softmax_two(_dot_nt(qs, kw) * scale, wvalid, vw,
                           _dot_nt(qs, kwn.astype(BF16)) * scale, causal, vwn.astype(BF16))
        _nsa_gated_out(o_ref, gs, k, Ts, o_c, o_s, o_w)
        nwb_ref[0, pl.ds((Lw - Ts) * 2 * KVH + k, Ts, stride=2 * KVH), :] = kwn
        nwb_ref[0, pl.ds((Lw - Ts) * 2 * KVH + KVH + k, Ts, stride=2 * KVH), :] = vwn
    nwb_ref[0, 0:(Lw - Ts) * 2 * KVH, :] = wb_ref[0, Ts * 2 * KVH:Lw * 2 * KVH, :]


def nsa_sample(q, gate, xnew, xwin, pool, table, wbuf, pe, w1, w2, B, Ts, row0, past, o_prev):
    M, QW = q.shape
    D = HEAD_DIM
    n_pages = past // PAGE
    assert past % PAGE == 0 and Ts < NSA_CMP_STRIDE
    Lw = wbuf.shape[1] // (2 * NSA_KV_HEADS)
    L = past + Ts
    n_slc = -(-L // NSA_SLC_LEN)
    ncmp = (L - NSA_CMP_LEN) // NSA_CMP_STRIDE + 1
    hb = PAGE // NSA_CMP_STRIDE
    half = NSA_CMP_STRIDE * D
    c2s = _cmp_to_slc(n_pages * hb, ncmp, n_slc)
    expand = jnp.asarray(np.arange(LANES)[:, None] == np.arange(past)[None, :] // NSA_SLC_LEN, BF16)
    r0 = row0 // Ts
    rows = lambda w: pl.BlockSpec((Ts, w), lambda b, tab: (r0 + b, 0))
    full = lambda a: pl.BlockSpec(a.shape, lambda b, tab: (0,) * a.ndim)
    page = lambda u: pl.BlockSpec((1,) + pool.shape[1:], lambda b, tab: (tab[b * n_pages + u], 0, 0))
    pe2 = pe.reshape(2, 2, half)
    in_specs = ([rows(QW), rows(LANES), rows(xnew.shape[1]), rows(xwin.shape[1]),
                 pl.BlockSpec((1,) + wbuf.shape[1:], lambda b, tab: (b, 0, 0)),
                 full(pe2), full(w1), full(w2), full(c2s), full(expand)]
                + [page(u) for u in range(n_pages)] + [pl.BlockSpec(memory_space=pl.ANY)])
    grid_spec = pltpu.PrefetchScalarGridSpec(
        num_scalar_prefetch=1,
        grid=(B,),
        in_specs=in_specs,
        out_specs=[rows(QW), pl.BlockSpec((1,) + wbuf.shape[1:], lambda b, tab: (b, 0, 0))],
        scratch_shapes=[pltpu.VMEM((2 * NSA_KV_HEADS, n_pages * hb, half), F32)])
    return pl.pallas_call(
        functools.partial(_nsa_sample_kernel, Ts=Ts, n_pages=n_pages, Lw=Lw, n_slc=n_slc, ncmp=ncmp, past=past),
        grid_spec=grid_spec,
        out_shape=[jax.ShapeDtypeStruct((M, QW), F32), jax.ShapeDtypeStruct(wbuf.shape, F32)],
        input_output_aliases={11 + n_pages: 0},
        compiler_params=_cp("parallel"),
        name="nsa_sample",
    )(table, q, gate, xnew, xwin, wbuf, pe2, w1, w2, c2s, expand, *([pool] * n_pages), o_prev)


def _cmp_to_slc(nrows, ncmp, n_slc):
    i = np.arange(nrows)[:, None] * NSA_CMP_STRIDE
    j = np.arange(LANES)[None, :] * NSA_SLC_LEN
    ov = np.clip(np.minimum(i + NSA_CMP_LEN, j + NSA_SLC_LEN) - np.maximum(i, j), 0, None) / NSA_CMP_LEN
    ov = np.where((np.arange(nrows)[:, None] < ncmp) & (np.arange(LANES)[None, :] < n_slc), ov, 0.0)
    return jnp.asarray(ov, BF16)


def nsa_attn(q, gate, kc, pages, table, wpages, B, T):
    M = q.shape[0]
    QW = NSA_HEADS * HEAD_DIM
    tq = min(PAGE, T)
    nqb = T // tq
    n_pages = T // PAGE
    n_slc = -(-T // NSA_SLC_LEN)
    ncmp = (T - NSA_CMP_LEN) // NSA_CMP_STRIDE + 1
    n_wp = NSA_WINDOW // PAGE
    R = NSA_GROUP * tq
    c2s = _cmp_to_slc(kc.shape[2], ncmp, n_slc)
    qrow = lambda b, qi: b * nqb + qi
    plast = lambda qi: (qi * tq + tq - 1) // PAGE
    pidx = lambda b, qi, p, tab: (tab[b * n_pages + jnp.minimum(p, plast(qi))], 0, 0)
    widx = lambda b, qi, p, tab: (b * n_pages + jnp.clip(p, jnp.maximum(plast(qi) - n_wp, 0), plast(qi)), 0, 0)
    in_specs = [pl.BlockSpec((tq, QW), lambda b, qi, p, tab: (qrow(b, qi), 0)),
                pl.BlockSpec((tq, LANES), lambda b, qi, p, tab: (qrow(b, qi), 0)),
                pl.BlockSpec((1,) + kc.shape[1:], lambda b, qi, p, tab: (b, 0, 0, 0)),
                pl.BlockSpec(c2s.shape, lambda b, qi, p, tab: (0, 0)),
                pl.BlockSpec((1,) + pages.shape[1:], pidx),
                pl.BlockSpec((1,) + wpages.shape[1:], widx)]
    vec = lambda: pltpu.VMEM((NSA_KV_HEADS, R, 1), F32)
    mat = lambda: pltpu.VMEM((NSA_KV_HEADS, R, HEAD_DIM), F32)
    grid_spec = pltpu.PrefetchScalarGridSpec(
        num_scalar_prefetch=1,
        grid=(B, nqb, n_pages),
        in_specs=in_specs,
        out_specs=pl.BlockSpec((tq, QW), lambda b, qi, p, tab: (qrow(b, qi), 0)),
        scratch_shapes=[mat(), pltpu.VMEM((NSA_KV_HEADS, tq, LANES), F32),
                        vec(), vec(), mat(), vec(), vec(), mat()])
    return pl.pallas_call(
        functools.partial(_nsa_attn_kernel, tq=tq, n_pages=n_pages, n_wp=n_wp, n_slc=n_slc, ncmp=ncmp),
        grid_spec=grid_spec,
        out_shape=jax.ShapeDtypeStruct((M, QW), F32),
        compiler_params=_cp("parallel", "parallel", "arbitrary"),
        name="nsa_attn_prompt",
    )(table, q, gate, kc, c2s, pages, wpages)


def _router_kernel(x_ref, g_ref, wr_ref, h_ref, e_ref, p_ref):
    xn = _rms(x_ref[...], g_ref[...])
    h_ref[...] = xn
    xh, xm, xl = _split3(xn)
    wh, wm, wl = _split3(wr_ref[...])
    logits = (_dot(xh, wh) + _dot(xh, wm) + _dot(xm, wh)
              + _dot(xh, wl) + _dot(xl, wh) + _dot(xm, wm))
    lane = lax.broadcasted_iota(I32, logits.shape, 1)
    lg = jnp.where(lane < N_EXPERTS, logits, -jnp.inf)
    m1 = jnp.max(lg, axis=-1, keepdims=True)
    i1 = jnp.min(jnp.where(lg == m1, lane, LANES), axis=-1, keepdims=True)
    lg2 = jnp.where(lane == i1, -jnp.inf, lg)
    m2 = jnp.max(lg2, axis=-1, keepdims=True)
    i2 = jnp.min(jnp.where(lg2 == m2, lane, LANES), axis=-1, keepdims=True)
    e = jnp.exp(m2 - m1)
    e_ref[...] = jnp.where(lane == 0, i1, jnp.where(lane == 1, i2, 0))
    p_ref[...] = jnp.where(lane == 0, 1.0 / (1.0 + e), jnp.where(lane == 1, e / (1.0 + e), 0.0))


def moe_router(x, g, router):
    M, D = x.shape
    tm = _tile(M, 512)
    wr = jnp.zeros((D, LANES), F32).at[:, :N_EXPERTS].set(router)
    return pl.pallas_call(
        _router_kernel,
        grid=(M // tm,),
        in_specs=[pl.BlockSpec((tm, D), lambda i: (i, 0)),
                  pl.BlockSpec((1, D), lambda i: (0, 0)),
                  pl.BlockSpec((D, LANES), lambda i: (0, 0))],
        out_specs=[pl.BlockSpec((tm, D), lambda i: (i, 0)),
                   pl.BlockSpec((tm, LANES), lambda i: (i, 0)),
                   pl.BlockSpec((tm, LANES), lambda i: (i, 0))],
        out_shape=[jax.ShapeDtypeStruct((M, D), F32),
                   jax.ShapeDtypeStruct((M, LANES), I32),
                   jax.ShapeDtypeStruct((M, LANES), F32)],
        compiler_params=_cp("parallel"),
        name="moe_router",
    )(x, g.reshape(1, D), wr)


def _row_copy(src_ref, dst_ref, sem, src_row, dst_row):
    return pltpu.make_async_copy(src_ref.at[pl.ds(src_row, 1)], dst_ref.at[pl.ds(dst_row, 1)], sem)


def _moe_ffn_kernel(be_ref, rt_ref, nu_ref, h_ref, wg_ref, wu_ref, wd_ref, y_ref,
                    xbuf, xb16, acc, sem, *, tm, n_f):
    del be_ref
    r, f = pl.program_id(0), pl.program_id(1)
    used = r < nu_ref[0]

    @pl.when(used & (f == 0))
    def _():
        def issue(i, carry):
            _row_copy(h_ref, xbuf, sem, rt_ref[r * tm + i], i).start()
            return carry

        lax.fori_loop(0, tm, issue, 0)

        def drain(i, carry):
            _row_copy(h_ref, xbuf, sem, 0, i).wait()
            return carry

        lax.fori_loop(0, tm, drain, 0)
        xb16[...] = xbuf[...].astype(BF16)
        acc[...] = jnp.zeros_like(acc)

    @pl.when(used)
    def _():
        x = xb16[...]
        hg = _dot(x, wg_ref[0])
        hu = _dot(x, wu_ref[0])
        acc[...] += _dot((_silu(hg) * hu).astype(BF16), wd_ref[0])

    @pl.when(used & (f == n_f - 1))
    def _():
        y_ref[...] = acc[...]

    @pl.when(jnp.logical_not(used) & (f == n_f - 1))
    def _():
        y_ref[...] = jnp.zeros_like(y_ref)


def moe_ffn(h, block_expert, row_tok, n_used, w_up, w_down, layer):
    D = h.shape[1]
    F = w_down.shape[2]
    tm = MOE_ROWS
    nb = block_expert.shape[0]
    tf = 512
    n_f = F // tf
    live = lambda r, f, nu: jnp.where(r < nu[0], f, 0)
    grid_spec = pltpu.PrefetchScalarGridSpec(
        num_scalar_prefetch=3,
        grid=(nb, n_f),
        in_specs=[pl.BlockSpec(memory_space=pl.ANY),
                  pl.BlockSpec((None, 1, D, tf), lambda r, f, be, rt, nu: (layer, be[r], 0, live(r, f, nu))),
                  pl.BlockSpec((None, 1, D, tf), lambda r, f, be, rt, nu: (layer, be[r], 0, n_f + live(r, f, nu))),
                  pl.BlockSpec((None, 1, tf, D), lambda r, f, be, rt, nu: (layer, be[r], live(r, f, nu), 0))],
        out_specs=pl.BlockSpec((tm, D), lambda r, f, be, rt, nu: (r, 0)),
        scratch_shapes=[pltpu.VMEM((tm, D), F32), pltpu.VMEM((tm, D), BF16), pltpu.VMEM((tm, D), F32),
                        pltpu.SemaphoreType.DMA(())])
    return pl.pallas_call(
        functools.partial(_moe_ffn_kernel, tm=tm, n_f=n_f),
        grid_spec=grid_spec,
        out_shape=jax.ShapeDtypeStruct((nb * tm, D), F32),
        compiler_params=_cp("arbitrary", "arbitrary"),
        name="moe_ffn",
    )(block_expert, row_tok, n_used, h, w_up, w_up, w_down)


def _moe_combine_kernel(p0_ref, p1_ref, x_ref, g_ref, y_ref, o_ref, ybuf, sem, *, tm):
    i = pl.program_id(0)

    def issue(t, carry):
        _row_copy(y_ref, ybuf.at[0], sem, p0_ref[i * tm + t], t).start()
        _row_copy(y_ref, ybuf.at[1], sem, p1_ref[i * tm + t], t).start()
        return carry

    lax.fori_loop(0, tm, issue, 0)

    def drain(t, carry):
        _row_copy(y_ref, ybuf.at[0], sem, 0, t).wait()
        _row_copy(y_ref, ybuf.at[1], sem, 0, t).wait()
        return carry

    lax.fori_loop(0, tm, drain, 0)
    g = g_ref[...]
    o_ref[...] = x_ref[...] + (g[:, 0:1] * ybuf[0] + g[:, 1:2] * ybuf[1])


def moe_combine(x, gates, y, pos0, pos1):
    M, D = x.shape
    tm = _tile(M, 256)
    grid_spec = pltpu.PrefetchScalarGridSpec(
        num_scalar_prefetch=2,
        grid=(M // tm,),
        in_specs=[pl.BlockSpec((tm, D), lambda i, a, b: (i, 0)),
                  pl.BlockSpec((tm, LANES), lambda i, a, b: (i, 0)),
                  pl.BlockSpec(memory_space=pl.ANY)],
        out_specs=pl.BlockSpec((tm, D), lambda i, a, b: (i, 0)),
        scratch_shapes=[pltpu.VMEM((2, tm, D), F32), pltpu.SemaphoreType.DMA(())])
    return pl.pallas_call(
        functools.partial(_moe_combine_kernel, tm=tm),
        grid_spec=grid_spec,
        out_shape=jax.ShapeDtypeStruct((M, D), F32),
        compiler_params=_cp("arbitrary"),
        name="moe_combine",
    )(pos0, pos1, x, gates, y)


def moe_layer(x, g, router, w_up, w_down, layer):
    M = x.shape[0]
    h, eidx, gates = moe_router(x, g, router)
    A = 2 * M
    e = eidx[:, :2].reshape(A)
    onehot = (e[:, None] == jnp.arange(N_EXPERTS, dtype=I32)[None, :]).astype(I32)
    csum = jnp.cumsum(onehot, axis=0)
    rank = jnp.sum(csum * onehot, axis=1) - 1
    counts = csum[-1]
    padded = (counts + MOE_ROWS - 1) // MOE_ROWS * MOE_ROWS
    pend = jnp.cumsum(padded)
    dest = ((pend - padded)[e] + rank).astype(I32)
    nb = (A + N_EXPERTS * (MOE_ROWS - 1) + MOE_ROWS - 1) // MOE_ROWS
    row_tok = jnp.zeros((nb * MOE_ROWS,), I32).at[dest].set(jnp.arange(A, dtype=I32) // 2)
    block_expert = jnp.minimum(
        jnp.searchsorted(pend, jnp.arange(nb, dtype=I32) * MOE_ROWS, side='right'), N_EXPERTS - 1).astype(I32)
    n_used = (pend[-1:] // MOE_ROWS).astype(I32)
    y = moe_ffn(h, block_expert, row_tok, n_used, w_up, w_down, layer)
    dest2 = dest.reshape(M, 2)
    return moe_combine(x, gates, y, dest2[:, 0], dest2[:, 1])


def _rope_tables(pos, half):
    inv = ROPE_THETA ** (-jnp.arange(half, dtype=F32) / half)
    ang = pos.astype(F32)[:, None] * inv[None, :]
    return jnp.cos(ang), jnp.sin(ang)


def _row_rope_tables(Bp, T, Bs, Ts, past):
    pos = jnp.concatenate([jnp.tile(jnp.arange(T, dtype=I32), Bp),
                           jnp.tile(past + jnp.arange(Ts, dtype=I32), Bs)])
    cos, sin = _rope_tables(pos, HEAD_DIM // 2)
    return jnp.concatenate([cos, cos], axis=1), jnp.concatenate([-sin, sin], axis=1)


def kernel(x_prompt, x_sample, state_ret, state_hgrn, cache_dil_w128, cache_dil_w512, cache_dil_w2048,
           cache_nsa_kv, cache_nsa_win, page_table, norm_mix, norm_ffn,
           ret_w_in, ret_gn_g, ret_gn_b, ret_w_out,
           hg_w_in, hg_lower_bounds, hg_gn_g, hg_w_out,
           dil_w_in, dil_qn_g, dil_kn_g, dil_w_out,
           nsa_w_in, nsa_w_gate, nsa_qn_g, nsa_kn_g, nsa_cmp_pe, nsa_cmp_w1, nsa_cmp_w2, nsa_w_out,
           ffn_w_up, ffn_w_down, moe_router_w, moe_w_up, moe_w_down):
    Bp, T, D = x_prompt.shape
    Bs, Ts, _ = x_sample.shape
    Np, Ns = Bp * T, Bs * Ts
    M = Np + Ns
    n_pages = page_table.shape[1]
    past = n_pages * PAGE
    bf = lambda w: w.astype(BF16)

    x = jnp.concatenate([x_prompt.reshape(Np, D), x_sample.reshape(Ns, D)], axis=0)
    lb_all = jnp.cumsum(jax.nn.softmax(hg_lower_bounds.astype(F32), axis=0), axis=0)
    lb_all = lb_all - lb_all[0:1]

    proj = norm_proj(x, norm_mix[0], bf(ret_w_in[0]), 1024)
    cos_r, sin_r = _rope_tables(jnp.arange(max(T, past + Ts), dtype=I32), RET_DK // 2)
    y, ret_p = ret_core(proj, Bp, T, 0, cos_r, sin_r, 0, ret_gn_g[0], ret_gn_b[0], None, None)
    y, ret_s = ret_core(proj, Bs, Ts, Np, cos_r, sin_r, past, ret_gn_g[0], ret_gn_b[0], state_ret[0], y)
    x = out_proj(y, bf(ret_w_out[0]), x)
    x = ffn_dense(x, norm_ffn[0], bf(ffn_w_up[0]), bf(ffn_w_down[0]))

    proj = norm_proj(x, norm_mix[1], bf(hg_w_in[0]), 1024)
    y, hg_p = hgrn_core(proj, Bp, T, 0, lb_all[1], hg_gn_g[0], None, None)
    y, hg_s = hgrn_core(proj, Bs, Ts, Np, lb_all[1], hg_gn_g[0], state_hgrn[0], y)
    x = out_proj(y, bf(hg_w_out[0]), x)
    moe_up, moe_down = bf(moe_w_up), bf(moe_w_down)
    x = moe_layer(x, norm_ffn[1], moe_router_w[0], moe_up, moe_down, 0)

    cos_h, sin_h = _row_rope_tables(Bp, T, Bs, Ts, past)
    HW = DIL_HEADS * HEAD_DIM
    dil_in_w = dil_w_in.shape[2]
    proj = norm_proj(x, norm_mix[2], bf(dil_w_in[0]), 3 * HW)
    ones = jnp.ones((DIL_HEADS, HEAD_DIM), F32)
    gains = jnp.concatenate([jnp.concatenate([ones * dil_qn_g[0, gi], ones * dil_kn_g[0, gi], ones], axis=0)
                             for gi in range(len(DIL_GROUPS))], axis=0)
    flags = ([True] * (2 * DIL_HEADS) + [False] * DIL_HEADS) * len(DIL_GROUPS)
    (pp,) = prep_heads(proj, cos_h, sin_h, gains, flags, [(0, dil_in_w)])
    caches = (cache_dil_w128, cache_dil_w512, cache_dil_w2048)
    os_, ls, dil_p, dil_s = [], [], [], []
    for gi, (window, _) in enumerate(DIL_GROUPS):
        o_g, l_g = dil_prompt(pp, gi, Bp, T)
        buf = caches[gi][0]
        o_g, l_g, nbuf = dil_decode(pp, gi, Bs, Ts, Np, buf.reshape(Bs, -1, HEAD_DIM), o_g, l_g)
        os_.append(o_g)
        ls.append(l_g)
        keep = min(window, T)
        kv = pp[:Np, gi * 3 * HW + HW:(gi + 1) * 3 * HW].reshape(Bp, T, 2, DIL_HEADS, HEAD_DIM)
        dil_p.append(kv[:, T - keep:][None])
        dil_s.append(nbuf.reshape(buf.shape)[None])
    x = out_proj(dil_merge(os_, ls), bf(dil_w_out[0]), x)
    x = ffn_dense(x, norm_ffn[2], bf(ffn_w_up[1]), bf(ffn_w_down[1]))

    QW = NSA_HEADS * HEAD_DIM
    KW = NSA_KV_HEADS * HEAD_DIM
    w_in = jnp.concatenate([nsa_w_in[0], nsa_w_gate[0],
                            jnp.zeros((D, LANES - nsa_w_gate.shape[2]), F32)], axis=1)
    proj = norm_proj(x, norm_mix[3], bf(w_in), w_in.shape[1] // 3)
    ones_q = jnp.ones((NSA_HEADS, HEAD_DIM), F32)
    ones_k = jnp.ones((NSA_KV_HEADS, HEAD_DIM), F32)
    gains = jnp.concatenate([ones_q * nsa_qn_g[0]]
                            + [blk for s in range(3) for blk in (ones_k * nsa_kn_g[0, s], ones_k)]
                            + [jnp.ones((1, HEAD_DIM), F32)], axis=0)
    flags = [True] * NSA_HEADS + ([True] * NSA_KV_HEADS + [False] * NSA_KV_HEADS) * 3 + [False]
    q, new, win, gate = prep_heads(proj, cos_h, sin_h, gains, flags,
                                   [(0, QW), (QW, QW + 4 * KW), (QW + 4 * KW, QW + 6 * KW),
                                    (QW + 6 * KW, QW + 6 * KW + LANES)])
    w1, w2 = bf(nsa_cmp_w1[0]), bf(nsa_cmp_w2[0])
    ppages = new.reshape(M // PAGE, PAGE, 4 * KW)
    ptab = jnp.arange(Bp * (T // PAGE), dtype=I32)
    kc_p = nsa_compress(ppages, ptab, Bp, T // PAGE, nsa_cmp_pe[0], w1, w2)
    o = nsa_attn(q, gate, kc_p, ppages, ptab, win.reshape(M // PAGE, PAGE, 2 * KW), Bp, T)
    pool = cache_nsa_kv[0].reshape(cache_nsa_kv.shape[1], -1, HEAD_DIM)
    wbuf = cache_nsa_win[0]
    o, win_s = nsa_sample(q, gate, new, win, pool, page_table.reshape(-1).astype(I32),
                          wbuf.reshape(Bs, -1, HEAD_DIM), nsa_cmp_pe[0], w1, w2, Bs, Ts, Np, past, o)
    x = out_proj(o, bf(nsa_w_out[0]), x)
    x = moe_layer(x, norm_ffn[3], moe_router_w[1], moe_up, moe_down, 1)

    keep = min(NSA_WINDOW, T)
    win_p = win[:Np].reshape(Bp, T, 2, NSA_KV_HEADS, HEAD_DIM)[:, T - keep:]
    win_s = win_s.reshape(wbuf.shape)
    return (x[:Np].reshape(Bp, T, D), x[Np:].reshape(Bs, Ts, D),
            ret_p[None], ret_s[None], hg_p[None], hg_s[None],
            dil_p[0], dil_s[0], dil_p[1], dil_s[1], dil_p[2], dil_s[2],
            new[:Np].reshape(Bp, T, 4, NSA_KV_HEADS, HEAD_DIM)[None],
            new[Np:].reshape(Bs, Ts, 4, NSA_KV_HEADS, HEAD_DIM)[None],
            win_p[None], win_s[None])
```

```python
import functools
import math

import numpy as np
import jax
import jax.numpy as jnp
from jax import lax
from jax.experimental import pallas as pl
from jax.experimental.pallas import tpu as pltpu

F32 = jnp.float32
BF16 = jnp.bfloat16
I32 = jnp.int32

EPS = 1e-6
NEG = -1e30
ROPE_THETA = 10000.0
LANES = 128
VMEM_LIMIT = 56 * 1024 * 1024

HEAD_DIM = 128
RET_HEADS, RET_DK, RET_DV, RET_CHUNK = 4, 256, 512, 128
HG_HEADS, HG_DK, HG_CHUNK = 8, 128, 64
DIL_HEADS = 4
DIL_GROUPS = ((128, 1), (512, 4), (2048, 16))
DIL_QBLOCK = 128
NSA_HEADS, NSA_KV_HEADS, NSA_GROUP = 8, 2, 4
NSA_CMP_LEN, NSA_CMP_STRIDE = 32, 16
NSA_SLC_LEN, NSA_TOP_N, NSA_WINDOW = 64, 16, 512
PAGE = 128
N_EXPERTS = 8
MOE_ROWS = 512


def _cp(*sem):
    return pltpu.CompilerParams(dimension_semantics=sem, vmem_limit_bytes=VMEM_LIMIT)


def _tile(n, pref):
    t = pref
    while n % t:
        t //= 2
    return t


def _dot(a, b):
    return jnp.dot(a, b, preferred_element_type=F32)


def _dot_nt(a, b):
    return lax.dot_general(a, b, (((1,), (1,)), ((), ())), preferred_element_type=F32)


def _dot_tn(a, b):
    return lax.dot_general(a, b, (((0,), (0,)), ((), ())), preferred_element_type=F32)


def _split3(x):
    hi = x.astype(BF16)
    r = x - hi.astype(F32)
    mid = r.astype(BF16)
    lo = (r - mid.astype(F32)).astype(BF16)
    return hi, mid, lo


def _rms(x, g):
    return x * lax.rsqrt(jnp.mean(x * x, axis=-1, keepdims=True) + EPS) * g


def _sigmoid(x):
    return 1.0 / (1.0 + jnp.exp(-x))


def _silu(x):
    return x * _sigmoid(x)


def _norm_proj_kernel(x_ref, g_ref, w_ref, o_ref, xn_ref):
    @pl.when(pl.program_id(1) == 0)
    def _():
        xn_ref[...] = _rms(x_ref[...], g_ref[...]).astype(BF16)

    o_ref[...] = _dot(xn_ref[...], w_ref[...]).astype(o_ref.dtype)


def norm_proj(x, g, w, tn):
    M, D = x.shape
    N = w.shape[1]
    tm = _tile(M, 1024)
    return pl.pallas_call(
        _norm_proj_kernel,
        grid=(M // tm, N // tn),
        in_specs=[pl.BlockSpec((tm, D), lambda i, j: (i, 0)),
                  pl.BlockSpec((1, D), lambda i, j: (0, 0)),
                  pl.BlockSpec((D, tn), lambda i, j: (0, j))],
        out_specs=pl.BlockSpec((tm, tn), lambda i, j: (i, j)),
        out_shape=jax.ShapeDtypeStruct((M, N), F32),
        scratch_shapes=[pltpu.VMEM((tm, D), BF16)],
        compiler_params=_cp("parallel", "arbitrary"),
        name="norm_proj",
    )(x, g.reshape(1, D), w)


def _out_proj_kernel(a_ref, w_ref, x_ref, o_ref):
    o_ref[...] = x_ref[...] + _dot(a_ref[...].astype(BF16), w_ref[...])


def out_proj(a, w, x):
    M, K = a.shape
    D = w.shape[1]
    tm = _tile(M, 512)
    return pl.pallas_call(
        _out_proj_kernel,
        grid=(M // tm,),
        in_specs=[pl.BlockSpec((tm, K), lambda i: (i, 0)),
                  pl.BlockSpec((K, D), lambda i: (0, 0)),
                  pl.BlockSpec((tm, D), lambda i: (i, 0))],
        out_specs=pl.BlockSpec((tm, D), lambda i: (i, 0)),
        out_shape=jax.ShapeDtypeStruct((M, D), F32),
        compiler_params=_cp("parallel"),
        name="out_proj",
    )(a, w, x)


def _ffn_dense_kernel(x_ref, g_ref, wg_ref, wu_ref, wd_ref, o_ref, xn_ref, acc_ref, *, n_f):
    f = pl.program_id(1)

    @pl.when(f == 0)
    def _():
        xn_ref[...] = _rms(x_ref[...], g_ref[...]).astype(BF16)
        acc_ref[...] = jnp.zeros_like(acc_ref)

    xn = xn_ref[...]
    hg = _dot(xn, wg_ref[...])
    hu = _dot(xn, wu_ref[...])
    acc_ref[...] += _dot((_silu(hg) * hu).astype(BF16), wd_ref[...])

    @pl.when(f == n_f - 1)
    def _():
        o_ref[...] = x_ref[...] + acc_ref[...]


def ffn_dense(x, g, w_up, w_down):
    M, D = x.shape
    F = w_down.shape[0]
    tm = _tile(M, 512)
    tf = F // 2 if (F // 2) % LANES == 0 else F
    n_f = F // tf
    return pl.pallas_call(
        functools.partial(_ffn_dense_kernel, n_f=n_f),
        grid=(M // tm, n_f),
        in_specs=[pl.BlockSpec((tm, D), lambda i, f: (i, 0)),
                  pl.BlockSpec((1, D), lambda i, f: (0, 0)),
                  pl.BlockSpec((D, tf), lambda i, f: (0, f)),
                  pl.BlockSpec((D, tf), lambda i, f: (0, n_f + f)),
                  pl.BlockSpec((tf, D), lambda i, f: (f, 0))],
        out_specs=pl.BlockSpec((tm, D), lambda i, f: (i, 0)),
        out_shape=jax.ShapeDtypeStruct((M, D), F32),
        scratch_shapes=[pltpu.VMEM((tm, D), BF16), pltpu.VMEM((tm, D), F32)],
        compiler_params=_cp("parallel", "arbitrary"),
        name="ffn_dense",
    )(x, g.reshape(1, D), w_up, w_up, w_down)


def _ret_kernel(*refs, C, has_s0, has_prev):
    q_ref, k_ref, v_ref, g_ref, cos_ref, sin_ref, gng_ref, gnb_ref = refs[:8]
    n = 8
    s0_ref = None
    if has_s0:
        s0_ref = refs[n]
        n += 1
    if has_prev:
        n += 1
    y_ref, s_ref = refs[n], refs[n + 1]
    c = pl.program_id(1)

    @pl.when(c == 0)
    def _():
        if has_s0:
            s_ref[...] = s0_ref[...]
        else:
            s_ref[...] = jnp.zeros_like(s_ref)

    cos, sin = cos_ref[...], sin_ref[...]
    half = RET_DK // 2
    t = lax.broadcasted_iota(I32, (C, 1), 0).astype(F32)
    diff = (lax.broadcasted_iota(I32, (C, C), 0) - lax.broadcasted_iota(I32, (C, C), 1)).astype(F32)

    def rot(x):
        x1, x2 = x[:, :half], x[:, half:]
        return jnp.concatenate([x1 * cos - x2 * sin, x2 * cos + x1 * sin], axis=1)

    for h in range(RET_HEADS):
        lg = math.log(1.0 - 2.0 ** (-5.0 - h))
        qr = rot(q_ref[:, h * RET_DK:(h + 1) * RET_DK])
        kr = rot(k_ref[:, h * RET_DK:(h + 1) * RET_DK]) * RET_DK ** -0.5
        v = v_ref[:, h * RET_DV:(h + 1) * RET_DV].astype(BF16)
        decay = jnp.where(diff >= 0, jnp.exp(lg * jnp.maximum(diff, 0.0)), 0.0)
        sc = _dot_nt(qr.astype(BF16), kr.astype(BF16)) * decay
        S = s_ref[0, h]
        o = _dot(sc.astype(BF16), v) + _dot((qr * jnp.exp(lg * (t + 1.0))).astype(BF16), S.astype(BF16))
        kd = (kr * jnp.exp(lg * (C - 1.0 - t))).astype(BF16)
        s_ref[0, h] = math.exp(lg * C) * S + _dot_tn(kd, v)
        mu = jnp.mean(o, axis=-1, keepdims=True)
        var = jnp.mean(jnp.square(o - mu), axis=-1, keepdims=True)
        sl = slice(h * RET_DV, (h + 1) * RET_DV)
        on = (o - mu) * lax.rsqrt(var + EPS) * gng_ref[:, sl] + gnb_ref[:, sl]
        y_ref[:, sl] = _silu(g_ref[:, sl]) * on


def ret_core(proj, B, T, row0, cos, sin, tab0, gng, gnb, s0, y_prev):
    M = proj.shape[0]
    C = RET_CHUNK if T % RET_CHUNK == 0 else T
    nC = T // C
    r0, t0 = row0 // C, tab0 // C
    qkw, vw = RET_HEADS * RET_DK, RET_HEADS * RET_DV
    row = lambda b, c: r0 + b * nC + c
    in_specs = [pl.BlockSpec((C, qkw), lambda b, c: (row(b, c), 0)),
                pl.BlockSpec((C, qkw), lambda b, c: (row(b, c), 1)),
                pl.BlockSpec((C, vw), lambda b, c: (row(b, c), 2 * qkw // vw)),
                pl.BlockSpec((C, vw), lambda b, c: (row(b, c), 2 * qkw // vw + 1)),
                pl.BlockSpec((C, RET_DK // 2), lambda b, c: (t0 + c, 0)),
                pl.BlockSpec((C, RET_DK // 2), lambda b, c: (t0 + c, 0)),
                pl.BlockSpec((1, vw), lambda b, c: (0, 0)),
                pl.BlockSpec((1, vw), lambda b, c: (0, 0))]
    args = [proj, proj, proj, proj, cos, sin, gng.reshape(1, vw), gnb.reshape(1, vw)]
    sblock = (1, RET_HEADS, RET_DK, RET_DV)
    if s0 is not None:
        in_specs.append(pl.BlockSpec(sblock, lambda b, c: (b, 0, 0, 0)))
        args.append(s0)
    aliases = {}
    if y_prev is not None:
        in_specs.append(pl.BlockSpec(memory_space=pl.ANY))
        aliases = {len(args): 0}
        args.append(y_prev)
    return pl.pallas_call(
        functools.partial(_ret_kernel, C=C, has_s0=s0 is not None, has_prev=y_prev is not None),
        grid=(B, nC),
        in_specs=in_specs,
        out_specs=[pl.BlockSpec((C, vw), lambda b, c: (row(b, c), 0)),
                   pl.BlockSpec(sblock, lambda b, c: (b, 0, 0, 0))],
        out_shape=[jax.ShapeDtypeStruct((M, vw), F32),
                   jax.ShapeDtypeStruct((B,) + sblock[1:], F32)],
        input_output_aliases=aliases,
        compiler_params=_cp("parallel", "arbitrary"),
        name="ret_core",
    )(*args)


def _hgrn_levels(C):
    out, m = [], C // 2
    while m >= 1:
        out.append(m)
        m //= 2
    return out


def _hgrn_mats(C):
    t = np.arange(C)[:, None]
    u = np.arange(C)[None, :]
    mats = [(u <= t).astype(np.float32)]
    for m in _hgrn_levels(C):
        r = 2 * m * (t // (2 * m)) + m - 1
        a = ((u > r) & (u <= t)).astype(np.float32) - ((u > t) & (u <= r)).astype(np.float32)
        mats.append(a)
    return np.stack(mats)


def _hgrn_kernel(*refs, C, has_s0, has_prev):
    q_ref, f_ref, i_ref, g_ref, lb_ref, gng_ref, a_ref = refs[:7]
    n = 7
    s0_ref = None
    if has_s0:
        s0_ref = refs[n]
        n += 1
    if has_prev:
        n += 1
    y_ref, s_ref = refs[n], refs[n + 1]
    c = pl.program_id(1)

    @pl.when(c == 0)
    def _():
        if has_s0:
            s_ref[...] = s0_ref[...]
        else:
            s_ref[...] = jnp.zeros_like(s_ref)

    lb = lb_ref[...]
    f = lb + (1.0 - lb) * _sigmoid(f_ref[...])
    logf = jnp.log(f)
    kk = 1.0 - f
    q = _silu(q_ref[...])
    v = i_ref[...].astype(BF16)
    parts = _split3(logf)

    def amul(idx):
        a = a_ref[idx]
        return _dot(a, parts[0]) + _dot(a, parts[1]) + _dot(a, parts[2])

    bcum = amul(0)
    blast = bcum[C - 1:C, :]
    qb = (q * jnp.exp(bcum)).astype(BF16)
    kst = (kk * jnp.exp(blast - bcum)).astype(BF16)
    eb = jnp.exp(blast)

    ti = lax.broadcasted_iota(I32, (C, C), 0)
    si = lax.broadcasted_iota(I32, (C, C), 1)
    terms = [(q.astype(BF16), kk.astype(BF16), ti == si)]
    for li, m in enumerate(_hgrn_levels(C)):
        dq = amul(1 + li)
        qt = (q * jnp.exp(jnp.minimum(dq, 0.0))).astype(BF16)
        kt = (kk * jnp.exp(jnp.minimum(-dq, 0.0))).astype(BF16)
        blk = 2 * m
        mask = ((ti // blk) == (si // blk)) & ((ti % blk) >= m) & ((si % blk) < m)
        terms.append((qt, kt, mask))

    eye = (lax.broadcasted_iota(I32, (HG_DK, HG_DK), 0) == lax.broadcasted_iota(I32, (HG_DK, HG_DK), 1))
    for h in range(HG_HEADS):
        sl = slice(h * HG_DK, (h + 1) * HG_DK)
        sc = jnp.zeros((C, C), F32)
        for qt, kt, mask in terms:
            sc = sc + jnp.where(mask, _dot_nt(qt[:, sl], kt[:, sl]), 0.0)
        S = s_ref[0, h]
        vh = v[:, sl]
        o = _dot(sc.astype(BF16), vh) + _dot(qb[:, sl], S.astype(BF16))
        ecol = jnp.sum(jnp.where(eye, eb[:, sl], 0.0), axis=1, keepdims=True)
        s_ref[0, h] = ecol * S + _dot_tn(kst[:, sl], vh)
        on = o * lax.rsqrt(jnp.mean(o * o, axis=-1, keepdims=True) + EPS)
        y_ref[:, sl] = on * gng_ref[:, sl] * _sigmoid(g_ref[:, sl])


def hgrn_core(proj, B, T, row0, lb, gng, s0, y_prev):
    M = proj.shape[0]
    W = HG_HEADS * HG_DK
    C = HG_CHUNK if T % HG_CHUNK == 0 else T
    nC = T // C
    r0 = row0 // C
    row = lambda b, c: r0 + b * nC + c
    amats = jnp.asarray(_hgrn_mats(C), BF16)
    col = lambda j: pl.BlockSpec((C, W), lambda b, c: (row(b, c), j))
    in_specs = [col(j) for j in range(4)]
    in_specs += [pl.BlockSpec((1, W), lambda b, c: (0, 0)),
                 pl.BlockSpec((1, W), lambda b, c: (0, 0)),
                 pl.BlockSpec(amats.shape, lambda b, c: (0, 0, 0))]
    args = [proj, proj, proj, proj, lb.reshape(1, W), gng.reshape(1, W), amats]
    sblock = (1, HG_HEADS, HG_DK, HG_DK)
    if s0 is not None:
        in_specs.append(pl.BlockSpec(sblock, lambda b, c: (b, 0, 0, 0)))
        args.append(s0)
    aliases = {}
    if y_prev is not None:
        in_specs.append(pl.BlockSpec(memory_space=pl.ANY))
        aliases = {len(args): 0}
        args.append(y_prev)
    return pl.pallas_call(
        functools.partial(_hgrn_kernel, C=C, has_s0=s0 is not None, has_prev=y_prev is not None),
        grid=(B, nC),
        in_specs=in_specs,
        out_specs=[pl.BlockSpec((C, W), lambda b, c: (row(b, c), 0)),
                   pl.BlockSpec(sblock, lambda b, c: (b, 0, 0, 0))],
        out_shape=[jax.ShapeDtypeStruct((M, W), F32),
                   jax.ShapeDtypeStruct((B,) + sblock[1:], F32)],
        input_output_aliases=aliases,
        compiler_params=_cp("parallel", "arbitrary"),
        name="hgrn_core",
    )(*args)


def _prep_kernel(x_ref, cos_ref, sin_ref, gain_ref, *o_refs, flags, outs, classes):
    cos, sin = cos_ref[...], sin_ref[...]
    tm = x_ref.shape[0]
    c_refs = o_refs[len(outs):len(outs) + len(classes)]
    stage = o_refs[-1] if classes else None
    for (lo, hi), o_ref in zip(outs, o_refs):
        for j in range(lo // LANES, hi // LANES):
            xj = x_ref[:, j * LANES:(j + 1) * LANES]
            if flags[j]:
                y = _rms(xj, gain_ref[j:j + 1, :])
                xj = y * cos + pltpu.roll(y, LANES // 2, 1) * sin
            o_ref[:, j * LANES - lo:(j + 1) * LANES - lo] = xj.astype(o_ref.dtype)
            if any(clo <= j * LANES < chi for clo, chi, _ in classes):
                stage[j] = xj
    for (lo, hi, d), c_ref in zip(classes, c_refs):
        for r in range(d):
            for j in range(lo // LANES, hi // LANES):
                c_ref[r, :, j * LANES - lo:(j + 1) * LANES - lo] = stage[j, pl.ds(r, tm // d, stride=d), :]


def prep_heads(x, cos, sin, gains, flags, outs, classes=()):
    M, W = x.shape
    tm = _tile(M, 256)
    nb = W // LANES
    return pl.pallas_call(
        functools.partial(_prep_kernel, flags=tuple(flags), outs=tuple(outs), classes=tuple(classes)),
        grid=(M // tm,),
        in_specs=[pl.BlockSpec((tm, W), lambda i: (i, 0)),
                  pl.BlockSpec((tm, LANES), lambda i: (i, 0)),
                  pl.BlockSpec((tm, LANES), lambda i: (i, 0)),
                  pl.BlockSpec((nb, LANES), lambda i: (0, 0))],
        out_specs=([pl.BlockSpec((tm, hi - lo), lambda i: (i, 0)) for lo, hi in outs]
                   + [pl.BlockSpec((d, tm // d, hi - lo), lambda i: (0, i, 0)) for lo, hi, d in classes]),
        out_shape=([jax.ShapeDtypeStruct((M, hi - lo), F32) for lo, hi in outs]
                   + [jax.ShapeDtypeStruct((d, M // d, hi - lo), F32) for lo, hi, d in classes]),
        scratch_shapes=[pltpu.VMEM((nb, tm, LANES), F32)] if classes else [],
        compiler_params=_cp("parallel"),
        name="prep_heads",
    )(x, cos, sin, gains)


def _softmax_head(s, valid, v):
    s = jnp.where(valid, s, NEG)
    m = jnp.max(s, axis=-1, keepdims=True)
    p = jnp.exp(s - m)
    l = jnp.sum(p, axis=-1, keepdims=True)
    o = _dot(p.astype(BF16), v) / l
    return o, m + jnp.log(l)


def _dil_prompt_kernel(q_ref, kc_ref, kp_ref, vc_ref, vp_ref, o_ref, l_ref, *, tq, reach):
    qi = pl.program_id(2)
    scale = HEAD_DIM ** -0.5
    qpos = qi * tq + lax.broadcasted_iota(I32, (tq, 2 * tq), 0)
    kpos = (qi - 1) * tq + lax.broadcasted_iota(I32, (tq, 2 * tq), 1)
    delta = qpos - kpos
    valid = (delta >= 0) & (delta <= reach) & (kpos >= 0)
    lane = lax.broadcasted_iota(I32, (tq, LANES), 1)
    lse_all = jnp.zeros((tq, LANES), F32)
    for h in range(DIL_HEADS):
        sl = slice(h * HEAD_DIM, (h + 1) * HEAD_DIM)
        q = q_ref[:, sl].astype(BF16)
        k = jnp.concatenate([kp_ref[:, sl], kc_ref[:, sl]], axis=0).astype(BF16)
        v = jnp.concatenate([vp_ref[:, sl], vc_ref[:, sl]], axis=0).astype(BF16)
        o, lse = _softmax_head(_dot_nt(q, k) * scale, valid, v)
        o_ref[h] = o
        lse_all = jnp.where(lane == h, lse, lse_all)
    l_ref[...] = lse_all


def dil_prompt(qkv, gi, col0, B, T):
    window, d = DIL_GROUPS[gi]
    HW = DIL_HEADS * HEAD_DIM
    Tc = T // d
    tq = min(DIL_QBLOCK, Tc)
    nq = Tc // tq

    def blk(off, prev):
        return pl.BlockSpec(
            (None, tq, HW),
            lambda b, r, qi: (r, b * nq + (jnp.maximum(qi - 1, 0) if prev else qi), col0 + off))

    return pl.pallas_call(
        functools.partial(_dil_prompt_kernel, tq=tq, reach=window // d),
        grid=(B, d, nq),
        in_specs=[blk(0, False), blk(1, False), blk(1, True), blk(2, False), blk(2, True)],
        out_specs=[pl.BlockSpec((None, DIL_HEADS, tq, HEAD_DIM), lambda b, r, qi: (r, 0, b * nq + qi, 0)),
                   pl.BlockSpec((None, tq, LANES), lambda b, r, qi: (r, b * nq + qi, 0))],
        out_shape=[jax.ShapeDtypeStruct((d, DIL_HEADS, B * Tc, HEAD_DIM), F32),
                   jax.ShapeDtypeStruct((d, B * Tc, LANES), F32)],
        compiler_params=_cp("parallel", "parallel", "arbitrary"),
        name=f"dil_prompt_{gi}",
    )(qkv, qkv, qkv, qkv, qkv)


def _dil_decode_kernel(q_ref, kn_ref, vn_ref, buf_ref, o_ref, l_ref, nb_ref, *, L, Ts, d, window):
    H = DIL_HEADS
    rpp = 2 * H
    scale = HEAD_DIM ** -0.5
    t = lax.broadcasted_iota(I32, (Ts, L + Ts), 0)
    i = lax.broadcasted_iota(I32, (Ts, L + Ts), 1)
    delta = L + t - i
    valid = (delta >= 0) & (delta % d == 0) & (delta <= window)
    lane = lax.broadcasted_iota(I32, (Ts, LANES), 1)
    lse_all = jnp.zeros((Ts, LANES), F32)
    for h in range(H):
        sl = slice(h * HEAD_DIM, (h + 1) * HEAD_DIM)
        q = q_ref[:, sl].astype(BF16)
        k = jnp.concatenate([buf_ref[0, pl.ds(h, L, stride=rpp), :], kn_ref[:, sl]], axis=0).astype(BF16)
        v = jnp.concatenate([buf_ref[0, pl.ds(H + h, L, stride=rpp), :], vn_ref[:, sl]], axis=0).astype(BF16)
        o, lse = _softmax_head(_dot_nt(q, k) * scale, valid, v)
        o_ref[h] = o
        lse_all = jnp.where(lane == h, lse, lse_all)
        nb_ref[0, pl.ds((L - Ts) * rpp + h, Ts, stride=rpp), :] = kn_ref[:, sl]
        nb_ref[0, pl.ds((L - Ts) * rpp + H + h, Ts, stride=rpp), :] = vn_ref[:, sl]
    l_ref[...] = lse_all
    nb_ref[0, 0:(L - Ts) * rpp, :] = buf_ref[0, Ts * rpp:L * rpp, :]


def dil_decode(pp, gi, B, Ts, row0, buf):
    window, d = DIL_GROUPS[gi]
    HW = DIL_HEADS * HEAD_DIM
    rows = buf.shape[1]
    L = rows // (2 * DIL_HEADS)
    r0 = row0 // Ts
    col = lambda off: pl.BlockSpec((Ts, HW), lambda b: (r0 + b, gi * 3 + off))
    return pl.pallas_call(
        functools.partial(_dil_decode_kernel, L=L, Ts=Ts, d=d, window=window),
        grid=(B,),
        in_specs=[col(0), col(1), col(2),
                  pl.BlockSpec((1, rows, HEAD_DIM), lambda b: (b, 0, 0))],
        out_specs=[pl.BlockSpec((None, DIL_HEADS, Ts, HEAD_DIM), lambda b: (0, 0, b, 0)),
                   pl.BlockSpec((None, Ts, LANES), lambda b: (0, b, 0)),
                   pl.BlockSpec((1, rows, HEAD_DIM), lambda b: (b, 0, 0))],
        out_shape=[jax.ShapeDtypeStruct((1, DIL_HEADS, B * Ts, HEAD_DIM), F32),
                   jax.ShapeDtypeStruct((1, B * Ts, LANES), F32),
                   jax.ShapeDtypeStruct((B, rows, HEAD_DIM), F32)],
        compiler_params=_cp("parallel"),
        name=f"dil_decode_{gi}",
    )(pp, pp, pp, buf)


def _dil_merge_kernel(*refs, ds):
    ng = len(ds)
    o_refs, l_refs, out_ref, stage = refs[:ng], refs[ng:2 * ng], refs[2 * ng], refs[2 * ng + 1]
    dmax = max(ds)
    n = out_ref.shape[0] // dmax
    for r in range(dmax):
        def rows(d):
            return r % d, pl.ds(r // d, n, stride=dmax // d)

        ls = [l_refs[g][rows(ds[g]) + (slice(None),)] for g in range(ng)]
        m = functools.reduce(jnp.maximum, ls)
        es = [jnp.exp(l - m) for l in ls]
        den = functools.reduce(lambda a, b: a + b, es)
        ws = [e / den for e in es]
        for h in range(DIL_HEADS):
            acc = None
            for g in range(ng):
                c, rs = rows(ds[g])
                term = ws[g][:, h:h + 1] * o_refs[g][c, h, rs, :]
                acc = term if acc is None else acc + term
            stage[h, pl.ds(r, n, stride=dmax), :] = acc
    for h in range(DIL_HEADS):
        out_ref[:, h * HEAD_DIM:(h + 1) * HEAD_DIM] = stage[h]


def dil_merge(os_, ls, rows):
    ds = tuple(o.shape[0] for o in os_)
    H, D = os_[0].shape[1], os_[0].shape[3]
    tm = _tile(rows, 256)
    ospec = lambda a: pl.BlockSpec((a.shape[0], H, tm // a.shape[0], D), lambda i: (0, 0, i, 0))
    lspec = lambda a: pl.BlockSpec((a.shape[0], tm // a.shape[0], a.shape[2]), lambda i: (0, i, 0))
    return pl.pallas_call(
        functools.partial(_dil_merge_kernel, ds=ds),
        grid=(rows // tm,),
        in_specs=[ospec(a) for a in os_] + [lspec(a) for a in ls],
        out_specs=pl.BlockSpec((tm, H * D), lambda i: (i, 0)),
        out_shape=jax.ShapeDtypeStruct((rows, H * D), F32),
        scratch_shapes=[pltpu.VMEM((H, tm, D), F32)],
        compiler_params=_cp("parallel"),
        name="dil_merge",
    )(*os_, *ls)


def _nsa_compress_kernel(tab_ref, pg_ref, pe_ref, w1_ref, w2_ref, o_ref, hs_ref, *, n_pages, ncmp):
    del tab_ref
    p = pl.program_id(1)
    hb = PAGE // NSA_CMP_STRIDE
    ncol = 2 * NSA_KV_HEADS
    rowlen = 4 * NSA_KV_HEADS * HEAD_DIM
    for c in range(ncol):
        for j in range(NSA_CMP_STRIDE):
            hs_ref[c, pl.ds(pl.multiple_of(p * hb, hb), hb), j * HEAD_DIM:(j + 1) * HEAD_DIM] = (
                pg_ref[0, :, j * rowlen + c * HEAD_DIM:j * rowlen + (c + 1) * HEAD_DIM])

    @pl.when(p == n_pages - 1)
    def _():
        half = NSA_CMP_STRIDE * HEAD_DIM
        R = n_pages * hb
        rowi = lax.broadcasted_iota(I32, (R, HEAD_DIM), 0)
        for c in range(ncol):
            s = c // NSA_KV_HEADS
            H = hs_ref[c]
            a = _dot((H + pe_ref[s, 0:1, :]).astype(BF16), w1_ref[s, 0:half, :])
            bm = _dot((H + pe_ref[s, 1:2, :]).astype(BF16), w1_ref[s, half:2 * half, :])
            pre = a + pltpu.roll(bm, R - 1, 0)
            out = _dot(_silu(pre).astype(BF16), w2_ref[s])
            o_ref[0, c] = jnp.where(rowi < ncmp, out, 0.0).astype(BF16)


def nsa_compress(pages, table, B, n_pages, pe, w1, w2):
    hb = PAGE // NSA_CMP_STRIDE
    R = n_pages * hb
    ncmp = (n_pages * PAGE - NSA_CMP_LEN) // NSA_CMP_STRIDE + 1
    half = NSA_CMP_STRIDE * HEAD_DIM
    pv = pages.reshape(pages.shape[0], hb, NSA_CMP_STRIDE * pages.shape[2])
    grid_spec = pltpu.PrefetchScalarGridSpec(
        num_scalar_prefetch=1,
        grid=(B, n_pages),
        in_specs=[pl.BlockSpec((1,) + pv.shape[1:], lambda b, p, tab: (tab[b * n_pages + p], 0, 0)),
                  pl.BlockSpec((2, 2, half), lambda b, p, tab: (0, 0, 0)),
                  pl.BlockSpec(w1.shape, lambda b, p, tab: (0, 0, 0)),
                  pl.BlockSpec(w2.shape, lambda b, p, tab: (0, 0, 0))],
        out_specs=pl.BlockSpec((1, 4, R, HEAD_DIM), lambda b, p, tab: (b, 0, 0, 0)),
        scratch_shapes=[pltpu.VMEM((4, R, half), F32)])
    return pl.pallas_call(
        functools.partial(_nsa_compress_kernel, n_pages=n_pages, ncmp=ncmp),
        grid_spec=grid_spec,
        out_shape=jax.ShapeDtypeStruct((B, 4, R, HEAD_DIM), BF16),
        compiler_params=_cp("parallel", "arbitrary"),
        name="nsa_compress",
    )(table, pv, pe.reshape(2, 2, half), w1, w2)


def _flash_update(m_ref, l_ref, a_ref, k, s, valid, v):
    s = jnp.where(valid, s, NEG)
    m_old = m_ref[k]
    m_new = jnp.maximum(m_old, jnp.max(s, axis=-1, keepdims=True))
    p = jnp.where(valid, jnp.exp(s - m_new), 0.0)
    alpha = jnp.exp(m_old - m_new)
    l_ref[k] = alpha * l_ref[k] + jnp.sum(p, axis=-1, keepdims=True)
    a_ref[k] = alpha * a_ref[k] + _dot(p.astype(BF16), v)
    m_ref[k] = m_new


def _nsa_qstack(q_ref, k):
    G, D = NSA_GROUP, HEAD_DIM
    return jnp.concatenate([q_ref[:, (k * G + g) * D:(k * G + g + 1) * D] for g in range(G)],
                           axis=0).astype(BF16)


def _nsa_cmp_select(qs, kc, vc, c2s, pos_r, pos_q, n_slc, ncmp):
    R, tq = qs.shape[0], pos_q.shape[0]
    nidx = lax.broadcasted_iota(I32, (R, kc.shape[0]), 1)
    s = _dot_nt(qs, kc) * HEAD_DIM ** -0.5
    valid = (NSA_CMP_STRIDE * nidx + NSA_CMP_LEN - 1 <= pos_r) & (nidx < ncmp)
    s = jnp.where(valid, s, NEG)
    pr = jnp.where(valid, jnp.exp(s - jnp.max(s, axis=-1, keepdims=True)), 0.0)
    pc = pr / jnp.maximum(jnp.sum(pr, axis=-1, keepdims=True), 1e-30)
    o_cmp = _dot(pc.astype(BF16), vc)
    psum = pc[0:tq]
    for g in range(1, NSA_GROUP):
        psum = psum + pc[g * tq:(g + 1) * tq]
    ph, pm, plo = _split3(psum)
    imp = _dot(ph, c2s) + _dot(pm, c2s) + _dot(plo, c2s)
    j = lax.broadcasted_iota(I32, (tq, LANES), 1)
    cur = pos_q // NSA_SLC_LEN
    forced = (j == 0) | (j == cur) | (j == cur - 1)
    score = jnp.where(forced, 1e4, jnp.where(j <= cur, imp, -1.0))
    score = jnp.where(j < n_slc, score, -2.0)
    cnt = jnp.zeros((tq, LANES), F32)
    for jp in range(n_slc):
        col = score[:, jp:jp + 1]
        cnt = cnt + ((col > score) | ((col == score) & (jp < j))).astype(F32)
    return o_cmp, (cnt < float(min(NSA_TOP_N, n_slc))).astype(F32)


def _nsa_attn_kernel(qi_ref, kb_ref, q_ref, gate_ref, kc_ref, c2s_ref, exp_ref, kv_ref, wkv_ref, o_ref,
                     ocmp, sel, m_s, l_s, a_s, m_w, l_w, a_w, *, tq, KB, n_slc, ncmp):
    step = pl.program_id(1)
    qi, kb = qi_ref[step], kb_ref[step]
    G, KVH, D = NSA_GROUP, NSA_KV_HEADS, HEAD_DIM
    R = G * tq
    scale = D ** -0.5
    pos_r = qi * tq + lax.broadcasted_iota(I32, (R, 1), 0) % tq
    pos_q = qi * tq + lax.broadcasted_iota(I32, (tq, 1), 0)
    kb_last = (qi * tq + tq - 1) // KB
    qstack = functools.partial(_nsa_qstack, q_ref)

    @pl.when(kb == 0)
    def _():
        for k in range(KVH):
            ocmp[k], sel[k] = _nsa_cmp_select(qstack(k), kc_ref[0, k], kc_ref[0, KVH + k], c2s_ref[...],
                                              pos_r, pos_q, n_slc, ncmp)
        for ref in (m_s, m_w):
            ref[...] = jnp.full(ref.shape, NEG, F32)
        for ref in (l_s, a_s, l_w, a_w):
            ref[...] = jnp.zeros(ref.shape, F32)

    dist = pos_r - (kb * KB + lax.broadcasted_iota(I32, (R, KB), 1))
    for k in range(KVH):
        ks = kv_ref[:, k * D:(k + 1) * D].astype(BF16)
        vs = kv_ref[:, (KVH + k) * D:(KVH + k + 1) * D].astype(BF16)
        selrow = _dot(sel[k].astype(BF16), exp_ref[...])
        valid = (jnp.concatenate([selrow] * G, axis=0) > 0.5) & (dist >= 0)
        _flash_update(m_s, l_s, a_s, k, _dot_nt(qstack(k), ks) * scale, valid, vs)

    @pl.when(kb >= kb_last - NSA_WINDOW // KB)
    def _():
        valid = (dist >= 0) & (dist <= NSA_WINDOW)
        for k in range(KVH):
            kw = wkv_ref[:, k * D:(k + 1) * D].astype(BF16)
            vw = wkv_ref[:, (KVH + k) * D:(KVH + k + 1) * D].astype(BF16)
            _flash_update(m_w, l_w, a_w, k, _dot_nt(qstack(k), kw) * scale, valid, vw)

    @pl.when(kb == kb_last)
    def _():
        gs = _sigmoid(gate_ref[...])
        for k in range(KVH):
            _nsa_gated_out(o_ref, gs, k, tq, ocmp[k], a_s[k] / jnp.maximum(l_s[k], 1e-30),
                           a_w[k] / jnp.maximum(l_w[k], 1e-30))


def _nsa_gated_out(o_ref, gs, k, tq, o_c, o_s, o_w):
    for g in range(NSA_GROUP):
        hd = k * NSA_GROUP + g
        rows = slice(g * tq, (g + 1) * tq)
        o = (gs[:, 3 * hd:3 * hd + 1] * o_c[rows] + gs[:, 3 * hd + 1:3 * hd + 2] * o_s[rows]
             + gs[:, 3 * hd + 2:3 * hd + 3] * o_w[rows])
        o_ref[:, hd * HEAD_DIM:(hd + 1) * HEAD_DIM] = o


def _softmax_two(s1, ok1, v1, s2, ok2, v2):
    s1 = jnp.where(ok1, s1, NEG)
    s2 = jnp.where(ok2, s2, NEG)
    m = jnp.maximum(jnp.max(s1, axis=-1, keepdims=True), jnp.max(s2, axis=-1, keepdims=True))
    p1 = jnp.where(ok1, jnp.exp(s1 - m), 0.0)
    p2 = jnp.where(ok2, jnp.exp(s2 - m), 0.0)
    l = jnp.sum(p1, axis=-1, keepdims=True) + jnp.sum(p2, axis=-1, keepdims=True)
    return (_dot(p1.astype(BF16), v1) + _dot(p2.astype(BF16), v2)) / jnp.maximum(l, 1e-30)


def _nsa_sample_kernel(*refs, Ts, n_pages, Lw, n_slc, ncmp, past):
    (tab_ref, q_ref, gate_ref, xn_ref, xw_ref, wb_ref, pe_ref, w1_ref, w2_ref, c2s_ref, exp_ref) = refs[:11]
    pg_refs = refs[11:11 + n_pages]
    o_ref, nwb_ref, hs_ref = refs[12 + n_pages:]
    del tab_ref
    G, KVH, D = NSA_GROUP, NSA_KV_HEADS, HEAD_DIM
    R = G * Ts
    scale = D ** -0.5
    rpt = 4 * KVH
    hb = PAGE // NSA_CMP_STRIDE
    half = NSA_CMP_STRIDE * D
    pos_r = past + lax.broadcasted_iota(I32, (R, 1), 0) % Ts
    pos_q = past + lax.broadcasted_iota(I32, (Ts, 1), 0)

    def tile_rows(x):
        return jnp.concatenate([x] * G, axis=0)

    for u in range(n_pages):
        for c in range(2 * KVH):
            for j in range(NSA_CMP_STRIDE):
                hs_ref[c, u * hb:(u + 1) * hb, j * D:(j + 1) * D] = (
                    pg_refs[u][0, pl.ds(j * rpt + c, hb, stride=NSA_CMP_STRIDE * rpt), :])
    nrow = n_pages * hb
    rowi = lax.broadcasted_iota(I32, (nrow, D), 0)
    cmp = []
    for c in range(2 * KVH):
        s = c // KVH
        H = hs_ref[c]
        a = _dot((H + pe_ref[s, 0:1, :]).astype(BF16), w1_ref[s, 0:half, :])
        bm = _dot((H + pe_ref[s, 1:2, :]).astype(BF16), w1_ref[s, half:2 * half, :])
        out = _dot(_silu(a + pltpu.roll(bm, nrow - 1, 0)).astype(BF16), w2_ref[s])
        cmp.append(jnp.where(rowi < ncmp, out, 0.0).astype(BF16))

    gs = _sigmoid(gate_ref[...])
    jn = lax.broadcasted_iota(I32, (R, Ts), 1)
    causal = past + jn <= pos_r
    wpos = past - Lw + lax.broadcasted_iota(I32, (R, Lw), 1)
    wdist = pos_r - wpos
    wvalid = (wdist >= 0) & (wdist <= NSA_WINDOW)
    blk = past // NSA_SLC_LEN
    for k in range(KVH):
        qs = _nsa_qstack(q_ref, k)
        o_c, selk = _nsa_cmp_select(qs, cmp[k], cmp[KVH + k], c2s_ref[...], pos_r, pos_q, n_slc, ncmp)
        ks = jnp.concatenate([pg[0, pl.ds(2 * KVH + k, PAGE, stride=rpt), :] for pg in pg_refs],
                             axis=0).astype(BF16)
        vs = jnp.concatenate([pg[0, pl.ds(3 * KVH + k, PAGE, stride=rpt), :] for pg in pg_refs],
                             axis=0).astype(BF16)
        ok_past = tile_rows(_dot(selk.astype(BF16), exp_ref[...])) > 0.5
        ok_new = (tile_rows(selk[:, blk:blk + 1]) > 0.5) & causal
        kn = xn_ref[:, (2 * KVH + k) * D:(2 * KVH + k + 1) * D].astype(BF16)
        vn = xn_ref[:, (3 * KVH + k) * D:(3 * KVH + k + 1) * D].astype(BF16)
        o_s = _softmax_two(_dot_nt(qs, ks) * scale, ok_past, vs, _dot_nt(qs, kn) * scale, ok_new, vn)
        kw = wb_ref[0, pl.ds(k, Lw, stride=2 * KVH), :].astype(BF16)
        vw = wb_ref[0, pl.ds(KVH + k, Lw, stride=2 * KVH), :].astype(BF16)
        kwn = xw_ref[:, k * D:(k + 1) * D]
        vwn = xw_ref[:, (KVH + k) * D:(KVH + k + 1) * D]
        o_w = _softmax_two(_dot_nt(qs, kw) * scale, wvalid, vw,
                           _dot_nt(qs, kwn.astype(BF16)) * scale, causal, vwn.astype(BF16))
        _nsa_gated_out(o_ref, gs, k, Ts, o_c, o_s, o_w)
        nwb_ref[0, pl.ds((Lw - Ts) * 2 * KVH + k, Ts, stride=2 * KVH), :] = kwn
        nwb_ref[0, pl.ds((Lw - Ts) * 2 * KVH + KVH + k, Ts, stride=2 * KVH), :] = vwn
    nwb_ref[0, 0:(Lw - Ts) * 2 * KVH, :] = wb_ref[0, Ts * 2 * KVH:Lw * 2 * KVH, :]


def nsa_sample(q, gate, xnew, xwin, pool, table, wbuf, pe, w1, w2, B, Ts, row0, past, o_prev):
    M, QW = q.shape
    D = HEAD_DIM
    n_pages = past // PAGE
    assert past % PAGE == 0 and Ts < NSA_CMP_STRIDE
    Lw = wbuf.shape[1] // (2 * NSA_KV_HEADS)
    L = past + Ts
    n_slc = -(-L // NSA_SLC_LEN)
    ncmp = (L - NSA_CMP_LEN) // NSA_CMP_STRIDE + 1
    hb = PAGE // NSA_CMP_STRIDE
    half = NSA_CMP_STRIDE * D
    c2s = _cmp_to_slc(n_pages * hb, ncmp, n_slc)
    expand = jnp.asarray(np.arange(LANES)[:, None] == np.arange(past)[None, :] // NSA_SLC_LEN, BF16)
    r0 = row0 // Ts
    rows = lambda w: pl.BlockSpec((Ts, w), lambda b, tab: (r0 + b, 0))
    full = lambda a: pl.BlockSpec(a.shape, lambda b, tab: (0,) * a.ndim)
    page = lambda u: pl.BlockSpec((1,) + pool.shape[1:], lambda b, tab: (tab[b * n_pages + u], 0, 0))
    pe2 = pe.reshape(2, 2, half)
    in_specs = ([rows(QW), rows(LANES), rows(xnew.shape[1]), rows(xwin.shape[1]),
                 pl.BlockSpec((1,) + wbuf.shape[1:], lambda b, tab: (b, 0, 0)),
                 full(pe2), full(w1), full(w2), full(c2s), full(expand)]
                + [page(u) for u in range(n_pages)] + [pl.BlockSpec(memory_space=pl.ANY)])
    grid_spec = pltpu.PrefetchScalarGridSpec(
        num_scalar_prefetch=1,
        grid=(B,),
        in_specs=in_specs,
        out_specs=[rows(QW), pl.BlockSpec((1,) + wbuf.shape[1:], lambda b, tab: (b, 0, 0))],
        scratch_shapes=[pltpu.VMEM((2 * NSA_KV_HEADS, n_pages * hb, half), F32)])
    return pl.pallas_call(
        functools.partial(_nsa_sample_kernel, Ts=Ts, n_pages=n_pages, Lw=Lw, n_slc=n_slc, ncmp=ncmp, past=past),
        grid_spec=grid_spec,
        out_shape=[jax.ShapeDtypeStruct((M, QW), F32), jax.ShapeDtypeStruct(wbuf.shape, F32)],
        input_output_aliases={11 + n_pages: 0},
        compiler_params=_cp("parallel"),
        name="nsa_sample",
    )(table, q, gate, xnew, xwin, wbuf, pe2, w1, w2, c2s, expand, *([pool] * n_pages), o_prev)


def _cmp_to_slc(nrows, ncmp, n_slc):
    i = np.arange(nrows)[:, None] * NSA_CMP_STRIDE
    j = np.arange(LANES)[None, :] * NSA_SLC_LEN
    ov = np.clip(np.minimum(i + NSA_CMP_LEN, j + NSA_SLC_LEN) - np.maximum(i, j), 0, None) / NSA_CMP_LEN
    ov = np.where((np.arange(nrows)[:, None] < ncmp) & (np.arange(LANES)[None, :] < n_slc), ov, 0.0)
    return jnp.asarray(ov, BF16)


def nsa_attn(q, gate, kc, new, win, B, T):
    M = q.shape[0]
    D = HEAD_DIM
    QW = NSA_HEADS * D
    KW = 2 * NSA_KV_HEADS * D
    tq = min(PAGE, T)
    KB = min(NSA_WINDOW, T)
    nqb, nkb = T // tq, T // KB
    n_slc = -(-T // NSA_SLC_LEN)
    ncmp = (T - NSA_CMP_LEN) // NSA_CMP_STRIDE + 1
    R = NSA_GROUP * tq
    c2s = _cmp_to_slc(kc.shape[2], ncmp, n_slc)
    expand = jnp.asarray(np.arange(LANES)[:, None] == np.arange(T)[None, :] // NSA_SLC_LEN, BF16)
    pairs = [(qi, kb) for qi in range(nqb) for kb in range((qi * tq + tq - 1) // KB + 1)]
    qi_tab = jnp.asarray([p[0] for p in pairs], I32)
    kb_tab = jnp.asarray([p[1] for p in pairs], I32)
    qrow = lambda b, s, qt, kt: (b * nqb + qt[s], 0)
    in_specs = [pl.BlockSpec((tq, QW), qrow),
                pl.BlockSpec((tq, LANES), qrow),
                pl.BlockSpec((1,) + kc.shape[1:], lambda b, s, qt, kt: (b, 0, 0, 0)),
                pl.BlockSpec(c2s.shape, lambda b, s, qt, kt: (0, 0)),
                pl.BlockSpec((LANES, KB), lambda b, s, qt, kt: (0, kt[s])),
                pl.BlockSpec((KB, KW), lambda b, s, qt, kt: (b * nkb + kt[s], 1)),
                pl.BlockSpec((KB, KW), lambda b, s, qt, kt: (b * nkb + kt[s], 0))]
    vec = lambda: pltpu.VMEM((NSA_KV_HEADS, R, 1), F32)
    mat = lambda: pltpu.VMEM((NSA_KV_HEADS, R, D), F32)
    grid_spec = pltpu.PrefetchScalarGridSpec(
        num_scalar_prefetch=2,
        grid=(B, len(pairs)),
        in_specs=in_specs,
        out_specs=pl.BlockSpec((tq, QW), qrow),
        scratch_shapes=[mat(), pltpu.VMEM((NSA_KV_HEADS, tq, LANES), F32),
                        vec(), vec(), mat(), vec(), vec(), mat()])
    return pl.pallas_call(
        functools.partial(_nsa_attn_kernel, tq=tq, KB=KB, n_slc=n_slc, ncmp=ncmp),
        grid_spec=grid_spec,
        out_shape=jax.ShapeDtypeStruct((M, QW), F32),
        compiler_params=_cp("parallel", "arbitrary"),
        name="nsa_attn_prompt",
    )(qi_tab, kb_tab, q, gate, kc, c2s, expand, new, win)


def _router_kernel(x_ref, g_ref, wr_ref, h_ref, e_ref, p_ref):
    xn = _rms(x_ref[...], g_ref[...])
    h_ref[...] = xn
    xh, xm, xl = _split3(xn)
    wh, wm, wl = _split3(wr_ref[...])
    logits = (_dot(xh, wh) + _dot(xh, wm) + _dot(xm, wh)
              + _dot(xh, wl) + _dot(xl, wh) + _dot(xm, wm))
    lane = lax.broadcasted_iota(I32, logits.shape, 1)
    lg = jnp.where(lane < N_EXPERTS, logits, -jnp.inf)
    m1 = jnp.max(lg, axis=-1, keepdims=True)
    i1 = jnp.min(jnp.where(lg == m1, lane, LANES), axis=-1, keepdims=True)
    lg2 = jnp.where(lane == i1, -jnp.inf, lg)
    m2 = jnp.max(lg2, axis=-1, keepdims=True)
    i2 = jnp.min(jnp.where(lg2 == m2, lane, LANES), axis=-1, keepdims=True)
    e = jnp.exp(m2 - m1)
    e_ref[...] = jnp.where(lane == 0, i1, jnp.where(lane == 1, i2, 0))
    p_ref[...] = jnp.where(lane == 0, 1.0 / (1.0 + e), jnp.where(lane == 1, e / (1.0 + e), 0.0))


def moe_router(x, g, router):
    M, D = x.shape
    tm = _tile(M, 512)
    wr = jnp.zeros((D, LANES), F32).at[:, :N_EXPERTS].set(router)
    return pl.pallas_call(
        _router_kernel,
        grid=(M // tm,),
        in_specs=[pl.BlockSpec((tm, D), lambda i: (i, 0)),
                  pl.BlockSpec((1, D), lambda i: (0, 0)),
                  pl.BlockSpec((D, LANES), lambda i: (0, 0))],
        out_specs=[pl.BlockSpec((tm, D), lambda i: (i, 0)),
                   pl.BlockSpec((tm, LANES), lambda i: (i, 0)),
                   pl.BlockSpec((tm, LANES), lambda i: (i, 0))],
        out_shape=[jax.ShapeDtypeStruct((M, D), F32),
                   jax.ShapeDtypeStruct((M, LANES), I32),
                   jax.ShapeDtypeStruct((M, LANES), F32)],
        compiler_params=_cp("parallel"),
        name="moe_router",
    )(x, g.reshape(1, D), wr)


def _row_copy(src_ref, dst_ref, sem, src_row, dst_row):
    return pltpu.make_async_copy(src_ref.at[pl.ds(src_row, 1)], dst_ref.at[pl.ds(dst_row, 1)], sem)


def _moe_ffn_kernel(be_ref, rt_ref, nu_ref, h_ref, wg_ref, wu_ref, wd_ref, y_ref,
                    xbuf, xb16, acc, sems, *, tm, n_f):
    del be_ref
    r, f = pl.program_id(0), pl.program_id(1)
    n_used = nu_ref[0]
    used = r < n_used
    slot = r % 2

    def gather(block, slot_):
        def issue(i, carry):
            _row_copy(h_ref, xbuf.at[slot_], sems.at[slot_], rt_ref[block * tm + i], i).start()
            return carry

        lax.fori_loop(0, tm, issue, 0, unroll=8)

    @pl.when(used & (f == 0) & (r == 0))
    def _():
        gather(0, 0)

    @pl.when(used & (f == 0))
    def _():
        def drain(i, carry):
            _row_copy(h_ref, xbuf.at[slot], sems.at[slot], 0, i).wait()
            return carry

        lax.fori_loop(0, tm, drain, 0, unroll=8)
        xb16[...] = xbuf[slot].astype(BF16)
        acc[...] = jnp.zeros_like(acc)

    @pl.when((r + 1 < n_used) & (f == 0))
    def _():
        gather(r + 1, 1 - slot)

    @pl.when(used)
    def _():
        x = xb16[...]
        hg = _dot(x, wg_ref[0])
        hu = _dot(x, wu_ref[0])
        acc[...] += _dot((_silu(hg) * hu).astype(BF16), wd_ref[0])

    @pl.when(used & (f == n_f - 1))
    def _():
        y_ref[...] = acc[...]

    @pl.when(jnp.logical_not(used) & (f == n_f - 1))
    def _():
        y_ref[...] = jnp.zeros_like(y_ref)


def moe_ffn(h, block_expert, row_tok, n_used, w_up, w_down, layer):
    D = h.shape[1]
    F = w_down.shape[2]
    tm = MOE_ROWS
    nb = block_expert.shape[0]
    tf = 512
    n_f = F // tf
    live = lambda r, f, nu: jnp.where(r < nu[0], f, 0)
    grid_spec = pltpu.PrefetchScalarGridSpec(
        num_scalar_prefetch=3,
        grid=(nb, n_f),
        in_specs=[pl.BlockSpec(memory_space=pl.ANY),
                  pl.BlockSpec((None, 1, D, tf), lambda r, f, be, rt, nu: (layer, be[r], 0, live(r, f, nu))),
                  pl.BlockSpec((None, 1, D, tf), lambda r, f, be, rt, nu: (layer, be[r], 0, n_f + live(r, f, nu))),
                  pl.BlockSpec((None, 1, tf, D), lambda r, f, be, rt, nu: (layer, be[r], live(r, f, nu), 0))],
        out_specs=pl.BlockSpec((tm, D), lambda r, f, be, rt, nu: (r, 0)),
        scratch_shapes=[pltpu.VMEM((2, tm, D), F32), pltpu.VMEM((tm, D), BF16), pltpu.VMEM((tm, D), F32),
                        pltpu.SemaphoreType.DMA((2,))])
    return pl.pallas_call(
        functools.partial(_moe_ffn_kernel, tm=tm, n_f=n_f),
        grid_spec=grid_spec,
        out_shape=jax.ShapeDtypeStruct((nb * tm, D), F32),
        compiler_params=_cp("arbitrary", "arbitrary"),
        name="moe_ffn",
    )(block_expert, row_tok, n_used, h, w_up, w_up, w_down)


def _moe_combine_kernel(p0_ref, p1_ref, x_ref, g_ref, y_ref, o_ref, ybuf, sem, *, tm):
    i = pl.program_id(0)

    def issue(t, carry):
        _row_copy(y_ref, ybuf.at[0], sem, p0_ref[i * tm + t], t).start()
        _row_copy(y_ref, ybuf.at[1], sem, p1_ref[i * tm + t], t).start()
        return carry

    lax.fori_loop(0, tm, issue, 0, unroll=8)

    def drain(t, carry):
        _row_copy(y_ref, ybuf.at[0], sem, 0, t).wait()
        _row_copy(y_ref, ybuf.at[1], sem, 0, t).wait()
        return carry

    lax.fori_loop(0, tm, drain, 0, unroll=8)
    g = g_ref[...]
    o_ref[...] = x_ref[...] + (g[:, 0:1] * ybuf[0] + g[:, 1:2] * ybuf[1])


def moe_combine(x, gates, y, pos0, pos1):
    M, D = x.shape
    tm = _tile(M, 256)
    grid_spec = pltpu.PrefetchScalarGridSpec(
        num_scalar_prefetch=2,
        grid=(M // tm,),
        in_specs=[pl.BlockSpec((tm, D), lambda i, a, b: (i, 0)),
                  pl.BlockSpec((tm, LANES), lambda i, a, b: (i, 0)),
                  pl.BlockSpec(memory_space=pl.ANY)],
        out_specs=pl.BlockSpec((tm, D), lambda i, a, b: (i, 0)),
        scratch_shapes=[pltpu.VMEM((2, tm, D), F32), pltpu.SemaphoreType.DMA(())])
    return pl.pallas_call(
        functools.partial(_moe_combine_kernel, tm=tm),
        grid_spec=grid_spec,
        out_shape=jax.ShapeDtypeStruct((M, D), F32),
        compiler_params=_cp("arbitrary"),
        name="moe_combine",
    )(pos0, pos1, x, gates, y)


def moe_layer(x, g, router, w_up, w_down, layer):
    M = x.shape[0]
    h, eidx, gates = moe_router(x, g, router)
    A = 2 * M
    e = eidx[:, :2].reshape(A)
    onehot = (e[:, None] == jnp.arange(N_EXPERTS, dtype=I32)[None, :]).astype(I32)
    csum = jnp.cumsum(onehot, axis=0)
    rank = jnp.sum(csum * onehot, axis=1) - 1
    counts = csum[-1]
    padded = (counts + MOE_ROWS - 1) // MOE_ROWS * MOE_ROWS
    pend = jnp.cumsum(padded)
    dest = ((pend - padded)[e] + rank).astype(I32)
    nb = (A + N_EXPERTS * (MOE_ROWS - 1) + MOE_ROWS - 1) // MOE_ROWS
    row_tok = jnp.zeros((nb * MOE_ROWS,), I32).at[dest].set(jnp.arange(A, dtype=I32) // 2)
    block_expert = jnp.minimum(
        jnp.searchsorted(pend, jnp.arange(nb, dtype=I32) * MOE_ROWS, side='right'), N_EXPERTS - 1).astype(I32)
    n_used = (pend[-1:] // MOE_ROWS).astype(I32)
    y = moe_ffn(h, block_expert, row_tok, n_used, w_up, w_down, layer)
    dest2 = dest.reshape(M, 2)
    return moe_combine(x, gates, y, dest2[:, 0], dest2[:, 1])


def _rope_tables(pos, half):
    inv = ROPE_THETA ** (-jnp.arange(half, dtype=F32) / half)
    ang = pos.astype(F32)[:, None] * inv[None, :]
    return jnp.cos(ang), jnp.sin(ang)


def _row_rope_tables(Bp, T, Bs, Ts, past):
    pos = jnp.concatenate([jnp.tile(jnp.arange(T, dtype=I32), Bp),
                           jnp.tile(past + jnp.arange(Ts, dtype=I32), Bs)])
    cos, sin = _rope_tables(pos, HEAD_DIM // 2)
    return jnp.concatenate([cos, cos], axis=1), jnp.concatenate([-sin, sin], axis=1)


def kernel(x_prompt, x_sample, state_ret, state_hgrn, cache_dil_w128, cache_dil_w512, cache_dil_w2048,
           cache_nsa_kv, cache_nsa_win, page_table, norm_mix, norm_ffn,
           ret_w_in, ret_gn_g, ret_gn_b, ret_w_out,
           hg_w_in, hg_lower_bounds, hg_gn_g, hg_w_out,
           dil_w_in, dil_qn_g, dil_kn_g, dil_w_out,
           nsa_w_in, nsa_w_gate, nsa_qn_g, nsa_kn_g, nsa_cmp_pe, nsa_cmp_w1, nsa_cmp_w2, nsa_w_out,
           ffn_w_up, ffn_w_down, moe_router_w, moe_w_up, moe_w_down):
    Bp, T, D = x_prompt.shape
    Bs, Ts, _ = x_sample.shape
    Np, Ns = Bp * T, Bs * Ts
    M = Np + Ns
    n_pages = page_table.shape[1]
    past = n_pages * PAGE
    bf = lambda w: w.astype(BF16)

    x = jnp.concatenate([x_prompt.reshape(Np, D), x_sample.reshape(Ns, D)], axis=0)
    lb_all = jnp.cumsum(jax.nn.softmax(hg_lower_bounds.astype(F32), axis=0), axis=0)
    lb_all = lb_all - lb_all[0:1]

    proj = norm_proj(x, norm_mix[0], bf(ret_w_in[0]), 1024)
    cos_r, sin_r = _rope_tables(jnp.arange(max(T, past + Ts), dtype=I32), RET_DK // 2)
    y, ret_p = ret_core(proj, Bp, T, 0, cos_r, sin_r, 0, ret_gn_g[0], ret_gn_b[0], None, None)
    y, ret_s = ret_core(proj, Bs, Ts, Np, cos_r, sin_r, past, ret_gn_g[0], ret_gn_b[0], state_ret[0], y)
    x = out_proj(y, bf(ret_w_out[0]), x)
    x = ffn_dense(x, norm_ffn[0], bf(ffn_w_up[0]), bf(ffn_w_down[0]))

    proj = norm_proj(x, norm_mix[1], bf(hg_w_in[0]), 1024)
    y, hg_p = hgrn_core(proj, Bp, T, 0, lb_all[1], hg_gn_g[0], None, None)
    y, hg_s = hgrn_core(proj, Bs, Ts, Np, lb_all[1], hg_gn_g[0], state_hgrn[0], y)
    x = out_proj(y, bf(hg_w_out[0]), x)
    moe_up, moe_down = bf(moe_w_up), bf(moe_w_down)
    x = moe_layer(x, norm_ffn[1], moe_router_w[0], moe_up, moe_down, 0)

    cos_h, sin_h = _row_rope_tables(Bp, T, Bs, Ts, past)
    HW = DIL_HEADS * HEAD_DIM
    dil_in_w = dil_w_in.shape[2]
    proj = norm_proj(x, norm_mix[2], bf(dil_w_in[0]), 3 * HW)
    ones = jnp.ones((DIL_HEADS, HEAD_DIM), F32)
    gains = jnp.concatenate([jnp.concatenate([ones * dil_qn_g[0, gi], ones * dil_kn_g[0, gi], ones], axis=0)
                             for gi in range(len(DIL_GROUPS))], axis=0)
    flags = ([True] * (2 * DIL_HEADS) + [False] * DIL_HEADS) * len(DIL_GROUPS)
    dilated = [(gi, d) for gi, (_, d) in enumerate(DIL_GROUPS) if d > 1]
    pp, *by_class = prep_heads(proj, cos_h, sin_h, gains, flags, [(0, dil_in_w)],
                               [(gi * 3 * HW, (gi + 1) * 3 * HW, d) for gi, d in dilated])
    by_class = dict(zip([gi for gi, _ in dilated], by_class))
    caches = (cache_dil_w128, cache_dil_w512, cache_dil_w2048)
    op, lp, od, ld, dil_p, dil_s = [], [], [], [], [], []
    for gi, (window, _) in enumerate(DIL_GROUPS):
        if gi in by_class:
            o_g, l_g = dil_prompt(by_class[gi], gi, 0, Bp, T)
        else:
            o_g, l_g = dil_prompt(pp.reshape(1, M, dil_in_w), gi, gi * 3, Bp, T)
        op.append(o_g)
        lp.append(l_g)
        buf = caches[gi][0]
        o_g, l_g, nbuf = dil_decode(pp, gi, Bs, Ts, Np, buf.reshape(Bs, -1, HEAD_DIM))
        od.append(o_g)
        ld.append(l_g)
        keep = min(window, T)
        kv = pp[:Np, gi * 3 * HW + HW:(gi + 1) * 3 * HW].reshape(Bp, T, 2, DIL_HEADS, HEAD_DIM)
        dil_p.append(kv[:, T - keep:][None])
        dil_s.append(nbuf.reshape(buf.shape)[None])
    merged = jnp.concatenate([dil_merge(op, lp, Np), dil_merge(od, ld, Ns)], axis=0)
    x = out_proj(merged, bf(dil_w_out[0]), x)
    x = ffn_dense(x, norm_ffn[2], bf(ffn_w_up[1]), bf(ffn_w_down[1]))

    QW = NSA_HEADS * HEAD_DIM
    KW = NSA_KV_HEADS * HEAD_DIM
    w_in = jnp.concatenate([nsa_w_in[0], nsa_w_gate[0],
                            jnp.zeros((D, LANES - nsa_w_gate.shape[2]), F32)], axis=1)
    proj = norm_proj(x, norm_mix[3], bf(w_in), w_in.shape[1] // 3)
    ones_q = jnp.ones((NSA_HEADS, HEAD_DIM), F32)
    ones_k = jnp.ones((NSA_KV_HEADS, HEAD_DIM), F32)
    gains = jnp.concatenate([ones_q * nsa_qn_g[0]]
                            + [blk for s in range(3) for blk in (ones_k * nsa_kn_g[0, s], ones_k)]
                            + [jnp.ones((1, HEAD_DIM), F32)], axis=0)
    flags = [True] * NSA_HEADS + ([True] * NSA_KV_HEADS + [False] * NSA_KV_HEADS) * 3 + [False]
    q, new, win, gate = prep_heads(proj, cos_h, sin_h, gains, flags,
                                   [(0, QW), (QW, QW + 4 * KW), (QW + 4 * KW, QW + 6 * KW),
                                    (QW + 6 * KW, QW + 6 * KW + LANES)])
    w1, w2 = bf(nsa_cmp_w1[0]), bf(nsa_cmp_w2[0])
    ppages = new.reshape(M // PAGE, PAGE, 4 * KW)
    ptab = jnp.arange(Bp * (T // PAGE), dtype=I32)
    kc_p = nsa_compress(ppages, ptab, Bp, T // PAGE, nsa_cmp_pe[0], w1, w2)
    o = nsa_attn(q, gate, kc_p, new, win, Bp, T)
    pool = cache_nsa_kv[0].reshape(cache_nsa_kv.shape[1], -1, HEAD_DIM)
    wbuf = cache_nsa_win[0]
    o, win_s = nsa_sample(q, gate, new, win, pool, page_table.reshape(-1).astype(I32),
                          wbuf.reshape(Bs, -1, HEAD_DIM), nsa_cmp_pe[0], w1, w2, Bs, Ts, Np, past, o)
    x = out_proj(o, bf(nsa_w_out[0]), x)
    x = moe_layer(x, norm_ffn[3], moe_router_w[1], moe_up, moe_down, 1)

    keep = min(NSA_WINDOW, T)
    win_p = win[:Np].reshape(Bp, T, 2, NSA_KV_HEADS, HEAD_DIM)[:, T - keep:]
    win_s = win_s.reshape(wbuf.shape)
    return (x[:Np].reshape(Bp, T, D), x[Np:].reshape(Bs, Ts, D),
            ret_p[None], ret_s[None], hg_p[None], hg_s[None],
            dil_p[0], dil_s[0], dil_p[1], dil_s[1], dil_p[2], dil_s[2],
            new[:Np].reshape(Bp, T, 4, NSA_KV_HEADS, HEAD_DIM)[None],
            new[Np:].reshape(Bs, Ts, 4, NSA_KV_HEADS, HEAD_DIM)[None],
            win_p[None], win_s[None])
```

```python
import functools
import math

import numpy as np
import jax
import jax.numpy as jnp
from jax import lax
from jax.experimental import pallas as pl
from jax.experimental.pallas import tpu as pltpu

F32 = jnp.float32
BF16 = jnp.bfloat16
I32 = jnp.int32

EPS = 1e-6
NEG = -1e30
ROPE_THETA = 10000.0
LANES = 128
VMEM_LIMIT = 56 * 1024 * 1024

HEAD_DIM = 128
RET_HEADS, RET_DK, RET_DV, RET_CHUNK = 4, 256, 512, 128
HG_HEADS, HG_DK, HG_CHUNK = 8, 128, 64
DIL_HEADS = 4
DIL_GROUPS = ((128, 1), (512, 4), (2048, 16))
DIL_QBLOCK = 128
NSA_HEADS, NSA_KV_HEADS, NSA_GROUP = 8, 2, 4
NSA_CMP_LEN, NSA_CMP_STRIDE = 32, 16
NSA_SLC_LEN, NSA_TOP_N, NSA_WINDOW = 64, 16, 512
PAGE = 128
N_EXPERTS = 8
MOE_ROWS = 512


def _cp(*sem):
    return pltpu.CompilerParams(dimension_semantics=sem, vmem_limit_bytes=VMEM_LIMIT)


def _tile(n, pref):
    t = pref
    while n % t:
        t //= 2
    return t


def _dot(a, b):
    return jnp.dot(a, b, preferred_element_type=F32)


def _dot_nt(a, b):
    return lax.dot_general(a, b, (((1,), (1,)), ((), ())), preferred_element_type=F32)


def _dot_tn(a, b):
    return lax.dot_general(a, b, (((0,), (0,)), ((), ())), preferred_element_type=F32)


def _split3(x):
    hi = x.astype(BF16)
    r = x - hi.astype(F32)
    mid = r.astype(BF16)
    lo = (r - mid.astype(F32)).astype(BF16)
    return hi, mid, lo


def _rms(x, g):
    return x * lax.rsqrt(jnp.mean(x * x, axis=-1, keepdims=True) + EPS) * g


def _sigmoid(x):
    return 1.0 / (1.0 + jnp.exp(-x))


def _silu(x):
    return x * _sigmoid(x)


def _norm_proj_kernel(x_ref, g_ref, w_ref, o_ref, xn_ref):
    @pl.when(pl.program_id(1) == 0)
    def _():
        xn_ref[...] = _rms(x_ref[...], g_ref[...]).astype(BF16)

    o_ref[...] = _dot(xn_ref[...], w_ref[...]).astype(o_ref.dtype)


def norm_proj(x, g, w, tn):
    M, D = x.shape
    N = w.shape[1]
    tm = _tile(M, 1024)
    return pl.pallas_call(
        _norm_proj_kernel,
        grid=(M // tm, N // tn),
        in_specs=[pl.BlockSpec((tm, D), lambda i, j: (i, 0)),
                  pl.BlockSpec((1, D), lambda i, j: (0, 0)),
                  pl.BlockSpec((D, tn), lambda i, j: (0, j))],
        out_specs=pl.BlockSpec((tm, tn), lambda i, j: (i, j)),
        out_shape=jax.ShapeDtypeStruct((M, N), F32),
        scratch_shapes=[pltpu.VMEM((tm, D), BF16)],
        compiler_params=_cp("parallel", "arbitrary"),
        name="norm_proj",
    )(x, g.reshape(1, D), w)


def _out_proj_kernel(a_ref, w_ref, x_ref, o_ref):
    o_ref[...] = x_ref[...] + _dot(a_ref[...].astype(BF16), w_ref[...])


def out_proj(a, w, x):
    M, K = a.shape
    D = w.shape[1]
    tm = _tile(M, 512)
    return pl.pallas_call(
        _out_proj_kernel,
        grid=(M // tm,),
        in_specs=[pl.BlockSpec((tm, K), lambda i: (i, 0)),
                  pl.BlockSpec((K, D), lambda i: (0, 0)),
                  pl.BlockSpec((tm, D), lambda i: (i, 0))],
        out_specs=pl.BlockSpec((tm, D), lambda i: (i, 0)),
        out_shape=jax.ShapeDtypeStruct((M, D), F32),
        compiler_params=_cp("parallel"),
        name="out_proj",
    )(a, w, x)


def _ffn_dense_kernel(x_ref, g_ref, wg_ref, wu_ref, wd_ref, o_ref, xn_ref, acc_ref, *, n_f):
    f = pl.program_id(1)

    @pl.when(f == 0)
    def _():
        xn_ref[...] = _rms(x_ref[...], g_ref[...]).astype(BF16)
        acc_ref[...] = jnp.zeros_like(acc_ref)

    xn = xn_ref[...]
    hg = _dot(xn, wg_ref[...])
    hu = _dot(xn, wu_ref[...])
    acc_ref[...] += _dot((_silu(hg) * hu).astype(BF16), wd_ref[...])

    @pl.when(f == n_f - 1)
    def _():
        o_ref[...] = x_ref[...] + acc_ref[...]


def ffn_dense(x, g, w_up, w_down):
    M, D = x.shape
    F = w_down.shape[0]
    tm = _tile(M, 512)
    tf = F // 2 if (F // 2) % LANES == 0 else F
    n_f = F // tf
    return pl.pallas_call(
        functools.partial(_ffn_dense_kernel, n_f=n_f),
        grid=(M // tm, n_f),
        in_specs=[pl.BlockSpec((tm, D), lambda i, f: (i, 0)),
                  pl.BlockSpec((1, D), lambda i, f: (0, 0)),
                  pl.BlockSpec((D, tf), lambda i, f: (0, f)),
                  pl.BlockSpec((D, tf), lambda i, f: (0, n_f + f)),
                  pl.BlockSpec((tf, D), lambda i, f: (f, 0))],
        out_specs=pl.BlockSpec((tm, D), lambda i, f: (i, 0)),
        out_shape=jax.ShapeDtypeStruct((M, D), F32),
        scratch_shapes=[pltpu.VMEM((tm, D), BF16), pltpu.VMEM((tm, D), F32)],
        compiler_params=_cp("parallel", "arbitrary"),
        name="ffn_dense",
    )(x, g.reshape(1, D), w_up, w_up, w_down)


def _ret_kernel(*refs, C, has_s0, has_prev):
    q_ref, k_ref, v_ref, g_ref, cos_ref, sin_ref, gng_ref, gnb_ref = refs[:8]
    n = 8
    s0_ref = None
    if has_s0:
        s0_ref = refs[n]
        n += 1
    if has_prev:
        n += 1
    y_ref, s_ref = refs[n], refs[n + 1]
    c = pl.program_id(1)

    @pl.when(c == 0)
    def _():
        if has_s0:
            s_ref[...] = s0_ref[...]
        else:
            s_ref[...] = jnp.zeros_like(s_ref)

    cos, sin = cos_ref[...], sin_ref[...]
    half = RET_DK // 2
    t = lax.broadcasted_iota(I32, (C, 1), 0).astype(F32)
    diff = (lax.broadcasted_iota(I32, (C, C), 0) - lax.broadcasted_iota(I32, (C, C), 1)).astype(F32)

    def rot(x):
        x1, x2 = x[:, :half], x[:, half:]
        return jnp.concatenate([x1 * cos - x2 * sin, x2 * cos + x1 * sin], axis=1)

    for h in range(RET_HEADS):
        lg = math.log(1.0 - 2.0 ** (-5.0 - h))
        qr = rot(q_ref[:, h * RET_DK:(h + 1) * RET_DK])
        kr = rot(k_ref[:, h * RET_DK:(h + 1) * RET_DK]) * RET_DK ** -0.5
        v = v_ref[:, h * RET_DV:(h + 1) * RET_DV].astype(BF16)
        decay = jnp.where(diff >= 0, jnp.exp(lg * jnp.maximum(diff, 0.0)), 0.0)
        sc = _dot_nt(qr.astype(BF16), kr.astype(BF16)) * decay
        S = s_ref[0, h]
        o = _dot(sc.astype(BF16), v) + _dot((qr * jnp.exp(lg * (t + 1.0))).astype(BF16), S.astype(BF16))
        kd = (kr * jnp.exp(lg * (C - 1.0 - t))).astype(BF16)
        s_ref[0, h] = math.exp(lg * C) * S + _dot_tn(kd, v)
        mu = jnp.mean(o, axis=-1, keepdims=True)
        var = jnp.mean(jnp.square(o - mu), axis=-1, keepdims=True)
        sl = slice(h * RET_DV, (h + 1) * RET_DV)
        on = (o - mu) * lax.rsqrt(var + EPS) * gng_ref[:, sl] + gnb_ref[:, sl]
        y_ref[:, sl] = _silu(g_ref[:, sl]) * on


def ret_core(proj, B, T, row0, cos, sin, tab0, gng, gnb, s0, y_prev):
    M = proj.shape[0]
    C = RET_CHUNK if T % RET_CHUNK == 0 else T
    nC = T // C
    r0, t0 = row0 // C, tab0 // C
    qkw, vw = RET_HEADS * RET_DK, RET_HEADS * RET_DV
    row = lambda b, c: r0 + b * nC + c
    in_specs = [pl.BlockSpec((C, qkw), lambda b, c: (row(b, c), 0)),
                pl.BlockSpec((C, qkw), lambda b, c: (row(b, c), 1)),
                pl.BlockSpec((C, vw), lambda b, c: (row(b, c), 2 * qkw // vw)),
                pl.BlockSpec((C, vw), lambda b, c: (row(b, c), 2 * qkw // vw + 1)),
                pl.BlockSpec((C, RET_DK // 2), lambda b, c: (t0 + c, 0)),
                pl.BlockSpec((C, RET_DK // 2), lambda b, c: (t0 + c, 0)),
                pl.BlockSpec((1, vw), lambda b, c: (0, 0)),
                pl.BlockSpec((1, vw), lambda b, c: (0, 0))]
    args = [proj, proj, proj, proj, cos, sin, gng.reshape(1, vw), gnb.reshape(1, vw)]
    sblock = (1, RET_HEADS, RET_DK, RET_DV)
    if s0 is not None:
        in_specs.append(pl.BlockSpec(sblock, lambda b, c: (b, 0, 0, 0)))
        args.append(s0)
    aliases = {}
    if y_prev is not None:
        in_specs.append(pl.BlockSpec(memory_space=pl.ANY))
        aliases = {len(args): 0}
        args.append(y_prev)
    return pl.pallas_call(
        functools.partial(_ret_kernel, C=C, has_s0=s0 is not None, has_prev=y_prev is not None),
        grid=(B, nC),
        in_specs=in_specs,
        out_specs=[pl.BlockSpec((C, vw), lambda b, c: (row(b, c), 0)),
                   pl.BlockSpec(sblock, lambda b, c: (b, 0, 0, 0))],
        out_shape=[jax.ShapeDtypeStruct((M, vw), F32),
                   jax.ShapeDtypeStruct((B,) + sblock[1:], F32)],
        input_output_aliases=aliases,
        compiler_params=_cp("parallel", "arbitrary"),
        name="ret_core",
    )(*args)


def _hgrn_levels(C):
    out, m = [], C // 2
    while m >= 1:
        out.append(m)
        m //= 2
    return out


def _hgrn_mats(C):
    t = np.arange(C)[:, None]
    u = np.arange(C)[None, :]
    mats = [(u <= t).astype(np.float32)]
    for m in _hgrn_levels(C):
        r = 2 * m * (t // (2 * m)) + m - 1
        a = ((u > r) & (u <= t)).astype(np.float32) - ((u > t) & (u <= r)).astype(np.float32)
        mats.append(a)
    return np.stack(mats)


def _hgrn_kernel(*refs, C, has_s0, has_prev):
    q_ref, f_ref, i_ref, g_ref, lb_ref, gng_ref, a_ref = refs[:7]
    n = 7
    s0_ref = None
    if has_s0:
        s0_ref = refs[n]
        n += 1
    if has_prev:
        n += 1
    y_ref, s_ref = refs[n], refs[n + 1]
    c = pl.program_id(1)

    @pl.when(c == 0)
    def _():
        if has_s0:
            s_ref[...] = s0_ref[...]
        else:
            s_ref[...] = jnp.zeros_like(s_ref)

    lb = lb_ref[...]
    f = lb + (1.0 - lb) * _sigmoid(f_ref[...])
    logf = jnp.log(f)
    kk = 1.0 - f
    q = _silu(q_ref[...])
    v = i_ref[...].astype(BF16)
    parts = _split3(logf)

    def amul(idx):
        a = a_ref[idx]
        return _dot(a, parts[0]) + _dot(a, parts[1]) + _dot(a, parts[2])

    bcum = amul(0)
    blast = bcum[C - 1:C, :]
    qb = (q * jnp.exp(bcum)).astype(BF16)
    kst = (kk * jnp.exp(blast - bcum)).astype(BF16)
    eb = jnp.exp(blast)

    ti = lax.broadcasted_iota(I32, (C, C), 0)
    si = lax.broadcasted_iota(I32, (C, C), 1)
    terms = [(q.astype(BF16), kk.astype(BF16), ti == si)]
    for li, m in enumerate(_hgrn_levels(C)):
        dq = amul(1 + li)
        qt = (q * jnp.exp(jnp.minimum(dq, 0.0))).astype(BF16)
        kt = (kk * jnp.exp(jnp.minimum(-dq, 0.0))).astype(BF16)
        blk = 2 * m
        mask = ((ti // blk) == (si // blk)) & ((ti % blk) >= m) & ((si % blk) < m)
        terms.append((qt, kt, mask))

    eye = (lax.broadcasted_iota(I32, (HG_DK, HG_DK), 0) == lax.broadcasted_iota(I32, (HG_DK, HG_DK), 1))
    for h in range(HG_HEADS):
        sl = slice(h * HG_DK, (h + 1) * HG_DK)
        sc = jnp.zeros((C, C), F32)
        for qt, kt, mask in terms:
            sc = sc + jnp.where(mask, _dot_nt(qt[:, sl], kt[:, sl]), 0.0)
        S = s_ref[0, h]
        vh = v[:, sl]
        o = _dot(sc.astype(BF16), vh) + _dot(qb[:, sl], S.astype(BF16))
        ecol = jnp.sum(jnp.where(eye, eb[:, sl], 0.0), axis=1, keepdims=True)
        s_ref[0, h] = ecol * S + _dot_tn(kst[:, sl], vh)
        on = o * lax.rsqrt(jnp.mean(o * o, axis=-1, keepdims=True) + EPS)
        y_ref[:, sl] = on * gng_ref[:, sl] * _sigmoid(g_ref[:, sl])


def hgrn_core(proj, B, T, row0, lb, gng, s0, y_prev):
    M = proj.shape[0]
    W = HG_HEADS * HG_DK
    C = HG_CHUNK if T % HG_CHUNK == 0 else T
    nC = T // C
    r0 = row0 // C
    row = lambda b, c: r0 + b * nC + c
    amats = jnp.asarray(_hgrn_mats(C), BF16)
    col = lambda j: pl.BlockSpec((C, W), lambda b, c: (row(b, c), j))
    in_specs = [col(j) for j in range(4)]
    in_specs += [pl.BlockSpec((1, W), lambda b, c: (0, 0)),
                 pl.BlockSpec((1, W), lambda b, c: (0, 0)),
                 pl.BlockSpec(amats.shape, lambda b, c: (0, 0, 0))]
    args = [proj, proj, proj, proj, lb.reshape(1, W), gng.reshape(1, W), amats]
    sblock = (1, HG_HEADS, HG_DK, HG_DK)
    if s0 is not None:
        in_specs.append(pl.BlockSpec(sblock, lambda b, c: (b, 0, 0, 0)))
        args.append(s0)
    aliases = {}
    if y_prev is not None:
        in_specs.append(pl.BlockSpec(memory_space=pl.ANY))
        aliases = {len(args): 0}
        args.append(y_prev)
    return pl.pallas_call(
        functools.partial(_hgrn_kernel, C=C, has_s0=s0 is not None, has_prev=y_prev is not None),
        grid=(B, nC),
        in_specs=in_specs,
        out_specs=[pl.BlockSpec((C, W), lambda b, c: (row(b, c), 0)),
                   pl.BlockSpec(sblock, lambda b, c: (b, 0, 0, 0))],
        out_shape=[jax.ShapeDtypeStruct((M, W), F32),
                   jax.ShapeDtypeStruct((B,) + sblock[1:], F32)],
        input_output_aliases=aliases,
        compiler_params=_cp("parallel", "arbitrary"),
        name="hgrn_core",
    )(*args)


def _prep_kernel(x_ref, cos_ref, sin_ref, gain_ref, *o_refs, flags, outs, classes):
    cos, sin = cos_ref[...], sin_ref[...]
    tm = x_ref.shape[0]
    c_refs = o_refs[len(outs):len(outs) + len(classes)]
    stage = o_refs[-1] if classes else None
    for (lo, hi), o_ref in zip(outs, o_refs):
        for j in range(lo // LANES, hi // LANES):
            xj = x_ref[:, j * LANES:(j + 1) * LANES]
            if flags[j]:
                y = _rms(xj, gain_ref[j:j + 1, :])
                xj = y * cos + pltpu.roll(y, LANES // 2, 1) * sin
            o_ref[:, j * LANES - lo:(j + 1) * LANES - lo] = xj.astype(o_ref.dtype)
            if any(clo <= j * LANES < chi for clo, chi, _ in classes):
                stage[j] = xj
    for (lo, hi, d), c_ref in zip(classes, c_refs):
        for r in range(d):
            for j in range(lo // LANES, hi // LANES):
                c_ref[r, :, j * LANES - lo:(j + 1) * LANES - lo] = stage[j, pl.ds(r, tm // d, stride=d), :]


def prep_heads(x, cos, sin, gains, flags, outs, classes=()):
    M, W = x.shape
    tm = _tile(M, 256)
    nb = W // LANES
    return pl.pallas_call(
        functools.partial(_prep_kernel, flags=tuple(flags), outs=tuple(outs), classes=tuple(classes)),
        grid=(M // tm,),
        in_specs=[pl.BlockSpec((tm, W), lambda i: (i, 0)),
                  pl.BlockSpec((tm, LANES), lambda i: (i, 0)),
                  pl.BlockSpec((tm, LANES), lambda i: (i, 0)),
                  pl.BlockSpec((nb, LANES), lambda i: (0, 0))],
        out_specs=([pl.BlockSpec((tm, hi - lo), lambda i: (i, 0)) for lo, hi in outs]
                   + [pl.BlockSpec((d, tm // d, hi - lo), lambda i: (0, i, 0)) for lo, hi, d in classes]),
        out_shape=([jax.ShapeDtypeStruct((M, hi - lo), F32) for lo, hi in outs]
                   + [jax.ShapeDtypeStruct((d, M // d, hi - lo), F32) for lo, hi, d in classes]),
        scratch_shapes=[pltpu.VMEM((nb, tm, LANES), F32)] if classes else [],
        compiler_params=_cp("parallel"),
        name="prep_heads",
    )(x, cos, sin, gains)


def _softmax_head(s, valid, v):
    s = jnp.where(valid, s, NEG)
    m = jnp.max(s, axis=-1, keepdims=True)
    p = jnp.exp(s - m)
    l = jnp.sum(p, axis=-1, keepdims=True)
    o = _dot(p.astype(BF16), v) / l
    return o, m + jnp.log(l)


def _dil_prompt_kernel(q_ref, kc_ref, kp_ref, vc_ref, vp_ref, o_ref, l_ref, *, tq, reach):
    qi = pl.program_id(2)
    scale = HEAD_DIM ** -0.5
    qpos = qi * tq + lax.broadcasted_iota(I32, (tq, 2 * tq), 0)
    kpos = (qi - 1) * tq + lax.broadcasted_iota(I32, (tq, 2 * tq), 1)
    delta = qpos - kpos
    valid = (delta >= 0) & (delta <= reach) & (kpos >= 0)
    lane = lax.broadcasted_iota(I32, (tq, LANES), 1)
    lse_all = jnp.zeros((tq, LANES), F32)
    for h in range(DIL_HEADS):
        sl = slice(h * HEAD_DIM, (h + 1) * HEAD_DIM)
        q = q_ref[:, sl].astype(BF16)
        k = jnp.concatenate([kp_ref[:, sl], kc_ref[:, sl]], axis=0).astype(BF16)
        v = jnp.concatenate([vp_ref[:, sl], vc_ref[:, sl]], axis=0).astype(BF16)
        o, lse = _softmax_head(_dot_nt(q, k) * scale, valid, v)
        o_ref[h] = o
        lse_all = jnp.where(lane == h, lse, lse_all)
    l_ref[...] = lse_all


def dil_prompt(qkv, gi, col0, B, T):
    window, d = DIL_GROUPS[gi]
    HW = DIL_HEADS * HEAD_DIM
    Tc = T // d
    tq = min(DIL_QBLOCK, Tc)
    nq = Tc // tq

    def blk(off, prev):
        return pl.BlockSpec(
            (None, tq, HW),
            lambda b, r, qi: (r, b * nq + (jnp.maximum(qi - 1, 0) if prev else qi), col0 + off))

    return pl.pallas_call(
        functools.partial(_dil_prompt_kernel, tq=tq, reach=window // d),
        grid=(B, d, nq),
        in_specs=[blk(0, False), blk(1, False), blk(1, True), blk(2, False), blk(2, True)],
        out_specs=[pl.BlockSpec((None, DIL_HEADS, tq, HEAD_DIM), lambda b, r, qi: (r, 0, b * nq + qi, 0)),
                   pl.BlockSpec((None, tq, LANES), lambda b, r, qi: (r, b * nq + qi, 0))],
        out_shape=[jax.ShapeDtypeStruct((d, DIL_HEADS, B * Tc, HEAD_DIM), F32),
                   jax.ShapeDtypeStruct((d, B * Tc, LANES), F32)],
        compiler_params=_cp("parallel", "parallel", "arbitrary"),
        name=f"dil_prompt_{gi}",
    )(qkv, qkv, qkv, qkv, qkv)


def _dil_decode_kernel(q_ref, kn_ref, vn_ref, buf_ref, o_ref, l_ref, nb_ref, *, L, Ts, d, window):
    H = DIL_HEADS
    rpp = 2 * H
    scale = HEAD_DIM ** -0.5
    t = lax.broadcasted_iota(I32, (Ts, L + Ts), 0)
    i = lax.broadcasted_iota(I32, (Ts, L + Ts), 1)
    delta = L + t - i
    valid = (delta >= 0) & (delta % d == 0) & (delta <= window)
    lane = lax.broadcasted_iota(I32, (Ts, LANES), 1)
    lse_all = jnp.zeros((Ts, LANES), F32)
    for h in range(H):
        sl = slice(h * HEAD_DIM, (h + 1) * HEAD_DIM)
        q = q_ref[:, sl].astype(BF16)
        k = jnp.concatenate([buf_ref[0, pl.ds(h, L, stride=rpp), :], kn_ref[:, sl]], axis=0).astype(BF16)
        v = jnp.concatenate([buf_ref[0, pl.ds(H + h, L, stride=rpp), :], vn_ref[:, sl]], axis=0).astype(BF16)
        o, lse = _softmax_head(_dot_nt(q, k) * scale, valid, v)
        o_ref[h] = o
        lse_all = jnp.where(lane == h, lse, lse_all)
        nb_ref[0, pl.ds((L - Ts) * rpp + h, Ts, stride=rpp), :] = kn_ref[:, sl]
        nb_ref[0, pl.ds((L - Ts) * rpp + H + h, Ts, stride=rpp), :] = vn_ref[:, sl]
    l_ref[...] = lse_all
    nb_ref[0, 0:(L - Ts) * rpp, :] = buf_ref[0, Ts * rpp:L * rpp, :]


def dil_decode(pp, gi, B, Ts, row0, buf):
    window, d = DIL_GROUPS[gi]
    HW = DIL_HEADS * HEAD_DIM
    rows = buf.shape[1]
    L = rows // (2 * DIL_HEADS)
    r0 = row0 // Ts
    col = lambda off: pl.BlockSpec((Ts, HW), lambda b: (r0 + b, gi * 3 + off))
    return pl.pallas_call(
        functools.partial(_dil_decode_kernel, L=L, Ts=Ts, d=d, window=window),
        grid=(B,),
        in_specs=[col(0), col(1), col(2),
                  pl.BlockSpec((1, rows, HEAD_DIM), lambda b: (b, 0, 0))],
        out_specs=[pl.BlockSpec((None, DIL_HEADS, Ts, HEAD_DIM), lambda b: (0, 0, b, 0)),
                   pl.BlockSpec((None, Ts, LANES), lambda b: (0, b, 0)),
                   pl.BlockSpec((1, rows, HEAD_DIM), lambda b: (b, 0, 0))],
        out_shape=[jax.ShapeDtypeStruct((1, DIL_HEADS, B * Ts, HEAD_DIM), F32),
                   jax.ShapeDtypeStruct((1, B * Ts, LANES), F32),
                   jax.ShapeDtypeStruct((B, rows, HEAD_DIM), F32)],
        compiler_params=_cp("parallel"),
        name=f"dil_decode_{gi}",
    )(pp, pp, pp, buf)


def _dil_merge_kernel(*refs, ds):
    ng = len(ds)
    o_refs, l_refs, out_ref, stage = refs[:ng], refs[ng:2 * ng], refs[2 * ng], refs[2 * ng + 1]
    dmax = max(ds)
    n = out_ref.shape[0] // dmax
    for r in range(dmax):
        def rows(d):
            return r % d, pl.ds(r // d, n, stride=dmax // d)

        ls = [l_refs[g][rows(ds[g]) + (slice(None),)] for g in range(ng)]
        m = functools.reduce(jnp.maximum, ls)
        es = [jnp.exp(l - m) for l in ls]
        den = functools.reduce(lambda a, b: a + b, es)
        ws = [e / den for e in es]
        for h in range(DIL_HEADS):
            acc = None
            for g in range(ng):
                c, rs = rows(ds[g])
                term = ws[g][:, h:h + 1] * o_refs[g][c, h, rs, :]
                acc = term if acc is None else acc + term
            stage[h, pl.ds(r, n, stride=dmax), :] = acc
    for h in range(DIL_HEADS):
        out_ref[:, h * HEAD_DIM:(h + 1) * HEAD_DIM] = stage[h]


def dil_merge(os_, ls, rows):
    ds = tuple(o.shape[0] for o in os_)
    H, D = os_[0].shape[1], os_[0].shape[3]
    tm = _tile(rows, 256)
    ospec = lambda a: pl.BlockSpec((a.shape[0], H, tm // a.shape[0], D), lambda i: (0, 0, i, 0))
    lspec = lambda a: pl.BlockSpec((a.shape[0], tm // a.shape[0], a.shape[2]), lambda i: (0, i, 0))
    return pl.pallas_call(
        functools.partial(_dil_merge_kernel, ds=ds),
        grid=(rows // tm,),
        in_specs=[ospec(a) for a in os_] + [lspec(a) for a in ls],
        out_specs=pl.BlockSpec((tm, H * D), lambda i: (i, 0)),
        out_shape=jax.ShapeDtypeStruct((rows, H * D), F32),
        scratch_shapes=[pltpu.VMEM((H, tm, D), F32)],
        compiler_params=_cp("parallel"),
        name="dil_merge",
    )(*os_, *ls)


def _nsa_compress_kernel(tab_ref, pg_ref, pe_ref, w1_ref, w2_ref, o_ref, hs_ref, *, n_pages, ncmp):
    del tab_ref
    p = pl.program_id(1)
    hb = PAGE // NSA_CMP_STRIDE
    ncol = 2 * NSA_KV_HEADS
    rowlen = 4 * NSA_KV_HEADS * HEAD_DIM
    for c in range(ncol):
        for j in range(NSA_CMP_STRIDE):
            hs_ref[c, pl.ds(pl.multiple_of(p * hb, hb), hb), j * HEAD_DIM:(j + 1) * HEAD_DIM] = (
                pg_ref[0, :, j * rowlen + c * HEAD_DIM:j * rowlen + (c + 1) * HEAD_DIM])

    @pl.when(p == n_pages - 1)
    def _():
        half = NSA_CMP_STRIDE * HEAD_DIM
        R = n_pages * hb
        rowi = lax.broadcasted_iota(I32, (R, HEAD_DIM), 0)
        for c in range(ncol):
            s = c // NSA_KV_HEADS
            H = hs_ref[c]
            a = _dot((H + pe_ref[s, 0:1, :]).astype(BF16), w1_ref[s, 0:half, :])
            bm = _dot((H + pe_ref[s, 1:2, :]).astype(BF16), w1_ref[s, half:2 * half, :])
            pre = a + pltpu.roll(bm, R - 1, 0)
            out = _dot(_silu(pre).astype(BF16), w2_ref[s])
            o_ref[0, c] = jnp.where(rowi < ncmp, out, 0.0).astype(BF16)


def nsa_compress(pages, table, B, n_pages, pe, w1, w2):
    hb = PAGE // NSA_CMP_STRIDE
    R = n_pages * hb
    ncmp = (n_pages * PAGE - NSA_CMP_LEN) // NSA_CMP_STRIDE + 1
    half = NSA_CMP_STRIDE * HEAD_DIM
    pv = pages.reshape(pages.shape[0], hb, NSA_CMP_STRIDE * pages.shape[2])
    grid_spec = pltpu.PrefetchScalarGridSpec(
        num_scalar_prefetch=1,
        grid=(B, n_pages),
        in_specs=[pl.BlockSpec((1,) + pv.shape[1:], lambda b, p, tab: (tab[b * n_pages + p], 0, 0)),
                  pl.BlockSpec((2, 2, half), lambda b, p, tab: (0, 0, 0)),
                  pl.BlockSpec(w1.shape, lambda b, p, tab: (0, 0, 0)),
                  pl.BlockSpec(w2.shape, lambda b, p, tab: (0, 0, 0))],
        out_specs=pl.BlockSpec((1, 4, R, HEAD_DIM), lambda b, p, tab: (b, 0, 0, 0)),
        scratch_shapes=[pltpu.VMEM((4, R, half), F32)])
    return pl.pallas_call(
        functools.partial(_nsa_compress_kernel, n_pages=n_pages, ncmp=ncmp),
        grid_spec=grid_spec,
        out_shape=jax.ShapeDtypeStruct((B, 4, R, HEAD_DIM), BF16),
        compiler_params=_cp("parallel", "arbitrary"),
        name="nsa_compress",
    )(table, pv, pe.reshape(2, 2, half), w1, w2)


def _flash_update(m_ref, l_ref, a_ref, k, s, v_ones):
    D = HEAD_DIM
    m_old = m_ref[k]
    m_new = jnp.maximum(m_old, jnp.max(s, axis=-1, keepdims=True))
    m_use = jnp.maximum(m_new, -1e20)
    alpha = jnp.exp(m_old - m_use)
    pv = _dot(jnp.exp(s - m_use).astype(BF16), v_ones)
    l_ref[k] = alpha * l_ref[k] + pv[:, D:D + 1]
    a_ref[k] = alpha * a_ref[k] + pv[:, :D]
    m_ref[k] = m_new


def _nsa_qstack(q_ref, k, scale=None):
    G, D = NSA_GROUP, HEAD_DIM
    q = jnp.concatenate([q_ref[:, (k * G + g) * D:(k * G + g + 1) * D] for g in range(G)], axis=0)
    return (q if scale is None else q * scale).astype(BF16)


def _nsa_cmp_select(qs, kc, vc, c2s, pos_r, pos_q, n_slc, ncmp):
    R, tq = qs.shape[0], pos_q.shape[0]
    nidx = lax.broadcasted_iota(I32, (R, kc.shape[0]), 1)
    s = _dot_nt(qs, kc) * HEAD_DIM ** -0.5
    valid = (NSA_CMP_STRIDE * nidx + NSA_CMP_LEN - 1 <= pos_r) & (nidx < ncmp)
    s = jnp.where(valid, s, NEG)
    pr = jnp.where(valid, jnp.exp(s - jnp.max(s, axis=-1, keepdims=True)), 0.0)
    pc = pr / jnp.maximum(jnp.sum(pr, axis=-1, keepdims=True), 1e-30)
    o_cmp = _dot(pc.astype(BF16), vc)
    psum = pc[0:tq]
    for g in range(1, NSA_GROUP):
        psum = psum + pc[g * tq:(g + 1) * tq]
    ph, pm, plo = _split3(psum)
    imp = _dot(ph, c2s) + _dot(pm, c2s) + _dot(plo, c2s)
    j = lax.broadcasted_iota(I32, (tq, LANES), 1)
    cur = pos_q // NSA_SLC_LEN
    forced = (j == 0) | (j == cur) | (j == cur - 1)
    score = jnp.where(forced, 1e4, jnp.where(j <= cur, imp, -1.0))
    score = jnp.where(j < n_slc, score, -2.0)
    cnt = jnp.zeros((tq, LANES), F32)
    for jp in range(n_slc):
        col = score[:, jp:jp + 1]
        cnt = cnt + ((col > score) | ((col == score) & (jp < j))).astype(F32)
    return o_cmp, (cnt < float(min(NSA_TOP_N, n_slc))).astype(F32)


def _nsa_attn_kernel(qi_ref, kb_ref, q_ref, gate_ref, kc_ref, c2s_ref, blk_ref, kv_ref, wkv_ref, o_ref,
                     ocmp, qa, m_s, l_s, a_s, m_w, l_w, a_w, *, tq, KB, n_slc, ncmp):
    step = pl.program_id(1)
    qi, kb = qi_ref[step], kb_ref[step]
    G, KVH, D = NSA_GROUP, NSA_KV_HEADS, HEAD_DIM
    R = G * tq
    pos_r = qi * tq + lax.broadcasted_iota(I32, (R, 1), 0) % tq
    pos_q = qi * tq + lax.broadcasted_iota(I32, (tq, 1), 0)
    kb_last = (qi * tq + tq - 1) // KB

    @pl.when(kb == 0)
    def _():
        for k in range(KVH):
            ocmp[k], sel = _nsa_cmp_select(_nsa_qstack(q_ref, k), kc_ref[0, k], kc_ref[0, KVH + k], c2s_ref[...],
                                           pos_r, pos_q, n_slc, ncmp)
            bias = ((sel - 1.0) * -NEG).astype(BF16)
            qa[k] = jnp.concatenate([_nsa_qstack(q_ref, k, D ** -0.5), jnp.concatenate([bias] * G, axis=0)], axis=1)
        for ref in (m_s, m_w):
            ref[...] = jnp.full(ref.shape, NEG, F32)
        for ref in (l_s, a_s, l_w, a_w):
            ref[...] = jnp.zeros(ref.shape, F32)

    dist = pos_r - (kb * KB + lax.broadcasted_iota(I32, (R, KB), 1))
    ones = jnp.ones((KB, D), BF16)
    for k in range(KVH):
        ks = jnp.concatenate([kv_ref[:, k * D:(k + 1) * D].astype(BF16), blk_ref[...]], axis=1)
        vs = jnp.concatenate([kv_ref[:, (KVH + k) * D:(KVH + k + 1) * D].astype(BF16), ones], axis=1)
        s = _dot_nt(qa[k], ks)

        @pl.when(kb == kb_last)
        def _():
            _flash_update(m_s, l_s, a_s, k, jnp.where(dist >= 0, s, NEG), vs)

        @pl.when(kb != kb_last)
        def _():
            _flash_update(m_s, l_s, a_s, k, s, vs)

    @pl.when(kb >= kb_last - NSA_WINDOW // KB)
    def _():
        valid = (dist >= 0) & (dist <= NSA_WINDOW)
        for k in range(KVH):
            kw = wkv_ref[:, k * D:(k + 1) * D].astype(BF16)
            vw = jnp.concatenate([wkv_ref[:, (KVH + k) * D:(KVH + k + 1) * D].astype(BF16), ones], axis=1)
            _flash_update(m_w, l_w, a_w, k, jnp.where(valid, _dot_nt(qa[k, :, 0:D], kw), NEG), vw)

    @pl.when(kb == kb_last)
    def _():
        gs = _sigmoid(gate_ref[...])
        for k in range(KVH):
            _nsa_gated_out(o_ref, gs, k, tq, ocmp[k], a_s[k] / jnp.maximum(l_s[k], 1e-30),
                           a_w[k] / jnp.maximum(l_w[k], 1e-30))


def _nsa_gated_out(o_ref, gs, k, tq, o_c, o_s, o_w):
    for g in range(NSA_GROUP):
        hd = k * NSA_GROUP + g
        rows = slice(g * tq, (g + 1) * tq)
        o = (gs[:, 3 * hd:3 * hd + 1] * o_c[rows] + gs[:, 3 * hd + 1:3 * hd + 2] * o_s[rows]
             + gs[:, 3 * hd + 2:3 * hd + 3] * o_w[rows])
        o_ref[:, hd * HEAD_DIM:(hd + 1) * HEAD_DIM] = o


def _softmax_two(s1, ok1, v1, s2, ok2, v2):
    s1 = jnp.where(ok1, s1, NEG)
    s2 = jnp.where(ok2, s2, NEG)
    m = jnp.maximum(jnp.max(s1, axis=-1, keepdims=True), jnp.max(s2, axis=-1, keepdims=True))
    p1 = jnp.where(ok1, jnp.exp(s1 - m), 0.0)
    p2 = jnp.where(ok2, jnp.exp(s2 - m), 0.0)
    l = jnp.sum(p1, axis=-1, keepdims=True) + jnp.sum(p2, axis=-1, keepdims=True)
    return (_dot(p1.astype(BF16), v1) + _dot(p2.astype(BF16), v2)) / jnp.maximum(l, 1e-30)


def _nsa_sample_kernel(*refs, Ts, n_pages, Lw, n_slc, ncmp, past):
    (tab_ref, q_ref, gate_ref, xn_ref, xw_ref, wb_ref, pe_ref, w1_ref, w2_ref, c2s_ref, exp_ref) = refs[:11]
    pg_refs = refs[11:11 + n_pages]
    o_ref, nwb_ref, hs_ref = refs[12 + n_pages:]
    del tab_ref
    G, KVH, D = NSA_GROUP, NSA_KV_HEADS, HEAD_DIM
    R = G * Ts
    scale = D ** -0.5
    rpt = 4 * KVH
    hb = PAGE // NSA_CMP_STRIDE
    half = NSA_CMP_STRIDE * D
    pos_r = past + lax.broadcasted_iota(I32, (R, 1), 0) % Ts
    pos_q = past + lax.broadcasted_iota(I32, (Ts, 1), 0)

    def tile_rows(x):
        return jnp.concatenate([x] * G, axis=0)

    for u in range(n_pages):
        for c in range(2 * KVH):
            for j in range(NSA_CMP_STRIDE):
                hs_ref[c, u * hb:(u + 1) * hb, j * D:(j + 1) * D] = (
                    pg_refs[u][0, pl.ds(j * rpt + c, hb, stride=NSA_CMP_STRIDE * rpt), :])
    nrow = n_pages * hb
    rowi = lax.broadcasted_iota(I32, (nrow, D), 0)
    cmp = []
    for c in range(2 * KVH):
        s = c // KVH
        H = hs_ref[c]
        a = _dot((H + pe_ref[s, 0:1, :]).astype(BF16), w1_ref[s, 0:half, :])
        bm = _dot((H + pe_ref[s, 1:2, :]).astype(BF16), w1_ref[s, half:2 * half, :])
        out = _dot(_silu(a + pltpu.roll(bm, nrow - 1, 0)).astype(BF16), w2_ref[s])
        cmp.append(jnp.where(rowi < ncmp, out, 0.0).astype(BF16))

    gs = _sigmoid(gate_ref[...])
    jn = lax.broadcasted_iota(I32, (R, Ts), 1)
    causal = past + jn <= pos_r
    wpos = past - Lw + lax.broadcasted_iota(I32, (R, Lw), 1)
    wdist = pos_r - wpos
    wvalid = (wdist >= 0) & (wdist <= NSA_WINDOW)
    blk = past // NSA_SLC_LEN
    for k in range(KVH):
        qs = _nsa_qstack(q_ref, k)
        o_c, selk = _nsa_cmp_select(qs, cmp[k], cmp[KVH + k], c2s_ref[...], pos_r, pos_q, n_slc, ncmp)
        ks = jnp.concatenate([pg[0, pl.ds(2 * KVH + k, PAGE, stride=rpt), :] for pg in pg_refs],
                             axis=0).astype(BF16)
        vs = jnp.concatenate([pg[0, pl.ds(3 * KVH + k, PAGE, stride=rpt), :] for pg in pg_refs],
                             axis=0).astype(BF16)
        ok_past = tile_rows(_dot(selk.astype(BF16), exp_ref[...])) > 0.5
        ok_new = (tile_rows(selk[:, blk:blk + 1]) > 0.5) & causal
        kn = xn_ref[:, (2 * KVH + k) * D:(2 * KVH + k + 1) * D].astype(BF16)
        vn = xn_ref[:, (3 * KVH + k) * D:(3 * KVH + k + 1) * D].astype(BF16)
        o_s = _softmax_two(_dot_nt(qs, ks) * scale, ok_past, vs, _dot_nt(qs, kn) * scale, ok_new, vn)
        kw = wb_ref[0, pl.ds(k, Lw, stride=2 * KVH), :].astype(BF16)
        vw = wb_ref[0, pl.ds(KVH + k, Lw, stride=2 * KVH), :].astype(BF16)
        kwn = xw_ref[:, k * D:(k + 1) * D]
        vwn = xw_ref[:, (KVH + k) * D:(KVH + k + 1) * D]
        o_w = _softmax_two(_dot_nt(qs, kw) * scale, wvalid, vw,
                           _dot_nt(qs, kwn.astype(BF16)) * scale, causal, vwn.astype(BF16))
        _nsa_gated_out(o_ref, gs, k, Ts, o_c, o_s, o_w)
        nwb_ref[0, pl.ds((Lw - Ts) * 2 * KVH + k, Ts, stride=2 * KVH), :] = kwn
        nwb_ref[0, pl.ds((Lw - Ts) * 2 * KVH + KVH + k, Ts, stride=2 * KVH), :] = vwn
    nwb_ref[0, 0:(Lw - Ts) * 2 * KVH, :] = wb_ref[0, Ts * 2 * KVH:Lw * 2 * KVH, :]


def nsa_sample(q, gate, xnew, xwin, pool, table, wbuf, pe, w1, w2, B, Ts, row0, past, o_prev):
    M, QW = q.shape
    D = HEAD_DIM
    n_pages = past // PAGE
    assert past % PAGE == 0 and Ts < NSA_CMP_STRIDE
    Lw = wbuf.shape[1] // (2 * NSA_KV_HEADS)
    L = past + Ts
    n_slc = -(-L // NSA_SLC_LEN)
    ncmp = (L - NSA_CMP_LEN) // NSA_CMP_STRIDE + 1
    hb = PAGE // NSA_CMP_STRIDE
    half = NSA_CMP_STRIDE * D
    c2s = _cmp_to_slc(n_pages * hb, ncmp, n_slc)
    expand = jnp.asarray(np.arange(LANES)[:, None] == np.arange(past)[None, :] // NSA_SLC_LEN, BF16)
    r0 = row0 // Ts
    rows = lambda w: pl.BlockSpec((Ts, w), lambda b, tab: (r0 + b, 0))
    full = lambda a: pl.BlockSpec(a.shape, lambda b, tab: (0,) * a.ndim)
    page = lambda u: pl.BlockSpec((1,) + pool.shape[1:], lambda b, tab: (tab[b * n_pages + u], 0, 0))
    pe2 = pe.reshape(2, 2, half)
    in_specs = ([rows(QW), rows(LANES), rows(xnew.shape[1]), rows(xwin.shape[1]),
                 pl.BlockSpec((1,) + wbuf.shape[1:], lambda b, tab: (b, 0, 0)),
                 full(pe2), full(w1), full(w2), full(c2s), full(expand)]
                + [page(u) for u in range(n_pages)] + [pl.BlockSpec(memory_space=pl.ANY)])
    grid_spec = pltpu.PrefetchScalarGridSpec(
        num_scalar_prefetch=1,
        grid=(B,),
        in_specs=in_specs,
        out_specs=[rows(QW), pl.BlockSpec((1,) + wbuf.shape[1:], lambda b, tab: (b, 0, 0))],
        scratch_shapes=[pltpu.VMEM((2 * NSA_KV_HEADS, n_pages * hb, half), F32)])
    return pl.pallas_call(
        functools.partial(_nsa_sample_kernel, Ts=Ts, n_pages=n_pages, Lw=Lw, n_slc=n_slc, ncmp=ncmp, past=past),
        grid_spec=grid_spec,
        out_shape=[jax.ShapeDtypeStruct((M, QW), F32), jax.ShapeDtypeStruct(wbuf.shape, F32)],
        input_output_aliases={11 + n_pages: 0},
        compiler_params=_cp("parallel"),
        name="nsa_sample",
    )(table, q, gate, xnew, xwin, wbuf, pe2, w1, w2, c2s, expand, *([pool] * n_pages), o_prev)


def _cmp_to_slc(nrows, ncmp, n_slc):
    i = np.arange(nrows)[:, None] * NSA_CMP_STRIDE
    j = np.arange(LANES)[None, :] * NSA_SLC_LEN
    ov = np.clip(np.minimum(i + NSA_CMP_LEN, j + NSA_SLC_LEN) - np.maximum(i, j), 0, None) / NSA_CMP_LEN
    ov = np.where((np.arange(nrows)[:, None] < ncmp) & (np.arange(LANES)[None, :] < n_slc), ov, 0.0)
    return jnp.asarray(ov, BF16)


def nsa_attn(q, gate, kc, new, win, B, T):
    M = q.shape[0]
    D = HEAD_DIM
    QW = NSA_HEADS * D
    KW = 2 * NSA_KV_HEADS * D
    tq = min(PAGE, T)
    KB = min(NSA_WINDOW, T)
    nqb, nkb = T // tq, T // KB
    n_slc = -(-T // NSA_SLC_LEN)
    ncmp = (T - NSA_CMP_LEN) // NSA_CMP_STRIDE + 1
    R = NSA_GROUP * tq
    c2s = _cmp_to_slc(kc.shape[2], ncmp, n_slc)
    blocks = jnp.asarray(np.arange(T)[:, None] // NSA_SLC_LEN == np.arange(LANES)[None, :], BF16)
    pairs = [(qi, kb) for qi in range(nqb) for kb in range((qi * tq + tq - 1) // KB + 1)]
    qi_tab = jnp.asarray([p[0] for p in pairs], I32)
    kb_tab = jnp.asarray([p[1] for p in pairs], I32)
    qrow = lambda b, s, qt, kt: (b * nqb + qt[s], 0)
    in_specs = [pl.BlockSpec((tq, QW), qrow),
                pl.BlockSpec((tq, LANES), qrow),
                pl.BlockSpec((1,) + kc.shape[1:], lambda b, s, qt, kt: (b, 0, 0, 0)),
                pl.BlockSpec(c2s.shape, lambda b, s, qt, kt: (0, 0)),
                pl.BlockSpec((KB, LANES), lambda b, s, qt, kt: (kt[s], 0)),
                pl.BlockSpec((KB, KW), lambda b, s, qt, kt: (b * nkb + kt[s], 1)),
                pl.BlockSpec((KB, KW), lambda b, s, qt, kt: (b * nkb + kt[s], 0))]
    vec = lambda: pltpu.VMEM((NSA_KV_HEADS, R, 1), F32)
    mat = lambda: pltpu.VMEM((NSA_KV_HEADS, R, D), F32)
    grid_spec = pltpu.PrefetchScalarGridSpec(
        num_scalar_prefetch=2,
        grid=(B, len(pairs)),
        in_specs=in_specs,
        out_specs=pl.BlockSpec((tq, QW), qrow),
        scratch_shapes=[mat(), pltpu.VMEM((NSA_KV_HEADS, R, 2 * D), BF16),
                        vec(), vec(), mat(), vec(), vec(), mat()])
    return pl.pallas_call(
        functools.partial(_nsa_attn_kernel, tq=tq, KB=KB, n_slc=n_slc, ncmp=ncmp),
        grid_spec=grid_spec,
        out_shape=jax.ShapeDtypeStruct((M, QW), F32),
        compiler_params=_cp("parallel", "arbitrary"),
        name="nsa_attn_prompt",
    )(qi_tab, kb_tab, q, gate, kc, c2s, blocks, new, win)


def _router_kernel(x_ref, g_ref, wr_ref, h_ref, e_ref, p_ref):
    xn = _rms(x_ref[...], g_ref[...])
    h_ref[...] = xn
    xh, xm, xl = _split3(xn)
    wh, wm, wl = _split3(wr_ref[...])
    logits = (_dot(xh, wh) + _dot(xh, wm) + _dot(xm, wh)
              + _dot(xh, wl) + _dot(xl, wh) + _dot(xm, wm))
    lane = lax.broadcasted_iota(I32, logits.shape, 1)
    lg = jnp.where(lane < N_EXPERTS, logits, -jnp.inf)
    m1 = jnp.max(lg, axis=-1, keepdims=True)
    i1 = jnp.min(jnp.where(lg == m1, lane, LANES), axis=-1, keepdims=True)
    lg2 = jnp.where(lane == i1, -jnp.inf, lg)
    m2 = jnp.max(lg2, axis=-1, keepdims=True)
    i2 = jnp.min(jnp.where(lg2 == m2, lane, LANES), axis=-1, keepdims=True)
    e = jnp.exp(m2 - m1)
    e_ref[...] = jnp.where(lane == 0, i1, jnp.where(lane == 1, i2, 0))
    p_ref[...] = jnp.where(lane == 0, 1.0 / (1.0 + e), jnp.where(lane == 1, e / (1.0 + e), 0.0))


def moe_router(x, g, router):
    M, D = x.shape
    tm = _tile(M, 512)
    wr = jnp.zeros((D, LANES), F32).at[:, :N_EXPERTS].set(router)
    return pl.pallas_call(
        _router_kernel,
        grid=(M // tm,),
        in_specs=[pl.BlockSpec((tm, D), lambda i: (i, 0)),
                  pl.BlockSpec((1, D), lambda i: (0, 0)),
                  pl.BlockSpec((D, LANES), lambda i: (0, 0))],
        out_specs=[pl.BlockSpec((tm, D), lambda i: (i, 0)),
                   pl.BlockSpec((tm, LANES), lambda i: (i, 0)),
                   pl.BlockSpec((tm, LANES), lambda i: (i, 0))],
        out_shape=[jax.ShapeDtypeStruct((M, D), F32),
                   jax.ShapeDtypeStruct((M, LANES), I32),
                   jax.ShapeDtypeStruct((M, LANES), F32)],
        compiler_params=_cp("parallel"),
        name="moe_router",
    )(x, g.reshape(1, D), wr)


def _row_copy(src_ref, dst_ref, sem, src_row, dst_row):
    return pltpu.make_async_copy(src_ref.at[pl.ds(src_row, 1)], dst_ref.at[pl.ds(dst_row, 1)], sem)


def _moe_ffn_kernel(be_ref, rt_ref, nu_ref, h_ref, wg_ref, wu_ref, wd_ref, y_ref,
                    xbuf, xb16, acc, sems, *, tm, n_f):
    del be_ref
    r, f = pl.program_id(0), pl.program_id(1)
    n_used = nu_ref[0]
    used = r < n_used
    slot = r % 2

    def gather(block, slot_):
        def issue(i, carry):
            _row_copy(h_ref, xbuf.at[slot_], sems.at[slot_], rt_ref[block * tm + i], i).start()
            return carry

        lax.fori_loop(0, tm, issue, 0, unroll=8)

    @pl.when(used & (f == 0) & (r == 0))
    def _():
        gather(0, 0)

    @pl.when(used & (f == 0))
    def _():
        def drain(i, carry):
            _row_copy(h_ref, xbuf.at[slot], sems.at[slot], 0, i).wait()
            return carry

        lax.fori_loop(0, tm, drain, 0, unroll=8)
        xb16[...] = xbuf[slot].astype(BF16)
        acc[...] = jnp.zeros_like(acc)

    @pl.when((r + 1 < n_used) & (f == 0))
    def _():
        gather(r + 1, 1 - slot)

    @pl.when(used)
    def _():
        x = xb16[...]
        hg = _dot(x, wg_ref[0])
        hu = _dot(x, wu_ref[0])
        acc[...] += _dot((_silu(hg) * hu).astype(BF16), wd_ref[0])

    @pl.when(used & (f == n_f - 1))
    def _():
        y_ref[...] = acc[...]

    @pl.when(jnp.logical_not(used) & (f == n_f - 1))
    def _():
        y_ref[...] = jnp.zeros_like(y_ref)


def moe_ffn(h, block_expert, row_tok, n_used, w_up, w_down, layer):
    D = h.shape[1]
    F = w_down.shape[2]
    tm = MOE_ROWS
    nb = block_expert.shape[0]
    tf = F // 2 if (F // 2) % LANES == 0 else F
    n_f = F // tf
    live = lambda r, f, nu: jnp.where(r < nu[0], f, 0)
    grid_spec = pltpu.PrefetchScalarGridSpec(
        num_scalar_prefetch=3,
        grid=(nb, n_f),
        in_specs=[pl.BlockSpec(memory_space=pl.ANY),
                  pl.BlockSpec((None, 1, D, tf), lambda r, f, be, rt, nu: (layer, be[r], 0, live(r, f, nu))),
                  pl.BlockSpec((None, 1, D, tf), lambda r, f, be, rt, nu: (layer, be[r], 0, n_f + live(r, f, nu))),
                  pl.BlockSpec((None, 1, tf, D), lambda r, f, be, rt, nu: (layer, be[r], live(r, f, nu), 0))],
        out_specs=pl.BlockSpec((tm, D), lambda r, f, be, rt, nu: (r, 0)),
        scratch_shapes=[pltpu.VMEM((2, tm, D), F32), pltpu.VMEM((tm, D), BF16), pltpu.VMEM((tm, D), F32),
                        pltpu.SemaphoreType.DMA((2,))])
    return pl.pallas_call(
        functools.partial(_moe_ffn_kernel, tm=tm, n_f=n_f),
        grid_spec=grid_spec,
        out_shape=jax.ShapeDtypeStruct((nb * tm, D), F32),
        compiler_params=_cp("arbitrary", "arbitrary"),
        name="moe_ffn",
    )(block_expert, row_tok, n_used, h, w_up, w_up, w_down)


def _moe_combine_kernel(p0_ref, p1_ref, x_ref, g_ref, y_ref, o_ref, ybuf, sem, *, tm):
    i = pl.program_id(0)

    def issue(t, carry):
        _row_copy(y_ref, ybuf.at[0], sem, p0_ref[i * tm + t], t).start()
        _row_copy(y_ref, ybuf.at[1], sem, p1_ref[i * tm + t], t).start()
        return carry

    lax.fori_loop(0, tm, issue, 0, unroll=8)

    def drain(t, carry):
        _row_copy(y_ref, ybuf.at[0], sem, 0, t).wait()
        _row_copy(y_ref, ybuf.at[1], sem, 0, t).wait()
        return carry

    lax.fori_loop(0, tm, drain, 0, unroll=8)
    g = g_ref[...]
    o_ref[...] = x_ref[...] + (g[:, 0:1] * ybuf[0] + g[:, 1:2] * ybuf[1])


def moe_combine(x, gates, y, pos0, pos1):
    M, D = x.shape
    tm = _tile(M, 256)
    grid_spec = pltpu.PrefetchScalarGridSpec(
        num_scalar_prefetch=2,
        grid=(M // tm,),
        in_specs=[pl.BlockSpec((tm, D), lambda i, a, b: (i, 0)),
                  pl.BlockSpec((tm, LANES), lambda i, a, b: (i, 0)),
                  pl.BlockSpec(memory_space=pl.ANY)],
        out_specs=pl.BlockSpec((tm, D), lambda i, a, b: (i, 0)),
        scratch_shapes=[pltpu.VMEM((2, tm, D), F32), pltpu.SemaphoreType.DMA(())])
    return pl.pallas_call(
        functools.partial(_moe_combine_kernel, tm=tm),
        grid_spec=grid_spec,
        out_shape=jax.ShapeDtypeStruct((M, D), F32),
        compiler_params=_cp("arbitrary"),
        name="moe_combine",
    )(pos0, pos1, x, gates, y)


def moe_layer(x, g, router, w_up, w_down, layer):
    M = x.shape[0]
    h, eidx, gates = moe_router(x, g, router)
    A = 2 * M
    e = eidx[:, :2].reshape(A)
    onehot = (e[:, None] == jnp.arange(N_EXPERTS, dtype=I32)[None, :]).astype(I32)
    csum = jnp.cumsum(onehot, axis=0)
    rank = jnp.sum(csum * onehot, axis=1) - 1
    counts = csum[-1]
    padded = (counts + MOE_ROWS - 1) // MOE_ROWS * MOE_ROWS
    pend = jnp.cumsum(padded)
    dest = ((pend - padded)[e] + rank).astype(I32)
    nb = (A + N_EXPERTS * (MOE_ROWS - 1) + MOE_ROWS - 1) // MOE_ROWS
    row_tok = jnp.zeros((nb * MOE_ROWS,), I32).at[dest].set(jnp.arange(A, dtype=I32) // 2)
    block_expert = jnp.minimum(
        jnp.searchsorted(pend, jnp.arange(nb, dtype=I32) * MOE_ROWS, side='right'), N_EXPERTS - 1).astype(I32)
    n_used = (pend[-1:] // MOE_ROWS).astype(I32)
    y = moe_ffn(h, block_expert, row_tok, n_used, w_up, w_down, layer)
    dest2 = dest.reshape(M, 2)
    return moe_combine(x, gates, y, dest2[:, 0], dest2[:, 1])


def _rope_tables(pos, half):
    inv = ROPE_THETA ** (-jnp.arange(half, dtype=F32) / half)
    ang = pos.astype(F32)[:, None] * inv[None, :]
    return jnp.cos(ang), jnp.sin(ang)


def _row_rope_tables(Bp, T, Bs, Ts, past):
    pos = jnp.concatenate([jnp.tile(jnp.arange(T, dtype=I32), Bp),
                           jnp.tile(past + jnp.arange(Ts, dtype=I32), Bs)])
    cos, sin = _rope_tables(pos, HEAD_DIM // 2)
    return jnp.concatenate([cos, cos], axis=1), jnp.concatenate([-sin, sin], axis=1)


def kernel(x_prompt, x_sample, state_ret, state_hgrn, cache_dil_w128, cache_dil_w512, cache_dil_w2048,
           cache_nsa_kv, cache_nsa_win, page_table, norm_mix, norm_ffn,
           ret_w_in, ret_gn_g, ret_gn_b, ret_w_out,
           hg_w_in, hg_lower_bounds, hg_gn_g, hg_w_out,
           dil_w_in, dil_qn_g, dil_kn_g, dil_w_out,
           nsa_w_in, nsa_w_gate, nsa_qn_g, nsa_kn_g, nsa_cmp_pe, nsa_cmp_w1, nsa_cmp_w2, nsa_w_out,
           ffn_w_up, ffn_w_down, moe_router_w, moe_w_up, moe_w_down):
    Bp, T, D = x_prompt.shape
    Bs, Ts, _ = x_sample.shape
    Np, Ns = Bp * T, Bs * Ts
    M = Np + Ns
    n_pages = page_table.shape[1]
    past = n_pages * PAGE
    bf = lambda w: w.astype(BF16)

    x = jnp.concatenate([x_prompt.reshape(Np, D), x_sample.reshape(Ns, D)], axis=0)
    lb_all = jnp.cumsum(jax.nn.softmax(hg_lower_bounds.astype(F32), axis=0), axis=0)
    lb_all = lb_all - lb_all[0:1]

    proj = norm_proj(x, norm_mix[0], bf(ret_w_in[0]), 1024)
    cos_r, sin_r = _rope_tables(jnp.arange(max(T, past + Ts), dtype=I32), RET_DK // 2)
    y, ret_p = ret_core(proj, Bp, T, 0, cos_r, sin_r, 0, ret_gn_g[0], ret_gn_b[0], None, None)
    y, ret_s = ret_core(proj, Bs, Ts, Np, cos_r, sin_r, past, ret_gn_g[0], ret_gn_b[0], state_ret[0], y)
    x = out_proj(y, bf(ret_w_out[0]), x)
    x = ffn_dense(x, norm_ffn[0], bf(ffn_w_up[0]), bf(ffn_w_down[0]))

    proj = norm_proj(x, norm_mix[1], bf(hg_w_in[0]), 1024)
    y, hg_p = hgrn_core(proj, Bp, T, 0, lb_all[1], hg_gn_g[0], None, None)
    y, hg_s = hgrn_core(proj, Bs, Ts, Np, lb_all[1], hg_gn_g[0], state_hgrn[0], y)
    x = out_proj(y, bf(hg_w_out[0]), x)
    moe_up, moe_down = bf(moe_w_up), bf(moe_w_down)
    x = moe_layer(x, norm_ffn[1], moe_router_w[0], moe_up, moe_down, 0)

    cos_h, sin_h = _row_rope_tables(Bp, T, Bs, Ts, past)
    HW = DIL_HEADS * HEAD_DIM
    dil_in_w = dil_w_in.shape[2]
    proj = norm_proj(x, norm_mix[2], bf(dil_w_in[0]), 3 * HW)
    ones = jnp.ones((DIL_HEADS, HEAD_DIM), F32)
    gains = jnp.concatenate([jnp.concatenate([ones * dil_qn_g[0, gi], ones * dil_kn_g[0, gi], ones], axis=0)
                             for gi in range(len(DIL_GROUPS))], axis=0)
    flags = ([True] * (2 * DIL_HEADS) + [False] * DIL_HEADS) * len(DIL_GROUPS)
    dilated = [(gi, d) for gi, (_, d) in enumerate(DIL_GROUPS) if d > 1]
    pp, *by_class = prep_heads(proj, cos_h, sin_h, gains, flags, [(0, dil_in_w)],
                               [(gi * 3 * HW, (gi + 1) * 3 * HW, d) for gi, d in dilated])
    by_class = dict(zip([gi for gi, _ in dilated], by_class))
    caches = (cache_dil_w128, cache_dil_w512, cache_dil_w2048)
    op, lp, od, ld, dil_p, dil_s = [], [], [], [], [], []
    for gi, (window, _) in enumerate(DIL_GROUPS):
        if gi in by_class:
            o_g, l_g = dil_prompt(by_class[gi], gi, 0, Bp, T)
        else:
            o_g, l_g = dil_prompt(pp.reshape(1, M, dil_in_w), gi, gi * 3, Bp, T)
        op.append(o_g)
        lp.append(l_g)
        buf = caches[gi][0]
        o_g, l_g, nbuf = dil_decode(pp, gi, Bs, Ts, Np, buf.reshape(Bs, -1, HEAD_DIM))
        od.append(o_g)
        ld.append(l_g)
        keep = min(window, T)
        kv = pp[:Np, gi * 3 * HW + HW:(gi + 1) * 3 * HW].reshape(Bp, T, 2, DIL_HEADS, HEAD_DIM)
        dil_p.append(kv[:, T - keep:][None])
        dil_s.append(nbuf.reshape(buf.shape)[None])
    merged = jnp.concatenate([dil_merge(op, lp, Np), dil_merge(od, ld, Ns)], axis=0)
    x = out_proj(merged, bf(dil_w_out[0]), x)
    x = ffn_dense(x, norm_ffn[2], bf(ffn_w_up[1]), bf(ffn_w_down[1]))

    QW = NSA_HEADS * HEAD_DIM
    KW = NSA_KV_HEADS * HEAD_DIM
    w_in = jnp.concatenate([nsa_w_in[0], nsa_w_gate[0],
                            jnp.zeros((D, LANES - nsa_w_gate.shape[2]), F32)], axis=1)
    proj = norm_proj(x, norm_mix[3], bf(w_in), w_in.shape[1] // 3)
    ones_q = jnp.ones((NSA_HEADS, HEAD_DIM), F32)
    ones_k = jnp.ones((NSA_KV_HEADS, HEAD_DIM), F32)
    gains = jnp.concatenate([ones_q * nsa_qn_g[0]]
                            + [blk for s in range(3) for blk in (ones_k * nsa_kn_g[0, s], ones_k)]
                            + [jnp.ones((1, HEAD_DIM), F32)], axis=0)
    flags = [True] * NSA_HEADS + ([True] * NSA_KV_HEADS + [False] * NSA_KV_HEADS) * 3 + [False]
    q, new, win, gate = prep_heads(proj, cos_h, sin_h, gains, flags,
                                   [(0, QW), (QW, QW + 4 * KW), (QW + 4 * KW, QW + 6 * KW),
                                    (QW + 6 * KW, QW + 6 * KW + LANES)])
    w1, w2 = bf(nsa_cmp_w1[0]), bf(nsa_cmp_w2[0])
    ppages = new.reshape(M // PAGE, PAGE, 4 * KW)
    ptab = jnp.arange(Bp * (T // PAGE), dtype=I32)
    kc_p = nsa_compress(ppages, ptab, Bp, T // PAGE, nsa_cmp_pe[0], w1, w2)
    o = nsa_attn(q, gate, kc_p, new, win, Bp, T)
    pool = cache_nsa_kv[0].reshape(cache_nsa_kv.shape[1], -1, HEAD_DIM)
    wbuf = cache_nsa_win[0]
    o, win_s = nsa_sample(q, gate, new, win, pool, page_table.reshape(-1).astype(I32),
                          wbuf.reshape(Bs, -1, HEAD_DIM), nsa_cmp_pe[0], w1, w2, Bs, Ts, Np, past, o)
    x = out_proj(o, bf(nsa_w_out[0]), x)
    x = moe_layer(x, norm_ffn[3], moe_router_w[1], moe_up, moe_down, 1)

    keep = min(NSA_WINDOW, T)
    win_p = win[:Np].reshape(Bp, T, 2, NSA_KV_HEADS, HEAD_DIM)[:, T - keep:]
    win_s = win_s.reshape(wbuf.shape)
    return (x[:Np].reshape(Bp, T, D), x[Np:].reshape(Bs, Ts, D),
            ret_p[None], ret_s[None], hg_p[None], hg_s[None],
            dil_p[0], dil_s[0], dil_p[1], dil_s[1], dil_p[2], dil_s[2],
            new[:Np].reshape(Bp, T, 4, NSA_KV_HEADS, HEAD_DIM)[None],
            new[Np:].reshape(Bs, Ts, 4, NSA_KV_HEADS, HEAD_DIM)[None],
            win_p[None], win_s[None])
```

```python
import functools
import math

import numpy as np
import jax
import jax.numpy as jnp
from jax import lax
from jax.experimental import pallas as pl
from jax.experimental.pallas import tpu as pltpu

F32 = jnp.float32
BF16 = jnp.bfloat16
I32 = jnp.int32

EPS = 1e-6
NEG = -1e30
ROPE_THETA = 10000.0
LANES = 128
VMEM_LIMIT = 56 * 1024 * 1024

HEAD_DIM = 128
RET_HEADS, RET_DK, RET_DV, RET_CHUNK = 4, 256, 512, 128
HG_HEADS, HG_DK, HG_CHUNK = 8, 128, 64
DIL_HEADS = 4
DIL_GROUPS = ((128, 1), (512, 4), (2048, 16))
DIL_QBLOCK = 128
NSA_HEADS, NSA_KV_HEADS, NSA_GROUP = 8, 2, 4
NSA_CMP_LEN, NSA_CMP_STRIDE = 32, 16
NSA_SLC_LEN, NSA_TOP_N, NSA_WINDOW = 64, 16, 512
PAGE = 128
N_EXPERTS = 8
MOE_ROWS = 512


def _cp(*sem):
    return pltpu.CompilerParams(dimension_semantics=sem, vmem_limit_bytes=VMEM_LIMIT)


def _tile(n, pref):
    t = pref
    while n % t:
        t //= 2
    return t


def _dot(a, b):
    return jnp.dot(a, b, preferred_element_type=F32)


def _dot_nt(a, b):
    return lax.dot_general(a, b, (((1,), (1,)), ((), ())), preferred_element_type=F32)


def _dot_tn(a, b):
    return lax.dot_general(a, b, (((0,), (0,)), ((), ())), preferred_element_type=F32)


def _split3(x):
    hi = x.astype(BF16)
    r = x - hi.astype(F32)
    mid = r.astype(BF16)
    lo = (r - mid.astype(F32)).astype(BF16)
    return hi, mid, lo


def _rms(x, g):
    return x * lax.rsqrt(jnp.mean(x * x, axis=-1, keepdims=True) + EPS) * g


def _sigmoid(x):
    return 1.0 / (1.0 + jnp.exp(-x))


def _silu(x):
    return x * _sigmoid(x)


def _norm_proj_kernel(x_ref, g_ref, w_ref, o_ref, xn_ref):
    @pl.when(pl.program_id(1) == 0)
    def _():
        xn_ref[...] = _rms(x_ref[...], g_ref[...]).astype(BF16)

    o_ref[...] = _dot(xn_ref[...], w_ref[...]).astype(o_ref.dtype)


def norm_proj(x, g, w, tn):
    M, D = x.shape
    N = w.shape[1]
    tm = _tile(M, 1024)
    return pl.pallas_call(
        _norm_proj_kernel,
        grid=(M // tm, N // tn),
        in_specs=[pl.BlockSpec((tm, D), lambda i, j: (i, 0)),
                  pl.BlockSpec((1, D), lambda i, j: (0, 0)),
                  pl.BlockSpec((D, tn), lambda i, j: (0, j))],
        out_specs=pl.BlockSpec((tm, tn), lambda i, j: (i, j)),
        out_shape=jax.ShapeDtypeStruct((M, N), F32),
        scratch_shapes=[pltpu.VMEM((tm, D), BF16)],
        compiler_params=_cp("parallel", "arbitrary"),
        name="norm_proj",
    )(x, g.reshape(1, D), w)


def _out_proj_kernel(a_ref, w_ref, x_ref, o_ref):
    o_ref[...] = x_ref[...] + _dot(a_ref[...].astype(BF16), w_ref[...])


def out_proj(a, w, x):
    M, K = a.shape
    D = w.shape[1]
    tm = _tile(M, 512)
    return pl.pallas_call(
        _out_proj_kernel,
        grid=(M // tm,),
        in_specs=[pl.BlockSpec((tm, K), lambda i: (i, 0)),
                  pl.BlockSpec((K, D), lambda i: (0, 0)),
                  pl.BlockSpec((tm, D), lambda i: (i, 0))],
        out_specs=pl.BlockSpec((tm, D), lambda i: (i, 0)),
        out_shape=jax.ShapeDtypeStruct((M, D), F32),
        compiler_params=_cp("parallel"),
        name="out_proj",
    )(a, w, x)


def _ffn_dense_kernel(x_ref, g_ref, wg_ref, wu_ref, wd_ref, o_ref, xn_ref, acc_ref, *, n_f):
    f = pl.program_id(1)

    @pl.when(f == 0)
    def _():
        xn_ref[...] = _rms(x_ref[...], g_ref[...]).astype(BF16)
        acc_ref[...] = jnp.zeros_like(acc_ref)

    xn = xn_ref[...]
    hg = _dot(xn, wg_ref[...])
    hu = _dot(xn, wu_ref[...])
    acc_ref[...] += _dot((_silu(hg) * hu).astype(BF16), wd_ref[...])

    @pl.when(f == n_f - 1)
    def _():
        o_ref[...] = x_ref[...] + acc_ref[...]


def ffn_dense(x, g, w_up, w_down):
    M, D = x.shape
    F = w_down.shape[0]
    tm = _tile(M, 512)
    tf = F // 2 if (F // 2) % LANES == 0 else F
    n_f = F // tf
    return pl.pallas_call(
        functools.partial(_ffn_dense_kernel, n_f=n_f),
        grid=(M // tm, n_f),
        in_specs=[pl.BlockSpec((tm, D), lambda i, f: (i, 0)),
                  pl.BlockSpec((1, D), lambda i, f: (0, 0)),
                  pl.BlockSpec((D, tf), lambda i, f: (0, f)),
                  pl.BlockSpec((D, tf), lambda i, f: (0, n_f + f)),
                  pl.BlockSpec((tf, D), lambda i, f: (f, 0))],
        out_specs=pl.BlockSpec((tm, D), lambda i, f: (i, 0)),
        out_shape=jax.ShapeDtypeStruct((M, D), F32),
        scratch_shapes=[pltpu.VMEM((tm, D), BF16), pltpu.VMEM((tm, D), F32)],
        compiler_params=_cp("parallel", "arbitrary"),
        name="ffn_dense",
    )(x, g.reshape(1, D), w_up, w_up, w_down)


def _ret_kernel(*refs, C, has_s0, has_prev):
    q_ref, k_ref, v_ref, g_ref, cos_ref, sin_ref, gng_ref, gnb_ref = refs[:8]
    n = 8
    s0_ref = None
    if has_s0:
        s0_ref = refs[n]
        n += 1
    if has_prev:
        n += 1
    y_ref, s_ref = refs[n], refs[n + 1]
    c = pl.program_id(1)

    @pl.when(c == 0)
    def _():
        if has_s0:
            s_ref[...] = s0_ref[...]
        else:
            s_ref[...] = jnp.zeros_like(s_ref)

    cos, sin = cos_ref[...], sin_ref[...]
    half = RET_DK // 2
    t = lax.broadcasted_iota(I32, (C, 1), 0).astype(F32)
    diff = (lax.broadcasted_iota(I32, (C, C), 0) - lax.broadcasted_iota(I32, (C, C), 1)).astype(F32)

    def rot(x):
        x1, x2 = x[:, :half], x[:, half:]
        return jnp.concatenate([x1 * cos - x2 * sin, x2 * cos + x1 * sin], axis=1)

    for h in range(RET_HEADS):
        lg = math.log(1.0 - 2.0 ** (-5.0 - h))
        qr = rot(q_ref[:, h * RET_DK:(h + 1) * RET_DK])
        kr = rot(k_ref[:, h * RET_DK:(h + 1) * RET_DK]) * RET_DK ** -0.5
        v = v_ref[:, h * RET_DV:(h + 1) * RET_DV].astype(BF16)
        decay = jnp.where(diff >= 0, jnp.exp(lg * jnp.maximum(diff, 0.0)), 0.0)
        sc = _dot_nt(qr.astype(BF16), kr.astype(BF16)) * decay
        S = s_ref[0, h]
        o = _dot(sc.astype(BF16), v) + _dot((qr * jnp.exp(lg * (t + 1.0))).astype(BF16), S.astype(BF16))
        kd = (kr * jnp.exp(lg * (C - 1.0 - t))).astype(BF16)
        s_ref[0, h] = math.exp(lg * C) * S + _dot_tn(kd, v)
        mu = jnp.mean(o, axis=-1, keepdims=True)
        var = jnp.mean(jnp.square(o - mu), axis=-1, keepdims=True)
        sl = slice(h * RET_DV, (h + 1) * RET_DV)
        on = (o - mu) * lax.rsqrt(var + EPS) * gng_ref[:, sl] + gnb_ref[:, sl]
        y_ref[:, sl] = _silu(g_ref[:, sl]) * on


def ret_core(proj, B, T, row0, cos, sin, tab0, gng, gnb, s0, y_prev):
    M = proj.shape[0]
    C = RET_CHUNK if T % RET_CHUNK == 0 else T
    nC = T // C
    r0, t0 = row0 // C, tab0 // C
    qkw, vw = RET_HEADS * RET_DK, RET_HEADS * RET_DV
    row = lambda b, c: r0 + b * nC + c
    in_specs = [pl.BlockSpec((C, qkw), lambda b, c: (row(b, c), 0)),
                pl.BlockSpec((C, qkw), lambda b, c: (row(b, c), 1)),
                pl.BlockSpec((C, vw), lambda b, c: (row(b, c), 2 * qkw // vw)),
                pl.BlockSpec((C, vw), lambda b, c: (row(b, c), 2 * qkw // vw + 1)),
                pl.BlockSpec((C, RET_DK // 2), lambda b, c: (t0 + c, 0)),
                pl.BlockSpec((C, RET_DK // 2), lambda b, c: (t0 + c, 0)),
                pl.BlockSpec((1, vw), lambda b, c: (0, 0)),
                pl.BlockSpec((1, vw), lambda b, c: (0, 0))]
    args = [proj, proj, proj, proj, cos, sin, gng.reshape(1, vw), gnb.reshape(1, vw)]
    sblock = (1, RET_HEADS, RET_DK, RET_DV)
    if s0 is not None:
        in_specs.append(pl.BlockSpec(sblock, lambda b, c: (b, 0, 0, 0)))
        args.append(s0)
    aliases = {}
    if y_prev is not None:
        in_specs.append(pl.BlockSpec(memory_space=pl.ANY))
        aliases = {len(args): 0}
        args.append(y_prev)
    return pl.pallas_call(
        functools.partial(_ret_kernel, C=C, has_s0=s0 is not None, has_prev=y_prev is not None),
        grid=(B, nC),
        in_specs=in_specs,
        out_specs=[pl.BlockSpec((C, vw), lambda b, c: (row(b, c), 0)),
                   pl.BlockSpec(sblock, lambda b, c: (b, 0, 0, 0))],
        out_shape=[jax.ShapeDtypeStruct((M, vw), F32),
                   jax.ShapeDtypeStruct((B,) + sblock[1:], F32)],
        input_output_aliases=aliases,
        compiler_params=_cp("parallel", "arbitrary"),
        name="ret_core",
    )(*args)


def _hgrn_levels(C):
    out, m = [], C // 2
    while m >= 1:
        out.append(m)
        m //= 2
    return out


def _hgrn_mats(C):
    t = np.arange(C)[:, None]
    u = np.arange(C)[None, :]
    mats = [(u <= t).astype(np.float32)]
    for m in _hgrn_levels(C):
        r = 2 * m * (t // (2 * m)) + m - 1
        a = ((u > r) & (u <= t)).astype(np.float32) - ((u > t) & (u <= r)).astype(np.float32)
        mats.append(a)
    return np.stack(mats)


def _hgrn_kernel(*refs, C, has_s0, has_prev):
    q_ref, f_ref, i_ref, g_ref, lb_ref, gng_ref, a_ref = refs[:7]
    n = 7
    s0_ref = None
    if has_s0:
        s0_ref = refs[n]
        n += 1
    if has_prev:
        n += 1
    y_ref, s_ref = refs[n], refs[n + 1]
    c = pl.program_id(1)

    @pl.when(c == 0)
    def _():
        if has_s0:
            s_ref[...] = s0_ref[...]
        else:
            s_ref[...] = jnp.zeros_like(s_ref)

    lb = lb_ref[...]
    f = lb + (1.0 - lb) * _sigmoid(f_ref[...])
    logf = jnp.log(f)
    kk = 1.0 - f
    q = _silu(q_ref[...])
    v = i_ref[...].astype(BF16)
    parts = _split3(logf)

    def amul(idx):
        a = a_ref[idx]
        return _dot(a, parts[0]) + _dot(a, parts[1]) + _dot(a, parts[2])

    bcum = amul(0)
    blast = bcum[C - 1:C, :]
    qb = (q * jnp.exp(bcum)).astype(BF16)
    kst = (kk * jnp.exp(blast - bcum)).astype(BF16)
    eb = jnp.exp(blast)

    ti = lax.broadcasted_iota(I32, (C, C), 0)
    si = lax.broadcasted_iota(I32, (C, C), 1)
    terms = [(q.astype(BF16), kk.astype(BF16), ti == si)]
    for li, m in enumerate(_hgrn_levels(C)):
        dq = amul(1 + li)
        qt = (q * jnp.exp(jnp.minimum(dq, 0.0))).astype(BF16)
        kt = (kk * jnp.exp(jnp.minimum(-dq, 0.0))).astype(BF16)
        blk = 2 * m
        mask = ((ti // blk) == (si // blk)) & ((ti % blk) >= m) & ((si % blk) < m)
        terms.append((qt, kt, mask))

    eye = (lax.broadcasted_iota(I32, (HG_DK, HG_DK), 0) == lax.broadcasted_iota(I32, (HG_DK, HG_DK), 1))
    for h in range(HG_HEADS):
        sl = slice(h * HG_DK, (h + 1) * HG_DK)
        sc = jnp.zeros((C, C), F32)
        for qt, kt, mask in terms:
            sc = sc + jnp.where(mask, _dot_nt(qt[:, sl], kt[:, sl]), 0.0)
        S = s_ref[0, h]
        vh = v[:, sl]
        o = _dot(sc.astype(BF16), vh) + _dot(qb[:, sl], S.astype(BF16))
        ecol = jnp.sum(jnp.where(eye, eb[:, sl], 0.0), axis=1, keepdims=True)
        s_ref[0, h] = ecol * S + _dot_tn(kst[:, sl], vh)
        on = o * lax.rsqrt(jnp.mean(o * o, axis=-1, keepdims=True) + EPS)
        y_ref[:, sl] = on * gng_ref[:, sl] * _sigmoid(g_ref[:, sl])


def hgrn_core(proj, B, T, row0, lb, gng, s0, y_prev):
    M = proj.shape[0]
    W = HG_HEADS * HG_DK
    C = HG_CHUNK if T % HG_CHUNK == 0 else T
    nC = T // C
    r0 = row0 // C
    row = lambda b, c: r0 + b * nC + c
    amats = jnp.asarray(_hgrn_mats(C), BF16)
    col = lambda j: pl.BlockSpec((C, W), lambda b, c: (row(b, c), j))
    in_specs = [col(j) for j in range(4)]
    in_specs += [pl.BlockSpec((1, W), lambda b, c: (0, 0)),
                 pl.BlockSpec((1, W), lambda b, c: (0, 0)),
                 pl.BlockSpec(amats.shape, lambda b, c: (0, 0, 0))]
    args = [proj, proj, proj, proj, lb.reshape(1, W), gng.reshape(1, W), amats]
    sblock = (1, HG_HEADS, HG_DK, HG_DK)
    if s0 is not None:
        in_specs.append(pl.BlockSpec(sblock, lambda b, c: (b, 0, 0, 0)))
        args.append(s0)
    aliases = {}
    if y_prev is not None:
        in_specs.append(pl.BlockSpec(memory_space=pl.ANY))
        aliases = {len(args): 0}
        args.append(y_prev)
    return pl.pallas_call(
        functools.partial(_hgrn_kernel, C=C, has_s0=s0 is not None, has_prev=y_prev is not None),
        grid=(B, nC),
        in_specs=in_specs,
        out_specs=[pl.BlockSpec((C, W), lambda b, c: (row(b, c), 0)),
                   pl.BlockSpec(sblock, lambda b, c: (b, 0, 0, 0))],
        out_shape=[jax.ShapeDtypeStruct((M, W), F32),
                   jax.ShapeDtypeStruct((B,) + sblock[1:], F32)],
        input_output_aliases=aliases,
        compiler_params=_cp("parallel", "arbitrary"),
        name="hgrn_core",
    )(*args)


def _prep_kernel(xin_ref, g_ref, w_ref, cos_ref, sin_ref, gain_ref, *o_refs, flags, outs, classes):
    cos, sin = cos_ref[...], sin_ref[...]
    tm = xin_ref.shape[0]
    c_refs = o_refs[len(outs):len(outs) + len(classes)]
    x_ref = o_refs[-1]
    stage = o_refs[-2] if classes else None
    x_ref[...] = _dot(_rms(xin_ref[...], g_ref[...]).astype(BF16), w_ref[...])
    for (lo, hi), o_ref in zip(outs, o_refs):
        for j in range(lo // LANES, hi // LANES):
            xj = x_ref[:, j * LANES:(j + 1) * LANES]
            if flags[j]:
                y = _rms(xj, gain_ref[j:j + 1, :])
                xj = y * cos + pltpu.roll(y, LANES // 2, 1) * sin
            o_ref[:, j * LANES - lo:(j + 1) * LANES - lo] = xj.astype(o_ref.dtype)
            if any(clo <= j * LANES < chi for clo, chi, _ in classes):
                stage[j] = xj
    for (lo, hi, d), c_ref in zip(classes, c_refs):
        for r in range(d):
            for j in range(lo // LANES, hi // LANES):
                c_ref[r, :, j * LANES - lo:(j + 1) * LANES - lo] = stage[j, pl.ds(r, tm // d, stride=d), :]


def proj_heads(x, g, w, cos, sin, gains, flags, outs, classes=()):
    M, D = x.shape
    W = w.shape[1]
    tm = _tile(M, 256)
    nb = W // LANES
    return pl.pallas_call(
        functools.partial(_prep_kernel, flags=tuple(flags), outs=tuple(outs), classes=tuple(classes)),
        grid=(M // tm,),
        in_specs=[pl.BlockSpec((tm, D), lambda i: (i, 0)),
                  pl.BlockSpec((1, D), lambda i: (0, 0)),
                  pl.BlockSpec((D, W), lambda i: (0, 0)),
                  pl.BlockSpec((tm, LANES), lambda i: (i, 0)),
                  pl.BlockSpec((tm, LANES), lambda i: (i, 0)),
                  pl.BlockSpec((nb, LANES), lambda i: (0, 0))],
        out_specs=([pl.BlockSpec((tm, hi - lo), lambda i: (i, 0)) for lo, hi in outs]
                   + [pl.BlockSpec((d, tm // d, hi - lo), lambda i: (0, i, 0)) for lo, hi, d in classes]),
        out_shape=([jax.ShapeDtypeStruct((M, hi - lo), F32) for lo, hi in outs]
                   + [jax.ShapeDtypeStruct((d, M // d, hi - lo), F32) for lo, hi, d in classes]),
        scratch_shapes=([pltpu.VMEM((nb, tm, LANES), F32)] if classes else []) + [pltpu.VMEM((tm, W), F32)],
        compiler_params=_cp("parallel"),
        name="proj_heads",
    )(x, g.reshape(1, D), w, cos, sin, gains)


def _softmax_head(s, valid, v):
    s = jnp.where(valid, s, NEG)
    m = jnp.max(s, axis=-1, keepdims=True)
    p = jnp.exp(s - m)
    l = jnp.sum(p, axis=-1, keepdims=True)
    o = _dot(p.astype(BF16), v) / l
    return o, m + jnp.log(l)


def _dil_prompt_kernel(q_ref, kc_ref, kp_ref, vc_ref, vp_ref, o_ref, l_ref, *, tq, reach):
    qi = pl.program_id(2)
    scale = HEAD_DIM ** -0.5
    qpos = qi * tq + lax.broadcasted_iota(I32, (tq, 2 * tq), 0)
    kpos = (qi - 1) * tq + lax.broadcasted_iota(I32, (tq, 2 * tq), 1)
    delta = qpos - kpos
    valid = (delta >= 0) & (delta <= reach) & (kpos >= 0)
    lane = lax.broadcasted_iota(I32, (tq, LANES), 1)
    lse_all = jnp.zeros((tq, LANES), F32)
    for h in range(DIL_HEADS):
        sl = slice(h * HEAD_DIM, (h + 1) * HEAD_DIM)
        q = q_ref[:, sl].astype(BF16)
        k = jnp.concatenate([kp_ref[:, sl], kc_ref[:, sl]], axis=0).astype(BF16)
        v = jnp.concatenate([vp_ref[:, sl], vc_ref[:, sl]], axis=0).astype(BF16)
        o, lse = _softmax_head(_dot_nt(q, k) * scale, valid, v)
        o_ref[h] = o
        lse_all = jnp.where(lane == h, lse, lse_all)
    l_ref[...] = lse_all


def dil_prompt(qkv, gi, col0, B, T):
    window, d = DIL_GROUPS[gi]
    HW = DIL_HEADS * HEAD_DIM
    Tc = T // d
    tq = min(DIL_QBLOCK, Tc)
    nq = Tc // tq

    def blk(off, prev):
        return pl.BlockSpec(
            (None, tq, HW),
            lambda b, r, qi: (r, b * nq + (jnp.maximum(qi - 1, 0) if prev else qi), col0 + off))

    return pl.pallas_call(
        functools.partial(_dil_prompt_kernel, tq=tq, reach=window // d),
        grid=(B, d, nq),
        in_specs=[blk(0, False), blk(1, False), blk(1, True), blk(2, False), blk(2, True)],
        out_specs=[pl.BlockSpec((None, DIL_HEADS, tq, HEAD_DIM), lambda b, r, qi: (r, 0, b * nq + qi, 0)),
                   pl.BlockSpec((None, tq, LANES), lambda b, r, qi: (r, b * nq + qi, 0))],
        out_shape=[jax.ShapeDtypeStruct((d, DIL_HEADS, B * Tc, HEAD_DIM), F32),
                   jax.ShapeDtypeStruct((d, B * Tc, LANES), F32)],
        compiler_params=_cp("parallel", "parallel", "arbitrary"),
        name=f"dil_prompt_{gi}",
    )(qkv, qkv, qkv, qkv, qkv)


def _dil_decode_kernel(q_ref, kn_ref, vn_ref, buf_ref, o_ref, l_ref, nb_ref, *, L, Ts, d, window):
    H = DIL_HEADS
    rpp = 2 * H
    scale = HEAD_DIM ** -0.5
    t = lax.broadcasted_iota(I32, (Ts, L + Ts), 0)
    i = lax.broadcasted_iota(I32, (Ts, L + Ts), 1)
    delta = L + t - i
    valid = (delta >= 0) & (delta % d == 0) & (delta <= window)
    lane = lax.broadcasted_iota(I32, (Ts, LANES), 1)
    lse_all = jnp.zeros((Ts, LANES), F32)
    for h in range(H):
        sl = slice(h * HEAD_DIM, (h + 1) * HEAD_DIM)
        q = q_ref[:, sl].astype(BF16)
        k = jnp.concatenate([buf_ref[0, pl.ds(h, L, stride=rpp), :], kn_ref[:, sl]], axis=0).astype(BF16)
        v = jnp.concatenate([buf_ref[0, pl.ds(H + h, L, stride=rpp), :], vn_ref[:, sl]], axis=0).astype(BF16)
        o, lse = _softmax_head(_dot_nt(q, k) * scale, valid, v)
        o_ref[h] = o
        lse_all = jnp.where(lane == h, lse, lse_all)
        nb_ref[0, pl.ds((L - Ts) * rpp + h, Ts, stride=rpp), :] = kn_ref[:, sl]
        nb_ref[0, pl.ds((L - Ts) * rpp + H + h, Ts, stride=rpp), :] = vn_ref[:, sl]
    l_ref[...] = lse_all
    nb_ref[0, 0:(L - Ts) * rpp, :] = buf_ref[0, Ts * rpp:L * rpp, :]


def dil_decode(pp, gi, B, Ts, row0, buf):
    window, d = DIL_GROUPS[gi]
    HW = DIL_HEADS * HEAD_DIM
    rows = buf.shape[1]
    L = rows // (2 * DIL_HEADS)
    r0 = row0 // Ts
    col = lambda off: pl.BlockSpec((Ts, HW), lambda b: (r0 + b, gi * 3 + off))
    return pl.pallas_call(
        functools.partial(_dil_decode_kernel, L=L, Ts=Ts, d=d, window=window),
        grid=(B,),
        in_specs=[col(0), col(1), col(2),
                  pl.BlockSpec((1, rows, HEAD_DIM), lambda b: (b, 0, 0))],
        out_specs=[pl.BlockSpec((None, DIL_HEADS, Ts, HEAD_DIM), lambda b: (0, 0, b, 0)),
                   pl.BlockSpec((None, Ts, LANES), lambda b: (0, b, 0)),
                   pl.BlockSpec((1, rows, HEAD_DIM), lambda b: (b, 0, 0))],
        out_shape=[jax.ShapeDtypeStruct((1, DIL_HEADS, B * Ts, HEAD_DIM), F32),
                   jax.ShapeDtypeStruct((1, B * Ts, LANES), F32),
                   jax.ShapeDtypeStruct((B, rows, HEAD_DIM), F32)],
        compiler_params=_cp("parallel"),
        name=f"dil_decode_{gi}",
    )(pp, pp, pp, buf)


def _dil_merge_kernel(*refs, ds):
    ng = len(ds)
    o_refs, l_refs, out_ref, stage = refs[:ng], refs[ng:2 * ng], refs[2 * ng], refs[2 * ng + 1]
    dmax = max(ds)
    n = out_ref.shape[0] // dmax
    for r in range(dmax):
        def rows(d):
            return r % d, pl.ds(r // d, n, stride=dmax // d)

        ls = [l_refs[g][rows(ds[g]) + (slice(None),)] for g in range(ng)]
        m = functools.reduce(jnp.maximum, ls)
        es = [jnp.exp(l - m) for l in ls]
        den = functools.reduce(lambda a, b: a + b, es)
        ws = [e / den for e in es]
        for h in range(DIL_HEADS):
            acc = None
            for g in range(ng):
                c, rs = rows(ds[g])
                term = ws[g][:, h:h + 1] * o_refs[g][c, h, rs, :]
                acc = term if acc is None else acc + term
            stage[h, pl.ds(r, n, stride=dmax), :] = acc
    for h in range(DIL_HEADS):
        out_ref[:, h * HEAD_DIM:(h + 1) * HEAD_DIM] = stage[h]


def dil_merge(os_, ls, rows):
    ds = tuple(o.shape[0] for o in os_)
    H, D = os_[0].shape[1], os_[0].shape[3]
    tm = _tile(rows, 256)
    ospec = lambda a: pl.BlockSpec((a.shape[0], H, tm // a.shape[0], D), lambda i: (0, 0, i, 0))
    lspec = lambda a: pl.BlockSpec((a.shape[0], tm // a.shape[0], a.shape[2]), lambda i: (0, i, 0))
    return pl.pallas_call(
        functools.partial(_dil_merge_kernel, ds=ds),
        grid=(rows // tm,),
        in_specs=[ospec(a) for a in os_] + [lspec(a) for a in ls],
        out_specs=pl.BlockSpec((tm, H * D), lambda i: (i, 0)),
        out_shape=jax.ShapeDtypeStruct((rows, H * D), F32),
        scratch_shapes=[pltpu.VMEM((H, tm, D), F32)],
        compiler_params=_cp("parallel"),
        name="dil_merge",
    )(*os_, *ls)


def _nsa_compress_kernel(tab_ref, pg_ref, pe_ref, w1_ref, w2_ref, o_ref, hs_ref, *, n_pages, ncmp):
    del tab_ref
    p = pl.program_id(1)
    hb = PAGE // NSA_CMP_STRIDE
    ncol = 2 * NSA_KV_HEADS
    rowlen = 4 * NSA_KV_HEADS * HEAD_DIM
    for c in range(ncol):
        for j in range(NSA_CMP_STRIDE):
            hs_ref[c, pl.ds(pl.multiple_of(p * hb, hb), hb), j * HEAD_DIM:(j + 1) * HEAD_DIM] = (
                pg_ref[0, :, j * rowlen + c * HEAD_DIM:j * rowlen + (c + 1) * HEAD_DIM])

    @pl.when(p == n_pages - 1)
    def _():
        half = NSA_CMP_STRIDE * HEAD_DIM
        R = n_pages * hb
        rowi = lax.broadcasted_iota(I32, (R, HEAD_DIM), 0)
        for c in range(ncol):
            s = c // NSA_KV_HEADS
            H = hs_ref[c]
            a = _dot((H + pe_ref[s, 0:1, :]).astype(BF16), w1_ref[s, 0:half, :])
            bm = _dot((H + pe_ref[s, 1:2, :]).astype(BF16), w1_ref[s, half:2 * half, :])
            pre = a + pltpu.roll(bm, R - 1, 0)
            out = _dot(_silu(pre).astype(BF16), w2_ref[s])
            o_ref[0, c] = jnp.where(rowi < ncmp, out, 0.0).astype(BF16)


def nsa_compress(pages, table, B, n_pages, pe, w1, w2):
    hb = PAGE // NSA_CMP_STRIDE
    R = n_pages * hb
    ncmp = (n_pages * PAGE - NSA_CMP_LEN) // NSA_CMP_STRIDE + 1
    half = NSA_CMP_STRIDE * HEAD_DIM
    pv = pages.reshape(pages.shape[0], hb, NSA_CMP_STRIDE * pages.shape[2])
    grid_spec = pltpu.PrefetchScalarGridSpec(
        num_scalar_prefetch=1,
        grid=(B, n_pages),
        in_specs=[pl.BlockSpec((1,) + pv.shape[1:], lambda b, p, tab: (tab[b * n_pages + p], 0, 0)),
                  pl.BlockSpec((2, 2, half), lambda b, p, tab: (0, 0, 0)),
                  pl.BlockSpec(w1.shape, lambda b, p, tab: (0, 0, 0)),
                  pl.BlockSpec(w2.shape, lambda b, p, tab: (0, 0, 0))],
        out_specs=pl.BlockSpec((1, 4, R, HEAD_DIM), lambda b, p, tab: (b, 0, 0, 0)),
        scratch_shapes=[pltpu.VMEM((4, R, half), F32)])
    return pl.pallas_call(
        functools.partial(_nsa_compress_kernel, n_pages=n_pages, ncmp=ncmp),
        grid_spec=grid_spec,
        out_shape=jax.ShapeDtypeStruct((B, 4, R, HEAD_DIM), BF16),
        compiler_params=_cp("parallel", "arbitrary"),
        name="nsa_compress",
    )(table, pv, pe.reshape(2, 2, half), w1, w2)


def _flash_update(m_ref, l_ref, a_ref, k, s, v_ones):
    D = HEAD_DIM
    m_old = m_ref[k]
    m_new = jnp.maximum(m_old, jnp.max(s, axis=-1, keepdims=True))
    m_use = jnp.maximum(m_new, -1e20)
    alpha = jnp.exp(m_old - m_use)
    pv = _dot(jnp.exp(s - m_use).astype(BF16), v_ones)
    l_ref[k] = alpha * l_ref[k] + pv[:, D:D + 1]
    a_ref[k] = alpha * a_ref[k] + pv[:, :D]
    m_ref[k] = m_new


def _nsa_qstack(q_ref, k, scale=None):
    G, D = NSA_GROUP, HEAD_DIM
    q = jnp.concatenate([q_ref[:, (k * G + g) * D:(k * G + g + 1) * D] for g in range(G)], axis=0)
    return (q if scale is None else q * scale).astype(BF16)


def _nsa_cmp_select(qs, kc, vc, c2s, pos_r, pos_q, n_slc, ncmp):
    R, tq = qs.shape[0], pos_q.shape[0]
    nidx = lax.broadcasted_iota(I32, (R, kc.shape[0]), 1)
    s = _dot_nt(qs, kc) * HEAD_DIM ** -0.5
    valid = (NSA_CMP_STRIDE * nidx + NSA_CMP_LEN - 1 <= pos_r) & (nidx < ncmp)
    s = jnp.where(valid, s, NEG)
    pr = jnp.where(valid, jnp.exp(s - jnp.max(s, axis=-1, keepdims=True)), 0.0)
    pc = pr / jnp.maximum(jnp.sum(pr, axis=-1, keepdims=True), 1e-30)
    o_cmp = _dot(pc.astype(BF16), vc)
    psum = pc[0:tq]
    for g in range(1, NSA_GROUP):
        psum = psum + pc[g * tq:(g + 1) * tq]
    ph, pm, plo = _split3(psum)
    imp = _dot(ph, c2s) + _dot(pm, c2s) + _dot(plo, c2s)
    j = lax.broadcasted_iota(I32, (tq, LANES), 1)
    cur = pos_q // NSA_SLC_LEN
    forced = (j == 0) | (j == cur) | (j == cur - 1)
    score = jnp.where(forced, 1e4, jnp.where(j <= cur, imp, -1.0))
    score = jnp.where(j < n_slc, score, -2.0)
    cnt = jnp.zeros((tq, LANES), F32)
    for jp in range(n_slc):
        col = score[:, jp:jp + 1]
        cnt = cnt + ((col > score) | ((col == score) & (jp < j))).astype(F32)
    return o_cmp, (cnt < float(min(NSA_TOP_N, n_slc))).astype(F32)


def _nsa_attn_kernel(qi_ref, kb_ref, q_ref, gate_ref, kc_ref, c2s_ref, blk_ref, kv_ref, wkv_ref, o_ref,
                     ocmp, qa, m_s, l_s, a_s, m_w, l_w, a_w, *, tq, KB, n_slc, ncmp):
    step = pl.program_id(1)
    qi, kb = qi_ref[step], kb_ref[step]
    G, KVH, D = NSA_GROUP, NSA_KV_HEADS, HEAD_DIM
    R = G * tq
    pos_r = qi * tq + lax.broadcasted_iota(I32, (R, 1), 0) % tq
    pos_q = qi * tq + lax.broadcasted_iota(I32, (tq, 1), 0)
    kb_last = (qi * tq + tq - 1) // KB

    @pl.when(kb == 0)
    def _():
        for k in range(KVH):
            ocmp[k], sel = _nsa_cmp_select(_nsa_qstack(q_ref, k), kc_ref[0, k], kc_ref[0, KVH + k], c2s_ref[...],
                                           pos_r, pos_q, n_slc, ncmp)
            bias = ((sel - 1.0) * -NEG).astype(BF16)
            qa[k] = jnp.concatenate([_nsa_qstack(q_ref, k, D ** -0.5), jnp.concatenate([bias] * G, axis=0)], axis=1)
        for ref in (m_s, m_w):
            ref[...] = jnp.full(ref.shape, NEG, F32)
        for ref in (l_s, a_s, l_w, a_w):
            ref[...] = jnp.zeros(ref.shape, F32)

    dist = pos_r - (kb * KB + lax.broadcasted_iota(I32, (R, KB), 1))
    ones = jnp.ones((KB, D), BF16)
    for k in range(KVH):
        ks = jnp.concatenate([kv_ref[:, k * D:(k + 1) * D].astype(BF16), blk_ref[...]], axis=1)
        vs = jnp.concatenate([kv_ref[:, (KVH + k) * D:(KVH + k + 1) * D].astype(BF16), ones], axis=1)
        s = _dot_nt(qa[k], ks)

        @pl.when(kb == kb_last)
        def _():
            _flash_update(m_s, l_s, a_s, k, jnp.where(dist >= 0, s, NEG), vs)

        @pl.when(kb != kb_last)
        def _():
            _flash_update(m_s, l_s, a_s, k, s, vs)

    @pl.when(kb >= kb_last - NSA_WINDOW // KB)
    def _():
        valid = (dist >= 0) & (dist <= NSA_WINDOW)
        for k in range(KVH):
            kw = wkv_ref[:, k * D:(k + 1) * D].astype(BF16)
            vw = jnp.concatenate([wkv_ref[:, (KVH + k) * D:(KVH + k + 1) * D].astype(BF16), ones], axis=1)
            _flash_update(m_w, l_w, a_w, k, jnp.where(valid, _dot_nt(qa[k, :, 0:D], kw), NEG), vw)

    @pl.when(kb == kb_last)
    def _():
        gs = _sigmoid(gate_ref[...])
        for k in range(KVH):
            _nsa_gated_out(o_ref, gs, k, tq, ocmp[k], a_s[k] / jnp.maximum(l_s[k], 1e-30),
                           a_w[k] / jnp.maximum(l_w[k], 1e-30))


def _nsa_gated_out(o_ref, gs, k, tq, o_c, o_s, o_w):
    for g in range(NSA_GROUP):
        hd = k * NSA_GROUP + g
        rows = slice(g * tq, (g + 1) * tq)
        o = (gs[:, 3 * hd:3 * hd + 1] * o_c[rows] + gs[:, 3 * hd + 1:3 * hd + 2] * o_s[rows]
             + gs[:, 3 * hd + 2:3 * hd + 3] * o_w[rows])
        o_ref[:, hd * HEAD_DIM:(hd + 1) * HEAD_DIM] = o


def _softmax_two(s1, ok1, v1, s2, ok2, v2):
    s1 = jnp.where(ok1, s1, NEG)
    s2 = jnp.where(ok2, s2, NEG)
    m = jnp.maximum(jnp.max(s1, axis=-1, keepdims=True), jnp.max(s2, axis=-1, keepdims=True))
    p1 = jnp.where(ok1, jnp.exp(s1 - m), 0.0)
    p2 = jnp.where(ok2, jnp.exp(s2 - m), 0.0)
    l = jnp.sum(p1, axis=-1, keepdims=True) + jnp.sum(p2, axis=-1, keepdims=True)
    return (_dot(p1.astype(BF16), v1) + _dot(p2.astype(BF16), v2)) / jnp.maximum(l, 1e-30)


def _nsa_sample_kernel(*refs, Ts, n_pages, Lw, n_slc, ncmp, past):
    (tab_ref, q_ref, gate_ref, xn_ref, xw_ref, wb_ref, pe_ref, w1_ref, w2_ref, c2s_ref, exp_ref) = refs[:11]
    pg_refs = refs[11:11 + n_pages]
    o_ref, nwb_ref, hs_ref = refs[12 + n_pages:]
    del tab_ref
    G, KVH, D = NSA_GROUP, NSA_KV_HEADS, HEAD_DIM
    R = G * Ts
    scale = D ** -0.5
    rpt = 4 * KVH
    hb = PAGE // NSA_CMP_STRIDE
    half = NSA_CMP_STRIDE * D
    pos_r = past + lax.broadcasted_iota(I32, (R, 1), 0) % Ts
    pos_q = past + lax.broadcasted_iota(I32, (Ts, 1), 0)

    def tile_rows(x):
        return jnp.concatenate([x] * G, axis=0)

    for u in range(n_pages):
        for c in range(2 * KVH):
            for j in range(NSA_CMP_STRIDE):
                hs_ref[c, u * hb:(u + 1) * hb, j * D:(j + 1) * D] = (
                    pg_refs[u][0, pl.ds(j * rpt + c, hb, stride=NSA_CMP_STRIDE * rpt), :])
    nrow = n_pages * hb
    rowi = lax.broadcasted_iota(I32, (nrow, D), 0)
    cmp = []
    for c in range(2 * KVH):
        s = c // KVH
        H = hs_ref[c]
        a = _dot((H + pe_ref[s, 0:1, :]).astype(BF16), w1_ref[s, 0:half, :])
        bm = _dot((H + pe_ref[s, 1:2, :]).astype(BF16), w1_ref[s, half:2 * half, :])
        out = _dot(_silu(a + pltpu.roll(bm, nrow - 1, 0)).astype(BF16), w2_ref[s])
        cmp.append(jnp.where(rowi < ncmp, out, 0.0).astype(BF16))

    gs = _sigmoid(gate_ref[...])
    jn = lax.broadcasted_iota(I32, (R, Ts), 1)
    causal = past + jn <= pos_r
    wpos = past - Lw + lax.broadcasted_iota(I32, (R, Lw), 1)
    wdist = pos_r - wpos
    wvalid = (wdist >= 0) & (wdist <= NSA_WINDOW)
    blk = past // NSA_SLC_LEN
    for k in range(KVH):
        qs = _nsa_qstack(q_ref, k)
        o_c, selk = _nsa_cmp_select(qs, cmp[k], cmp[KVH + k], c2s_ref[...], pos_r, pos_q, n_slc, ncmp)
        ks = jnp.concatenate([pg[0, pl.ds(2 * KVH + k, PAGE, stride=rpt), :] for pg in pg_refs],
                             axis=0).astype(BF16)
        vs = jnp.concatenate([pg[0, pl.ds(3 * KVH + k, PAGE, stride=rpt), :] for pg in pg_refs],
                             axis=0).astype(BF16)
        ok_past = tile_rows(_dot(selk.astype(BF16), exp_ref[...])) > 0.5
        ok_new = (tile_rows(selk[:, blk:blk + 1]) > 0.5) & causal
        kn = xn_ref[:, (2 * KVH + k) * D:(2 * KVH + k + 1) * D].astype(BF16)
        vn = xn_ref[:, (3 * KVH + k) * D:(3 * KVH + k + 1) * D].astype(BF16)
        o_s = _softmax_two(_dot_nt(qs, ks) * scale, ok_past, vs, _dot_nt(qs, kn) * scale, ok_new, vn)
        kw = wb_ref[0, pl.ds(k, Lw, stride=2 * KVH), :].astype(BF16)
        vw = wb_ref[0, pl.ds(KVH + k, Lw, stride=2 * KVH), :].astype(BF16)
        kwn = xw_ref[:, k * D:(k + 1) * D]
        vwn = xw_ref[:, (KVH + k) * D:(KVH + k + 1) * D]
        o_w = _softmax_two(_dot_nt(qs, kw) * scale, wvalid, vw,
                           _dot_nt(qs, kwn.astype(BF16)) * scale, causal, vwn.astype(BF16))
        _nsa_gated_out(o_ref, gs, k, Ts, o_c, o_s, o_w)
        nwb_ref[0, pl.ds((Lw - Ts) * 2 * KVH + k, Ts, stride=2 * KVH), :] = kwn
        nwb_ref[0, pl.ds((Lw - Ts) * 2 * KVH + KVH + k, Ts, stride=2 * KVH), :] = vwn
    nwb_ref[0, 0:(Lw - Ts) * 2 * KVH, :] = wb_ref[0, Ts * 2 * KVH:Lw * 2 * KVH, :]


def nsa_sample(q, gate, xnew, xwin, pool, table, wbuf, pe, w1, w2, B, Ts, row0, past, o_prev):
    M, QW = q.shape
    D = HEAD_DIM
    n_pages = past // PAGE
    assert past % PAGE == 0 and Ts < NSA_CMP_STRIDE
    Lw = wbuf.shape[1] // (2 * NSA_KV_HEADS)
    L = past + Ts
    n_slc = -(-L // NSA_SLC_LEN)
    ncmp = (L - NSA_CMP_LEN) // NSA_CMP_STRIDE + 1
    hb = PAGE // NSA_CMP_STRIDE
    half = NSA_CMP_STRIDE * D
    c2s = _cmp_to_slc(n_pages * hb, ncmp, n_slc)
    expand = jnp.asarray(np.arange(LANES)[:, None] == np.arange(past)[None, :] // NSA_SLC_LEN, BF16)
    r0 = row0 // Ts
    rows = lambda w: pl.BlockSpec((Ts, w), lambda b, tab: (r0 + b, 0))
    full = lambda a: pl.BlockSpec(a.shape, lambda b, tab: (0,) * a.ndim)
    page = lambda u: pl.BlockSpec((1,) + pool.shape[1:], lambda b, tab: (tab[b * n_pages + u], 0, 0))
    pe2 = pe.reshape(2, 2, half)
    in_specs = ([rows(QW), rows(LANES), rows(xnew.shape[1]), rows(xwin.shape[1]),
                 pl.BlockSpec((1,) + wbuf.shape[1:], lambda b, tab: (b, 0, 0)),
                 full(pe2), full(w1), full(w2), full(c2s), full(expand)]
                + [page(u) for u in range(n_pages)] + [pl.BlockSpec(memory_space=pl.ANY)])
    grid_spec = pltpu.PrefetchScalarGridSpec(
        num_scalar_prefetch=1,
        grid=(B,),
        in_specs=in_specs,
        out_specs=[rows(QW), pl.BlockSpec((1,) + wbuf.shape[1:], lambda b, tab: (b, 0, 0))],
        scratch_shapes=[pltpu.VMEM((2 * NSA_KV_HEADS, n_pages * hb, half), F32)])
    return pl.pallas_call(
        functools.partial(_nsa_sample_kernel, Ts=Ts, n_pages=n_pages, Lw=Lw, n_slc=n_slc, ncmp=ncmp, past=past),
        grid_spec=grid_spec,
        out_shape=[jax.ShapeDtypeStruct((M, QW), F32), jax.ShapeDtypeStruct(wbuf.shape, F32)],
        input_output_aliases={11 + n_pages: 0},
        compiler_params=_cp("parallel"),
        name="nsa_sample",
    )(table, q, gate, xnew, xwin, wbuf, pe2, w1, w2, c2s, expand, *([pool] * n_pages), o_prev)


def _cmp_to_slc(nrows, ncmp, n_slc):
    i = np.arange(nrows)[:, None] * NSA_CMP_STRIDE
    j = np.arange(LANES)[None, :] * NSA_SLC_LEN
    ov = np.clip(np.minimum(i + NSA_CMP_LEN, j + NSA_SLC_LEN) - np.maximum(i, j), 0, None) / NSA_CMP_LEN
    ov = np.where((np.arange(nrows)[:, None] < ncmp) & (np.arange(LANES)[None, :] < n_slc), ov, 0.0)
    return jnp.asarray(ov, BF16)


def nsa_attn(q, gate, kc, new, win, B, T):
    M = q.shape[0]
    D = HEAD_DIM
    QW = NSA_HEADS * D
    KW = 2 * NSA_KV_HEADS * D
    tq = min(PAGE, T)
    KB = min(NSA_WINDOW, T)
    nqb, nkb = T // tq, T // KB
    n_slc = -(-T // NSA_SLC_LEN)
    ncmp = (T - NSA_CMP_LEN) // NSA_CMP_STRIDE + 1
    R = NSA_GROUP * tq
    c2s = _cmp_to_slc(kc.shape[2], ncmp, n_slc)
    blocks = jnp.asarray(np.arange(T)[:, None] // NSA_SLC_LEN == np.arange(LANES)[None, :], BF16)
    pairs = [(qi, kb) for qi in range(nqb) for kb in range((qi * tq + tq - 1) // KB + 1)]
    qi_tab = jnp.asarray([p[0] for p in pairs], I32)
    kb_tab = jnp.asarray([p[1] for p in pairs], I32)
    qrow = lambda b, s, qt, kt: (b * nqb + qt[s], 0)
    in_specs = [pl.BlockSpec((tq, QW), qrow),
                pl.BlockSpec((tq, LANES), qrow),
                pl.BlockSpec((1,) + kc.shape[1:], lambda b, s, qt, kt: (b, 0, 0, 0)),
                pl.BlockSpec(c2s.shape, lambda b, s, qt, kt: (0, 0)),
                pl.BlockSpec((KB, LANES), lambda b, s, qt, kt: (kt[s], 0)),
                pl.BlockSpec((KB, KW), lambda b, s, qt, kt: (b * nkb + kt[s], 1)),
                pl.BlockSpec((KB, KW), lambda b, s, qt, kt: (b * nkb + kt[s], 0))]
    vec = lambda: pltpu.VMEM((NSA_KV_HEADS, R, 1), F32)
    mat = lambda: pltpu.VMEM((NSA_KV_HEADS, R, D), F32)
    grid_spec = pltpu.PrefetchScalarGridSpec(
        num_scalar_prefetch=2,
        grid=(B, len(pairs)),
        in_specs=in_specs,
        out_specs=pl.BlockSpec((tq, QW), qrow),
        scratch_shapes=[mat(), pltpu.VMEM((NSA_KV_HEADS, R, 2 * D), BF16),
                        vec(), vec(), mat(), vec(), vec(), mat()])
    return pl.pallas_call(
        functools.partial(_nsa_attn_kernel, tq=tq, KB=KB, n_slc=n_slc, ncmp=ncmp),
        grid_spec=grid_spec,
        out_shape=jax.ShapeDtypeStruct((M, QW), F32),
        compiler_params=_cp("parallel", "arbitrary"),
        name="nsa_attn_prompt",
    )(qi_tab, kb_tab, q, gate, kc, c2s, blocks, new, win)


def _router_kernel(x_ref, g_ref, wr_ref, h_ref, e_ref, p_ref):
    xn = _rms(x_ref[...], g_ref[...])
    h_ref[...] = xn
    xh, xm, xl = _split3(xn)
    wh, wm, wl = _split3(wr_ref[...])
    logits = (_dot(xh, wh) + _dot(xh, wm) + _dot(xm, wh)
              + _dot(xh, wl) + _dot(xl, wh) + _dot(xm, wm))
    lane = lax.broadcasted_iota(I32, logits.shape, 1)
    lg = jnp.where(lane < N_EXPERTS, logits, -jnp.inf)
    m1 = jnp.max(lg, axis=-1, keepdims=True)
    i1 = jnp.min(jnp.where(lg == m1, lane, LANES), axis=-1, keepdims=True)
    lg2 = jnp.where(lane == i1, -jnp.inf, lg)
    m2 = jnp.max(lg2, axis=-1, keepdims=True)
    i2 = jnp.min(jnp.where(lg2 == m2, lane, LANES), axis=-1, keepdims=True)
    e = jnp.exp(m2 - m1)
    e_ref[...] = jnp.where(lane == 0, i1, jnp.where(lane == 1, i2, 0))
    p_ref[...] = jnp.where(lane == 0, 1.0 / (1.0 + e), jnp.where(lane == 1, e / (1.0 + e), 0.0))


def moe_router(x, g, router):
    M, D = x.shape
    tm = _tile(M, 512)
    wr = jnp.zeros((D, LANES), F32).at[:, :N_EXPERTS].set(router)
    return pl.pallas_call(
        _router_kernel,
        grid=(M // tm,),
        in_specs=[pl.BlockSpec((tm, D), lambda i: (i, 0)),
                  pl.BlockSpec((1, D), lambda i: (0, 0)),
                  pl.BlockSpec((D, LANES), lambda i: (0, 0))],
        out_specs=[pl.BlockSpec((tm, D), lambda i: (i, 0)),
                   pl.BlockSpec((tm, LANES), lambda i: (i, 0)),
                   pl.BlockSpec((tm, LANES), lambda i: (i, 0))],
        out_shape=[jax.ShapeDtypeStruct((M, D), F32),
                   jax.ShapeDtypeStruct((M, LANES), I32),
                   jax.ShapeDtypeStruct((M, LANES), F32)],
        compiler_params=_cp("parallel"),
        name="moe_router",
    )(x, g.reshape(1, D), wr)


def _row_copy(src_ref, dst_ref, sem, src_row, dst_row):
    return pltpu.make_async_copy(src_ref.at[pl.ds(src_row, 1)], dst_ref.at[pl.ds(dst_row, 1)], sem)


def _moe_ffn_kernel(be_ref, rt_ref, nu_ref, h_ref, wg_ref, wu_ref, wd_ref, y_ref,
                    xbuf, xb16, acc, sems, *, tm, n_f):
    del be_ref
    r, f = pl.program_id(0), pl.program_id(1)
    n_used = nu_ref[0]
    used = r < n_used
    slot = r % 2

    def gather(block, slot_):
        def issue(i, carry):
            _row_copy(h_ref, xbuf.at[slot_], sems.at[slot_], rt_ref[block * tm + i], i).start()
            return carry

        lax.fori_loop(0, tm, issue, 0, unroll=8)

    @pl.when(used & (f == 0) & (r == 0))
    def _():
        gather(0, 0)

    @pl.when(used & (f == 0))
    def _():
        def drain(i, carry):
            _row_copy(h_ref, xbuf.at[slot], sems.at[slot], 0, i).wait()
            return carry

        lax.fori_loop(0, tm, drain, 0, unroll=8)
        xb16[...] = xbuf[slot].astype(BF16)
        acc[...] = jnp.zeros_like(acc)

    @pl.when((r + 1 < n_used) & (f == 0))
    def _():
        gather(r + 1, 1 - slot)

    @pl.when(used)
    def _():
        x = xb16[...]
        hg = _dot(x, wg_ref[0])
        hu = _dot(x, wu_ref[0])
        acc[...] += _dot((_silu(hg) * hu).astype(BF16), wd_ref[0])

    @pl.when(used & (f == n_f - 1))
    def _():
        y_ref[...] = acc[...]

    @pl.when(jnp.logical_not(used) & (f == n_f - 1))
    def _():
        y_ref[...] = jnp.zeros_like(y_ref)


def moe_ffn(h, block_expert, row_tok, n_used, w_up, w_down, layer):
    D = h.shape[1]
    F = w_down.shape[2]
    tm = MOE_ROWS
    nb = block_expert.shape[0]
    tf = F // 2 if (F // 2) % LANES == 0 else F
    n_f = F // tf
    live = lambda r, f, nu: jnp.where(r < nu[0], f, 0)
    grid_spec = pltpu.PrefetchScalarGridSpec(
        num_scalar_prefetch=3,
        grid=(nb, n_f),
        in_specs=[pl.BlockSpec(memory_space=pl.ANY),
                  pl.BlockSpec((None, 1, D, tf), lambda r, f, be, rt, nu: (layer, be[r], 0, live(r, f, nu))),
                  pl.BlockSpec((None, 1, D, tf), lambda r, f, be, rt, nu: (layer, be[r], 0, n_f + live(r, f, nu))),
                  pl.BlockSpec((None, 1, tf, D), lambda r, f, be, rt, nu: (layer, be[r], live(r, f, nu), 0))],
        out_specs=pl.BlockSpec((tm, D), lambda r, f, be, rt, nu: (r, 0)),
        scratch_shapes=[pltpu.VMEM((2, tm, D), F32), pltpu.VMEM((tm, D), BF16), pltpu.VMEM((tm, D), F32),
                        pltpu.SemaphoreType.DMA((2,))])
    return pl.pallas_call(
        functools.partial(_moe_ffn_kernel, tm=tm, n_f=n_f),
        grid_spec=grid_spec,
        out_shape=jax.ShapeDtypeStruct((nb * tm, D), F32),
        compiler_params=_cp("arbitrary", "arbitrary"),
        name="moe_ffn",
    )(block_expert, row_tok, n_used, h, w_up, w_up, w_down)


def _moe_combine_kernel(p0_ref, p1_ref, x_ref, g_ref, y_ref, o_ref, ybuf, sem, *, tm):
    i = pl.program_id(0)

    def issue(t, carry):
        _row_copy(y_ref, ybuf.at[0], sem, p0_ref[i * tm + t], t).start()
        _row_copy(y_ref, ybuf.at[1], sem, p1_ref[i * tm + t], t).start()
        return carry

    lax.fori_loop(0, tm, issue, 0, unroll=8)

    def drain(t, carry):
        _row_copy(y_ref, ybuf.at[0], sem, 0, t).wait()
        _row_copy(y_ref, ybuf.at[1], sem, 0, t).wait()
        return carry

    lax.fori_loop(0, tm, drain, 0, unroll=8)
    g = g_ref[...]
    o_ref[...] = x_ref[...] + (g[:, 0:1] * ybuf[0] + g[:, 1:2] * ybuf[1])


def moe_combine(x, gates, y, pos0, pos1):
    M, D = x.shape
    tm = _tile(M, 256)
    grid_spec = pltpu.PrefetchScalarGridSpec(
        num_scalar_prefetch=2,
        grid=(M // tm,),
        in_specs=[pl.BlockSpec((tm, D), lambda i, a, b: (i, 0)),
                  pl.BlockSpec((tm, LANES), lambda i, a, b: (i, 0)),
                  pl.BlockSpec(memory_space=pl.ANY)],
        out_specs=pl.BlockSpec((tm, D), lambda i, a, b: (i, 0)),
        scratch_shapes=[pltpu.VMEM((2, tm, D), F32), pltpu.SemaphoreType.DMA(())])
    return pl.pallas_call(
        functools.partial(_moe_combine_kernel, tm=tm),
        grid_spec=grid_spec,
        out_shape=jax.ShapeDtypeStruct((M, D), F32),
        compiler_params=_cp("arbitrary"),
        name="moe_combine",
    )(pos0, pos1, x, gates, y)


def moe_layer(x, g, router, w_up, w_down, layer):
    M = x.shape[0]
    h, eidx, gates = moe_router(x, g, router)
    A = 2 * M
    e = eidx[:, :2].reshape(A)
    onehot = (e[:, None] == jnp.arange(N_EXPERTS, dtype=I32)[None, :]).astype(I32)
    csum = jnp.cumsum(onehot, axis=0)
    rank = jnp.sum(csum * onehot, axis=1) - 1
    counts = csum[-1]
    padded = (counts + MOE_ROWS - 1) // MOE_ROWS * MOE_ROWS
    pend = jnp.cumsum(padded)
    dest = ((pend - padded)[e] + rank).astype(I32)
    nb = (A + N_EXPERTS * (MOE_ROWS - 1) + MOE_ROWS - 1) // MOE_ROWS
    row_tok = jnp.zeros((nb * MOE_ROWS,), I32).at[dest].set(jnp.arange(A, dtype=I32) // 2)
    block_expert = jnp.minimum(
        jnp.searchsorted(pend, jnp.arange(nb, dtype=I32) * MOE_ROWS, side='right'), N_EXPERTS - 1).astype(I32)
    n_used = (pend[-1:] // MOE_ROWS).astype(I32)
    y = moe_ffn(h, block_expert, row_tok, n_used, w_up, w_down, layer)
    dest2 = dest.reshape(M, 2)
    return moe_combine(x, gates, y, dest2[:, 0], dest2[:, 1])


def _rope_tables(pos, half):
    inv = ROPE_THETA ** (-jnp.arange(half, dtype=F32) / half)
    ang = pos.astype(F32)[:, None] * inv[None, :]
    return jnp.cos(ang), jnp.sin(ang)


def _row_rope_tables(Bp, T, Bs, Ts, past):
    pos = jnp.concatenate([jnp.tile(jnp.arange(T, dtype=I32), Bp),
                           jnp.tile(past + jnp.arange(Ts, dtype=I32), Bs)])
    cos, sin = _rope_tables(pos, HEAD_DIM // 2)
    return jnp.concatenate([cos, cos], axis=1), jnp.concatenate([-sin, sin], axis=1)


def kernel(x_prompt, x_sample, state_ret, state_hgrn, cache_dil_w128, cache_dil_w512, cache_dil_w2048,
           cache_nsa_kv, cache_nsa_win, page_table, norm_mix, norm_ffn,
           ret_w_in, ret_gn_g, ret_gn_b, ret_w_out,
           hg_w_in, hg_lower_bounds, hg_gn_g, hg_w_out,
           dil_w_in, dil_qn_g, dil_kn_g, dil_w_out,
           nsa_w_in, nsa_w_gate, nsa_qn_g, nsa_kn_g, nsa_cmp_pe, nsa_cmp_w1, nsa_cmp_w2, nsa_w_out,
           ffn_w_up, ffn_w_down, moe_router_w, moe_w_up, moe_w_down):
    Bp, T, D = x_prompt.shape
    Bs, Ts, _ = x_sample.shape
    Np, Ns = Bp * T, Bs * Ts
    M = Np + Ns
    n_pages = page_table.shape[1]
    past = n_pages * PAGE
    bf = lambda w: w.astype(BF16)

    x = jnp.concatenate([x_prompt.reshape(Np, D), x_sample.reshape(Ns, D)], axis=0)
    lb_all = jnp.cumsum(jax.nn.softmax(hg_lower_bounds.astype(F32), axis=0), axis=0)
    lb_all = lb_all - lb_all[0:1]

    proj = norm_proj(x, norm_mix[0], bf(ret_w_in[0]), 1024)
    cos_r, sin_r = _rope_tables(jnp.arange(max(T, past + Ts), dtype=I32), RET_DK // 2)
    y, ret_p = ret_core(proj, Bp, T, 0, cos_r, sin_r, 0, ret_gn_g[0], ret_gn_b[0], None, None)
    y, ret_s = ret_core(proj, Bs, Ts, Np, cos_r, sin_r, past, ret_gn_g[0], ret_gn_b[0], state_ret[0], y)
    x = out_proj(y, bf(ret_w_out[0]), x)
    x = ffn_dense(x, norm_ffn[0], bf(ffn_w_up[0]), bf(ffn_w_down[0]))

    proj = norm_proj(x, norm_mix[1], bf(hg_w_in[0]), 1024)
    y, hg_p = hgrn_core(proj, Bp, T, 0, lb_all[1], hg_gn_g[0], None, None)
    y, hg_s = hgrn_core(proj, Bs, Ts, Np, lb_all[1], hg_gn_g[0], state_hgrn[0], y)
    x = out_proj(y, bf(hg_w_out[0]), x)
    moe_up, moe_down = bf(moe_w_up), bf(moe_w_down)
    x = moe_layer(x, norm_ffn[1], moe_router_w[0], moe_up, moe_down, 0)

    cos_h, sin_h = _row_rope_tables(Bp, T, Bs, Ts, past)
    HW = DIL_HEADS * HEAD_DIM
    dil_in_w = dil_w_in.shape[2]
    ones = jnp.ones((DIL_HEADS, HEAD_DIM), F32)
    gains = jnp.concatenate([jnp.concatenate([ones * dil_qn_g[0, gi], ones * dil_kn_g[0, gi], ones], axis=0)
                             for gi in range(len(DIL_GROUPS))], axis=0)
    flags = ([True] * (2 * DIL_HEADS) + [False] * DIL_HEADS) * len(DIL_GROUPS)
    dilated = [(gi, d) for gi, (_, d) in enumerate(DIL_GROUPS) if d > 1]
    pp, *by_class = proj_heads(x, norm_mix[2], bf(dil_w_in[0]), cos_h, sin_h, gains, flags, [(0, dil_in_w)],
                               [(gi * 3 * HW, (gi + 1) * 3 * HW, d) for gi, d in dilated])
    by_class = dict(zip([gi for gi, _ in dilated], by_class))
    caches = (cache_dil_w128, cache_dil_w512, cache_dil_w2048)
    op, lp, od, ld, dil_p, dil_s = [], [], [], [], [], []
    for gi, (window, _) in enumerate(DIL_GROUPS):
        if gi in by_class:
            o_g, l_g = dil_prompt(by_class[gi], gi, 0, Bp, T)
        else:
            o_g, l_g = dil_prompt(pp.reshape(1, M, dil_in_w), gi, gi * 3, Bp, T)
        op.append(o_g)
        lp.append(l_g)
        buf = caches[gi][0]
        o_g, l_g, nbuf = dil_decode(pp, gi, Bs, Ts, Np, buf.reshape(Bs, -1, HEAD_DIM))
        od.append(o_g)
        ld.append(l_g)
        keep = min(window, T)
        kv = pp[:Np, gi * 3 * HW + HW:(gi + 1) * 3 * HW].reshape(Bp, T, 2, DIL_HEADS, HEAD_DIM)
        dil_p.append(kv[:, T - keep:][None])
        dil_s.append(nbuf.reshape(buf.shape)[None])
    merged = jnp.concatenate([dil_merge(op, lp, Np), dil_merge(od, ld, Ns)], axis=0)
    x = out_proj(merged, bf(dil_w_out[0]), x)
    x = ffn_dense(x, norm_ffn[2], bf(ffn_w_up[1]), bf(ffn_w_down[1]))

    QW = NSA_HEADS * HEAD_DIM
    KW = NSA_KV_HEADS * HEAD_DIM
    w_in = jnp.concatenate([nsa_w_in[0], nsa_w_gate[0],
                            jnp.zeros((D, LANES - nsa_w_gate.shape[2]), F32)], axis=1)
    ones_q = jnp.ones((NSA_HEADS, HEAD_DIM), F32)
    ones_k = jnp.ones((NSA_KV_HEADS, HEAD_DIM), F32)
    gains = jnp.concatenate([ones_q * nsa_qn_g[0]]
                            + [blk for s in range(3) for blk in (ones_k * nsa_kn_g[0, s], ones_k)]
                            + [jnp.ones((1, HEAD_DIM), F32)], axis=0)
    flags = [True] * NSA_HEADS + ([True] * NSA_KV_HEADS + [False] * NSA_KV_HEADS) * 3 + [False]
    q, new, win, gate = proj_heads(x, norm_mix[3], bf(w_in), cos_h, sin_h, gains, flags,
                                   [(0, QW), (QW, QW + 4 * KW), (QW + 4 * KW, QW + 6 * KW),
                                    (QW + 6 * KW, QW + 6 * KW + LANES)])
    w1, w2 = bf(nsa_cmp_w1[0]), bf(nsa_cmp_w2[0])
    ppages = new.reshape(M // PAGE, PAGE, 4 * KW)
    ptab = jnp.arange(Bp * (T // PAGE), dtype=I32)
    kc_p = nsa_compress(ppages, ptab, Bp, T // PAGE, nsa_cmp_pe[0], w1, w2)
    o = nsa_attn(q, gate, kc_p, new, win, Bp, T)
    pool = cache_nsa_kv[0].reshape(cache_nsa_kv.shape[1], -1, HEAD_DIM)
    wbuf = cache_nsa_win[0]
    o, win_s = nsa_sample(q, gate, new, win, pool, page_table.reshape(-1).astype(I32),
                          wbuf.reshape(Bs, -1, HEAD_DIM), nsa_cmp_pe[0], w1, w2, Bs, Ts, Np, past, o)
    x = out_proj(o, bf(nsa_w_out[0]), x)
    x = moe_layer(x, norm_ffn[3], moe_router_w[1], moe_up, moe_down, 1)

    keep = min(NSA_WINDOW, T)
    win_p = win[:Np].reshape(Bp, T, 2, NSA_KV_HEADS, HEAD_DIM)[:, T - keep:]
    win_s = win_s.reshape(wbuf.shape)
    return (x[:Np].reshape(Bp, T, D), x[Np:].reshape(Bs, Ts, D),
            ret_p[None], ret_s[None], hg_p[None], hg_s[None],
            dil_p[0], dil_s[0], dil_p[1], dil_s[1], dil_p[2], dil_s[2],
            new[:Np].reshape(Bp, T, 4, NSA_KV_HEADS, HEAD_DIM)[None],
            new[Np:].reshape(Bs, Ts, 4, NSA_KV_HEADS, HEAD_DIM)[None],
            win_p[None], win_s[None])
```

```python
import functools
import math

import numpy as np
import jax
import jax.numpy as jnp
from jax import lax
from jax.experimental import pallas as pl
from jax.experimental.pallas import tpu as pltpu

F32 = jnp.float32
BF16 = jnp.bfloat16
I32 = jnp.int32

EPS = 1e-6
NEG = -1e30
ROPE_THETA = 10000.0
LANES = 128
VMEM_LIMIT = 56 * 1024 * 1024

HEAD_DIM = 128
RET_HEADS, RET_DK, RET_DV, RET_CHUNK = 4, 256, 512, 128
HG_HEADS, HG_DK, HG_CHUNK = 8, 128, 128
DIL_HEADS = 4
DIL_GROUPS = ((128, 1), (512, 4), (2048, 16))
DIL_QBLOCK = 128
NSA_HEADS, NSA_KV_HEADS, NSA_GROUP = 8, 2, 4
NSA_CMP_LEN, NSA_CMP_STRIDE = 32, 16
NSA_SLC_LEN, NSA_TOP_N, NSA_WINDOW = 64, 16, 512
PAGE = 128
N_EXPERTS = 8
MOE_ROWS = 512


def _cp(*sem):
    return pltpu.CompilerParams(dimension_semantics=sem, vmem_limit_bytes=VMEM_LIMIT)


def _tile(n, pref):
    t = pref
    while n % t:
        t //= 2
    return t


def _dot(a, b):
    return jnp.dot(a, b, preferred_element_type=F32)


def _dot_nt(a, b):
    return lax.dot_general(a, b, (((1,), (1,)), ((), ())), preferred_element_type=F32)


def _dot_tn(a, b):
    return lax.dot_general(a, b, (((0,), (0,)), ((), ())), preferred_element_type=F32)


def _split3(x):
    hi = x.astype(BF16)
    r = x - hi.astype(F32)
    mid = r.astype(BF16)
    lo = (r - mid.astype(F32)).astype(BF16)
    return hi, mid, lo


def _rms(x, g):
    return x * lax.rsqrt(jnp.mean(x * x, axis=-1, keepdims=True) + EPS) * g


def _sigmoid(x):
    return 1.0 / (1.0 + jnp.exp(-x))


def _silu(x):
    return x * _sigmoid(x)


def _norm_proj_kernel(x_ref, g_ref, w_ref, o_ref, xn_ref):
    @pl.when(pl.program_id(1) == 0)
    def _():
        xn_ref[...] = _rms(x_ref[...], g_ref[...]).astype(BF16)

    o_ref[...] = _dot(xn_ref[...], w_ref[...]).astype(o_ref.dtype)


def norm_proj(x, g, w, tn):
    M, D = x.shape
    N = w.shape[1]
    tm = _tile(M, 1024)
    return pl.pallas_call(
        _norm_proj_kernel,
        grid=(M // tm, N // tn),
        in_specs=[pl.BlockSpec((tm, D), lambda i, j: (i, 0)),
                  pl.BlockSpec((1, D), lambda i, j: (0, 0)),
                  pl.BlockSpec((D, tn), lambda i, j: (0, j))],
        out_specs=pl.BlockSpec((tm, tn), lambda i, j: (i, j)),
        out_shape=jax.ShapeDtypeStruct((M, N), F32),
        scratch_shapes=[pltpu.VMEM((tm, D), BF16)],
        compiler_params=_cp("parallel", "arbitrary"),
        name="norm_proj",
    )(x, g.reshape(1, D), w)


def _out_proj_kernel(a_ref, w_ref, x_ref, o_ref):
    o_ref[...] = x_ref[...] + _dot(a_ref[...].astype(BF16), w_ref[...])


def out_proj(a, w, x):
    M, K = a.shape
    D = w.shape[1]
    tm = _tile(M, 512)
    return pl.pallas_call(
        _out_proj_kernel,
        grid=(M // tm,),
        in_specs=[pl.BlockSpec((tm, K), lambda i: (i, 0)),
                  pl.BlockSpec((K, D), lambda i: (0, 0)),
                  pl.BlockSpec((tm, D), lambda i: (i, 0))],
        out_specs=pl.BlockSpec((tm, D), lambda i: (i, 0)),
        out_shape=jax.ShapeDtypeStruct((M, D), F32),
        compiler_params=_cp("parallel"),
        name="out_proj",
    )(a, w, x)


def _ffn_dense_kernel(x_ref, g_ref, wg_ref, wu_ref, wd_ref, o_ref, xn_ref, acc_ref, *, n_f):
    f = pl.program_id(1)

    @pl.when(f == 0)
    def _():
        xn_ref[...] = _rms(x_ref[...], g_ref[...]).astype(BF16)
        acc_ref[...] = jnp.zeros_like(acc_ref)

    xn = xn_ref[...]
    hg = _dot(xn, wg_ref[...])
    hu = _dot(xn, wu_ref[...])
    acc_ref[...] += _dot((_silu(hg) * hu).astype(BF16), wd_ref[...])

    @pl.when(f == n_f - 1)
    def _():
        o_ref[...] = x_ref[...] + acc_ref[...]


def ffn_dense(x, g, w_up, w_down):
    M, D = x.shape
    F = w_down.shape[0]
    tm = _tile(M, 512)
    tf = F // 2 if (F // 2) % LANES == 0 else F
    n_f = F // tf
    return pl.pallas_call(
        functools.partial(_ffn_dense_kernel, n_f=n_f),
        grid=(M // tm, n_f),
        in_specs=[pl.BlockSpec((tm, D), lambda i, f: (i, 0)),
                  pl.BlockSpec((1, D), lambda i, f: (0, 0)),
                  pl.BlockSpec((D, tf), lambda i, f: (0, f)),
                  pl.BlockSpec((D, tf), lambda i, f: (0, n_f + f)),
                  pl.BlockSpec((tf, D), lambda i, f: (f, 0))],
        out_specs=pl.BlockSpec((tm, D), lambda i, f: (i, 0)),
        out_shape=jax.ShapeDtypeStruct((M, D), F32),
        scratch_shapes=[pltpu.VMEM((tm, D), BF16), pltpu.VMEM((tm, D), F32)],
        compiler_params=_cp("parallel", "arbitrary"),
        name="ffn_dense",
    )(x, g.reshape(1, D), w_up, w_up, w_down)


def _ret_kernel(*refs, C, has_s0, has_prev):
    q_ref, k_ref, v_ref, g_ref, cos_ref, sin_ref, gng_ref, gnb_ref = refs[:8]
    n = 8
    s0_ref = None
    if has_s0:
        s0_ref = refs[n]
        n += 1
    if has_prev:
        n += 1
    y_ref, s_ref = refs[n], refs[n + 1]
    c = pl.program_id(1)

    @pl.when(c == 0)
    def _():
        if has_s0:
            s_ref[...] = s0_ref[...]
        else:
            s_ref[...] = jnp.zeros_like(s_ref)

    cos, sin = cos_ref[...], sin_ref[...]
    half = RET_DK // 2
    t = lax.broadcasted_iota(I32, (C, 1), 0).astype(F32)
    diff = (lax.broadcasted_iota(I32, (C, C), 0) - lax.broadcasted_iota(I32, (C, C), 1)).astype(F32)

    def rot(x):
        x1, x2 = x[:, :half], x[:, half:]
        return jnp.concatenate([x1 * cos - x2 * sin, x2 * cos + x1 * sin], axis=1)

    for h in range(RET_HEADS):
        lg = math.log(1.0 - 2.0 ** (-5.0 - h))
        qr = rot(q_ref[:, h * RET_DK:(h + 1) * RET_DK])
        kr = rot(k_ref[:, h * RET_DK:(h + 1) * RET_DK]) * RET_DK ** -0.5
        v = v_ref[:, h * RET_DV:(h + 1) * RET_DV].astype(BF16)
        decay = jnp.where(diff >= 0, jnp.exp(lg * jnp.maximum(diff, 0.0)), 0.0)
        sc = _dot_nt(qr.astype(BF16), kr.astype(BF16)) * decay
        S = s_ref[0, h]
        o = _dot(sc.astype(BF16), v) + _dot((qr * jnp.exp(lg * (t + 1.0))).astype(BF16), S.astype(BF16))
        kd = (kr * jnp.exp(lg * (C - 1.0 - t))).astype(BF16)
        s_ref[0, h] = math.exp(lg * C) * S + _dot_tn(kd, v)
        mu = jnp.mean(o, axis=-1, keepdims=True)
        var = jnp.mean(jnp.square(o - mu), axis=-1, keepdims=True)
        sl = slice(h * RET_DV, (h + 1) * RET_DV)
        on = (o - mu) * lax.rsqrt(var + EPS) * gng_ref[:, sl] + gnb_ref[:, sl]
        y_ref[:, sl] = _silu(g_ref[:, sl]) * on


def ret_core(proj, B, T, row0, cos, sin, tab0, gng, gnb, s0, y_prev):
    M = proj.shape[0]
    C = RET_CHUNK if T % RET_CHUNK == 0 else T
    nC = T // C
    r0, t0 = row0 // C, tab0 // C
    qkw, vw = RET_HEADS * RET_DK, RET_HEADS * RET_DV
    row = lambda b, c: r0 + b * nC + c
    in_specs = [pl.BlockSpec((C, qkw), lambda b, c: (row(b, c), 0)),
                pl.BlockSpec((C, qkw), lambda b, c: (row(b, c), 1)),
                pl.BlockSpec((C, vw), lambda b, c: (row(b, c), 2 * qkw // vw)),
                pl.BlockSpec((C, vw), lambda b, c: (row(b, c), 2 * qkw // vw + 1)),
                pl.BlockSpec((C, RET_DK // 2), lambda b, c: (t0 + c, 0)),
                pl.BlockSpec((C, RET_DK // 2), lambda b, c: (t0 + c, 0)),
                pl.BlockSpec((1, vw), lambda b, c: (0, 0)),
                pl.BlockSpec((1, vw), lambda b, c: (0, 0))]
    args = [proj, proj, proj, proj, cos, sin, gng.reshape(1, vw), gnb.reshape(1, vw)]
    sblock = (1, RET_HEADS, RET_DK, RET_DV)
    if s0 is not None:
        in_specs.append(pl.BlockSpec(sblock, lambda b, c: (b, 0, 0, 0)))
        args.append(s0)
    aliases = {}
    if y_prev is not None:
        in_specs.append(pl.BlockSpec(memory_space=pl.ANY))
        aliases = {len(args): 0}
        args.append(y_prev)
    return pl.pallas_call(
        functools.partial(_ret_kernel, C=C, has_s0=s0 is not None, has_prev=y_prev is not None),
        grid=(B, nC),
        in_specs=in_specs,
        out_specs=[pl.BlockSpec((C, vw), lambda b, c: (row(b, c), 0)),
                   pl.BlockSpec(sblock, lambda b, c: (b, 0, 0, 0))],
        out_shape=[jax.ShapeDtypeStruct((M, vw), F32),
                   jax.ShapeDtypeStruct((B,) + sblock[1:], F32)],
        input_output_aliases=aliases,
        compiler_params=_cp("parallel", "arbitrary"),
        name="ret_core",
    )(*args)


def _hgrn_levels(C):
    out, m = [], C // 2
    while m >= 1:
        out.append(m)
        m //= 2
    return out


def _hgrn_mats(C):
    t = np.arange(C)[:, None]
    u = np.arange(C)[None, :]
    mats = [(u <= t).astype(np.float32)]
    for m in _hgrn_levels(C):
        r = 2 * m * (t // (2 * m)) + m - 1
        a = ((u > r) & (u <= t)).astype(np.float32) - ((u > t) & (u <= r)).astype(np.float32)
        mats.append(a)
    return np.stack(mats)


def _hgrn_kernel(*refs, C, has_s0, has_prev):
    q_ref, f_ref, i_ref, g_ref, lb_ref, gng_ref, a_ref = refs[:7]
    n = 7
    s0_ref = None
    if has_s0:
        s0_ref = refs[n]
        n += 1
    if has_prev:
        n += 1
    y_ref, s_ref = refs[n], refs[n + 1]
    c = pl.program_id(1)

    @pl.when(c == 0)
    def _():
        if has_s0:
            s_ref[...] = s0_ref[...]
        else:
            s_ref[...] = jnp.zeros_like(s_ref)

    lb = lb_ref[...]
    f = lb + (1.0 - lb) * _sigmoid(f_ref[...])
    logf = jnp.log(f)
    kk = 1.0 - f
    q = _silu(q_ref[...])
    v = i_ref[...].astype(BF16)
    parts = _split3(logf)

    def amul(idx):
        a = a_ref[idx]
        return _dot(a, parts[0]) + _dot(a, parts[1]) + _dot(a, parts[2])

    bcum = amul(0)
    blast = bcum[C - 1:C, :]
    qb = (q * jnp.exp(bcum)).astype(BF16)
    kst = (kk * jnp.exp(blast - bcum)).astype(BF16)
    eb = jnp.exp(blast)

    ti = lax.broadcasted_iota(I32, (C, C), 0)
    si = lax.broadcasted_iota(I32, (C, C), 1)
    terms = [(q.astype(BF16), kk.astype(BF16), ti == si)]
    for li, m in enumerate(_hgrn_levels(C)):
        dq = amul(1 + li)
        qt = (q * jnp.exp(jnp.minimum(dq, 0.0))).astype(BF16)
        kt = (kk * jnp.exp(jnp.minimum(-dq, 0.0))).astype(BF16)
        blk = 2 * m
        mask = ((ti // blk) == (si // blk)) & ((ti % blk) >= m) & ((si % blk) < m)
        terms.append((qt, kt, mask))

    eye = (lax.broadcasted_iota(I32, (HG_DK, HG_DK), 0) == lax.broadcasted_iota(I32, (HG_DK, HG_DK), 1))
    for h in range(HG_HEADS):
        sl = slice(h * HG_DK, (h + 1) * HG_DK)
        sc = jnp.zeros((C, C), F32)
        for qt, kt, mask in terms:
            sc = sc + jnp.where(mask, _dot_nt(qt[:, sl], kt[:, sl]), 0.0)
        S = s_ref[0, h]
        vh = v[:, sl]
        o = _dot(sc.astype(BF16), vh) + _dot(qb[:, sl], S.astype(BF16))
        ecol = jnp.sum(jnp.where(eye, eb[:, sl], 0.0), axis=1, keepdims=True)
        s_ref[0, h] = ecol * S + _dot_tn(kst[:, sl], vh)
        on = o * lax.rsqrt(jnp.mean(o * o, axis=-1, keepdims=True) + EPS)
        y_ref[:, sl] = on * gng_ref[:, sl] * _sigmoid(g_ref[:, sl])


def hgrn_core(proj, B, T, row0, lb, gng, s0, y_prev):
    M = proj.shape[0]
    W = HG_HEADS * HG_DK
    C = HG_CHUNK if T % HG_CHUNK == 0 else T
    nC = T // C
    r0 = row0 // C
    row = lambda b, c: r0 + b * nC + c
    amats = jnp.asarray(_hgrn_mats(C), BF16)
    col = lambda j: pl.BlockSpec((C, W), lambda b, c: (row(b, c), j))
    in_specs = [col(j) for j in range(4)]
    in_specs += [pl.BlockSpec((1, W), lambda b, c: (0, 0)),
                 pl.BlockSpec((1, W), lambda b, c: (0, 0)),
                 pl.BlockSpec(amats.shape, lambda b, c: (0, 0, 0))]
    args = [proj, proj, proj, proj, lb.reshape(1, W), gng.reshape(1, W), amats]
    sblock = (1, HG_HEADS, HG_DK, HG_DK)
    if s0 is not None:
        in_specs.append(pl.BlockSpec(sblock, lambda b, c: (b, 0, 0, 0)))
        args.append(s0)
    aliases = {}
    if y_prev is not None:
        in_specs.append(pl.BlockSpec(memory_space=pl.ANY))
        aliases = {len(args): 0}
        args.append(y_prev)
    return pl.pallas_call(
        functools.partial(_hgrn_kernel, C=C, has_s0=s0 is not None, has_prev=y_prev is not None),
        grid=(B, nC),
        in_specs=in_specs,
        out_specs=[pl.BlockSpec((C, W), lambda b, c: (row(b, c), 0)),
                   pl.BlockSpec(sblock, lambda b, c: (b, 0, 0, 0))],
        out_shape=[jax.ShapeDtypeStruct((M, W), F32),
                   jax.ShapeDtypeStruct((B,) + sblock[1:], F32)],
        input_output_aliases=aliases,
        compiler_params=_cp("parallel", "arbitrary"),
        name="hgrn_core",
    )(*args)


def _prep_kernel(xin_ref, g_ref, w_ref, cos_ref, sin_ref, gain_ref, *o_refs, flags, outs, classes):
    cos, sin = cos_ref[...], sin_ref[...]
    tm = xin_ref.shape[0]
    c_refs = o_refs[len(outs):len(outs) + len(classes)]
    x_ref = o_refs[-1]
    stage = o_refs[-2] if classes else None
    x_ref[...] = _dot(_rms(xin_ref[...], g_ref[...]).astype(BF16), w_ref[...])
    for (lo, hi), o_ref in zip(outs, o_refs):
        for j in range(lo // LANES, hi // LANES):
            xj = x_ref[:, j * LANES:(j + 1) * LANES]
            if flags[j]:
                y = _rms(xj, gain_ref[j:j + 1, :])
                xj = y * cos + pltpu.roll(y, LANES // 2, 1) * sin
            o_ref[:, j * LANES - lo:(j + 1) * LANES - lo] = xj.astype(o_ref.dtype)
            if any(clo <= j * LANES < chi for clo, chi, _ in classes):
                stage[j] = xj
    for (lo, hi, d), c_ref in zip(classes, c_refs):
        for r in range(d):
            for j in range(lo // LANES, hi // LANES):
                c_ref[r, :, j * LANES - lo:(j + 1) * LANES - lo] = stage[j, pl.ds(r, tm // d, stride=d), :]


def proj_heads(x, g, w, cos, sin, gains, flags, outs, classes=()):
    M, D = x.shape
    W = w.shape[1]
    tm = _tile(M, 256)
    nb = W // LANES
    return pl.pallas_call(
        functools.partial(_prep_kernel, flags=tuple(flags), outs=tuple(outs), classes=tuple(classes)),
        grid=(M // tm,),
        in_specs=[pl.BlockSpec((tm, D), lambda i: (i, 0)),
                  pl.BlockSpec((1, D), lambda i: (0, 0)),
                  pl.BlockSpec((D, W), lambda i: (0, 0)),
                  pl.BlockSpec((tm, LANES), lambda i: (i, 0)),
                  pl.BlockSpec((tm, LANES), lambda i: (i, 0)),
                  pl.BlockSpec((nb, LANES), lambda i: (0, 0))],
        out_specs=([pl.BlockSpec((tm, hi - lo), lambda i: (i, 0)) for lo, hi in outs]
                   + [pl.BlockSpec((d, tm // d, hi - lo), lambda i: (0, i, 0)) for lo, hi, d in classes]),
        out_shape=([jax.ShapeDtypeStruct((M, hi - lo), F32) for lo, hi in outs]
                   + [jax.ShapeDtypeStruct((d, M // d, hi - lo), F32) for lo, hi, d in classes]),
        scratch_shapes=([pltpu.VMEM((nb, tm, LANES), F32)] if classes else []) + [pltpu.VMEM((tm, W), F32)],
        compiler_params=_cp("parallel"),
        name="proj_heads",
    )(x, g.reshape(1, D), w, cos, sin, gains)


def _softmax_head(s, valid, v):
    s = jnp.where(valid, s, NEG)
    m = jnp.max(s, axis=-1, keepdims=True)
    p = jnp.exp(s - m)
    l = jnp.sum(p, axis=-1, keepdims=True)
    o = _dot(p.astype(BF16), v) / l
    return o, m + jnp.log(l)


def _dil_prompt_kernel(q_ref, kc_ref, kp_ref, vc_ref, vp_ref, o_ref, l_ref, *, tq, reach):
    qi = pl.program_id(2)
    scale = HEAD_DIM ** -0.5
    qpos = qi * tq + lax.broadcasted_iota(I32, (tq, 2 * tq), 0)
    kpos = (qi - 1) * tq + lax.broadcasted_iota(I32, (tq, 2 * tq), 1)
    delta = qpos - kpos
    valid = (delta >= 0) & (delta <= reach) & (kpos >= 0)
    lane = lax.broadcasted_iota(I32, (tq, LANES), 1)
    lse_all = jnp.zeros((tq, LANES), F32)
    for h in range(DIL_HEADS):
        sl = slice(h * HEAD_DIM, (h + 1) * HEAD_DIM)
        q = q_ref[:, sl].astype(BF16)
        k = jnp.concatenate([kp_ref[:, sl], kc_ref[:, sl]], axis=0).astype(BF16)
        v = jnp.concatenate([vp_ref[:, sl], vc_ref[:, sl]], axis=0).astype(BF16)
        o, lse = _softmax_head(_dot_nt(q, k) * scale, valid, v)
        o_ref[h] = o
        lse_all = jnp.where(lane == h, lse, lse_all)
    l_ref[...] = lse_all


def dil_prompt(qkv, gi, col0, B, T):
    window, d = DIL_GROUPS[gi]
    HW = DIL_HEADS * HEAD_DIM
    Tc = T // d
    tq = min(DIL_QBLOCK, Tc)
    nq = Tc // tq

    def blk(off, prev):
        return pl.BlockSpec(
            (None, tq, HW),
            lambda b, r, qi: (r, b * nq + (jnp.maximum(qi - 1, 0) if prev else qi), col0 + off))

    return pl.pallas_call(
        functools.partial(_dil_prompt_kernel, tq=tq, reach=window // d),
        grid=(B, d, nq),
        in_specs=[blk(0, False), blk(1, False), blk(1, True), blk(2, False), blk(2, True)],
        out_specs=[pl.BlockSpec((None, DIL_HEADS, tq, HEAD_DIM), lambda b, r, qi: (r, 0, b * nq + qi, 0)),
                   pl.BlockSpec((None, tq, LANES), lambda b, r, qi: (r, b * nq + qi, 0))],
        out_shape=[jax.ShapeDtypeStruct((d, DIL_HEADS, B * Tc, HEAD_DIM), F32),
                   jax.ShapeDtypeStruct((d, B * Tc, LANES), F32)],
        compiler_params=_cp("parallel", "parallel", "arbitrary"),
        name=f"dil_prompt_{gi}",
    )(qkv, qkv, qkv, qkv, qkv)


def _dil_decode_kernel(q_ref, kn_ref, vn_ref, buf_ref, o_ref, l_ref, nb_ref, *, L, Ts, d, window):
    H = DIL_HEADS
    rpp = 2 * H
    scale = HEAD_DIM ** -0.5
    t = lax.broadcasted_iota(I32, (Ts, L + Ts), 0)
    i = lax.broadcasted_iota(I32, (Ts, L + Ts), 1)
    delta = L + t - i
    valid = (delta >= 0) & (delta % d == 0) & (delta <= window)
    lane = lax.broadcasted_iota(I32, (Ts, LANES), 1)
    lse_all = jnp.zeros((Ts, LANES), F32)
    for h in range(H):
        sl = slice(h * HEAD_DIM, (h + 1) * HEAD_DIM)
        q = q_ref[:, sl].astype(BF16)
        k = jnp.concatenate([buf_ref[0, pl.ds(h, L, stride=rpp), :], kn_ref[:, sl]], axis=0).astype(BF16)
        v = jnp.concatenate([buf_ref[0, pl.ds(H + h, L, stride=rpp), :], vn_ref[:, sl]], axis=0).astype(BF16)
        o, lse = _softmax_head(_dot_nt(q, k) * scale, valid, v)
        o_ref[h] = o
        lse_all = jnp.where(lane == h, lse, lse_all)
        nb_ref[0, pl.ds((L - Ts) * rpp + h, Ts, stride=rpp), :] = kn_ref[:, sl]
        nb_ref[0, pl.ds((L - Ts) * rpp + H + h, Ts, stride=rpp), :] = vn_ref[:, sl]
    l_ref[...] = lse_all
    nb_ref[0, 0:(L - Ts) * rpp, :] = buf_ref[0, Ts * rpp:L * rpp, :]


def dil_decode(pp, gi, B, Ts, row0, buf):
    window, d = DIL_GROUPS[gi]
    HW = DIL_HEADS * HEAD_DIM
    rows = buf.shape[1]
    L = rows // (2 * DIL_HEADS)
    r0 = row0 // Ts
    col = lambda off: pl.BlockSpec((Ts, HW), lambda b: (r0 + b, gi * 3 + off))
    return pl.pallas_call(
        functools.partial(_dil_decode_kernel, L=L, Ts=Ts, d=d, window=window),
        grid=(B,),
        in_specs=[col(0), col(1), col(2),
                  pl.BlockSpec((1, rows, HEAD_DIM), lambda b: (b, 0, 0))],
        out_specs=[pl.BlockSpec((None, DIL_HEADS, Ts, HEAD_DIM), lambda b: (0, 0, b, 0)),
                   pl.BlockSpec((None, Ts, LANES), lambda b: (0, b, 0)),
                   pl.BlockSpec((1, rows, HEAD_DIM), lambda b: (b, 0, 0))],
        out_shape=[jax.ShapeDtypeStruct((1, DIL_HEADS, B * Ts, HEAD_DIM), F32),
                   jax.ShapeDtypeStruct((1, B * Ts, LANES), F32),
                   jax.ShapeDtypeStruct((B, rows, HEAD_DIM), F32)],
        compiler_params=_cp("parallel"),
        name=f"dil_decode_{gi}",
    )(pp, pp, pp, buf)


def _dil_merge_kernel(*refs, ds):
    ng = len(ds)
    o_refs, l_refs, out_ref, stage = refs[:ng], refs[ng:2 * ng], refs[2 * ng], refs[2 * ng + 1]
    dmax = max(ds)
    n = out_ref.shape[0] // dmax
    for r in range(dmax):
        def rows(d):
            return r % d, pl.ds(r // d, n, stride=dmax // d)

        ls = [l_refs[g][rows(ds[g]) + (slice(None),)] for g in range(ng)]
        m = functools.reduce(jnp.maximum, ls)
        es = [jnp.exp(l - m) for l in ls]
        den = functools.reduce(lambda a, b: a + b, es)
        ws = [e / den for e in es]
        for h in range(DIL_HEADS):
            acc = None
            for g in range(ng):
                c, rs = rows(ds[g])
                term = ws[g][:, h:h + 1] * o_refs[g][c, h, rs, :]
                acc = term if acc is None else acc + term
            stage[h, pl.ds(r, n, stride=dmax), :] = acc
    for h in range(DIL_HEADS):
        out_ref[:, h * HEAD_DIM:(h + 1) * HEAD_DIM] = stage[h]


def dil_merge(os_, ls, rows):
    ds = tuple(o.shape[0] for o in os_)
    H, D = os_[0].shape[1], os_[0].shape[3]
    tm = _tile(rows, 256)
    ospec = lambda a: pl.BlockSpec((a.shape[0], H, tm // a.shape[0], D), lambda i: (0, 0, i, 0))
    lspec = lambda a: pl.BlockSpec((a.shape[0], tm // a.shape[0], a.shape[2]), lambda i: (0, i, 0))
    return pl.pallas_call(
        functools.partial(_dil_merge_kernel, ds=ds),
        grid=(rows // tm,),
        in_specs=[ospec(a) for a in os_] + [lspec(a) for a in ls],
        out_specs=pl.BlockSpec((tm, H * D), lambda i: (i, 0)),
        out_shape=jax.ShapeDtypeStruct((rows, H * D), F32),
        scratch_shapes=[pltpu.VMEM((H, tm, D), F32)],
        compiler_params=_cp("parallel"),
        name="dil_merge",
    )(*os_, *ls)


def _nsa_compress_kernel(tab_ref, pg_ref, pe_ref, w1_ref, w2_ref, o_ref, hs_ref, *, n_pages, ncmp):
    del tab_ref
    p = pl.program_id(1)
    hb = PAGE // NSA_CMP_STRIDE
    ncol = 2 * NSA_KV_HEADS
    rowlen = 4 * NSA_KV_HEADS * HEAD_DIM
    for c in range(ncol):
        for j in range(NSA_CMP_STRIDE):
            hs_ref[c, pl.ds(pl.multiple_of(p * hb, hb), hb), j * HEAD_DIM:(j + 1) * HEAD_DIM] = (
                pg_ref[0, :, j * rowlen + c * HEAD_DIM:j * rowlen + (c + 1) * HEAD_DIM])

    @pl.when(p == n_pages - 1)
    def _():
        half = NSA_CMP_STRIDE * HEAD_DIM
        R = n_pages * hb
        rowi = lax.broadcasted_iota(I32, (R, HEAD_DIM), 0)
        for c in range(ncol):
            s = c // NSA_KV_HEADS
            H = hs_ref[c]
            a = _dot((H + pe_ref[s, 0:1, :]).astype(BF16), w1_ref[s, 0:half, :])
            bm = _dot((H + pe_ref[s, 1:2, :]).astype(BF16), w1_ref[s, half:2 * half, :])
            pre = a + pltpu.roll(bm, R - 1, 0)
            out = _dot(_silu(pre).astype(BF16), w2_ref[s])
            o_ref[0, c] = jnp.where(rowi < ncmp, out, 0.0).astype(BF16)


def nsa_compress(pages, table, B, n_pages, pe, w1, w2):
    hb = PAGE // NSA_CMP_STRIDE
    R = n_pages * hb
    ncmp = (n_pages * PAGE - NSA_CMP_LEN) // NSA_CMP_STRIDE + 1
    half = NSA_CMP_STRIDE * HEAD_DIM
    pv = pages.reshape(pages.shape[0], hb, NSA_CMP_STRIDE * pages.shape[2])
    grid_spec = pltpu.PrefetchScalarGridSpec(
        num_scalar_prefetch=1,
        grid=(B, n_pages),
        in_specs=[pl.BlockSpec((1,) + pv.shape[1:], lambda b, p, tab: (tab[b * n_pages + p], 0, 0)),
                  pl.BlockSpec((2, 2, half), lambda b, p, tab: (0, 0, 0)),
                  pl.BlockSpec(w1.shape, lambda b, p, tab: (0, 0, 0)),
                  pl.BlockSpec(w2.shape, lambda b, p, tab: (0, 0, 0))],
        out_specs=pl.BlockSpec((1, 4, R, HEAD_DIM), lambda b, p, tab: (b, 0, 0, 0)),
        scratch_shapes=[pltpu.VMEM((4, R, half), F32)])
    return pl.pallas_call(
        functools.partial(_nsa_compress_kernel, n_pages=n_pages, ncmp=ncmp),
        grid_spec=grid_spec,
        out_shape=jax.ShapeDtypeStruct((B, 4, R, HEAD_DIM), BF16),
        compiler_params=_cp("parallel", "arbitrary"),
        name="nsa_compress",
    )(table, pv, pe.reshape(2, 2, half), w1, w2)


def _flash_update(m_ref, l_ref, a_ref, k, s, v_ones):
    D = HEAD_DIM
    m_old = m_ref[k]
    m_new = jnp.maximum(m_old, jnp.max(s, axis=-1, keepdims=True))
    m_use = jnp.maximum(m_new, -1e20)
    alpha = jnp.exp(m_old - m_use)
    pv = _dot(jnp.exp(s - m_use).astype(BF16), v_ones)
    l_ref[k] = alpha * l_ref[k] + pv[:, D:D + 1]
    a_ref[k] = alpha * a_ref[k] + pv[:, :D]
    m_ref[k] = m_new


def _nsa_qstack(q_ref, k, scale=None):
    G, D = NSA_GROUP, HEAD_DIM
    q = jnp.concatenate([q_ref[:, (k * G + g) * D:(k * G + g + 1) * D] for g in range(G)], axis=0)
    return (q if scale is None else q * scale).astype(BF16)


def _nsa_cmp_select(qs, kc, vc, c2s, pos_r, pos_q, n_slc, ncmp):
    R, tq = qs.shape[0], pos_q.shape[0]
    nidx = lax.broadcasted_iota(I32, (R, kc.shape[0]), 1)
    s = _dot_nt(qs, kc) * HEAD_DIM ** -0.5
    valid = (NSA_CMP_STRIDE * nidx + NSA_CMP_LEN - 1 <= pos_r) & (nidx < ncmp)
    s = jnp.where(valid, s, NEG)
    pr = jnp.where(valid, jnp.exp(s - jnp.max(s, axis=-1, keepdims=True)), 0.0)
    pc = pr / jnp.maximum(jnp.sum(pr, axis=-1, keepdims=True), 1e-30)
    o_cmp = _dot(pc.astype(BF16), vc)
    psum = pc[0:tq]
    for g in range(1, NSA_GROUP):
        psum = psum + pc[g * tq:(g + 1) * tq]
    ph, pm, plo = _split3(psum)
    imp = _dot(ph, c2s) + _dot(pm, c2s) + _dot(plo, c2s)
    j = lax.broadcasted_iota(I32, (tq, LANES), 1)
    cur = pos_q // NSA_SLC_LEN
    forced = (j == 0) | (j == cur) | (j == cur - 1)
    score = jnp.where(forced, 1e4, jnp.where(j <= cur, imp, -1.0))
    score = jnp.where(j < n_slc, score, -2.0)
    cnt = jnp.zeros((tq, LANES), F32)
    for jp in range(n_slc):
        col = score[:, jp:jp + 1]
        cnt = cnt + ((col > score) | ((col == score) & (jp < j))).astype(F32)
    return o_cmp, (cnt < float(min(NSA_TOP_N, n_slc))).astype(F32)


def _nsa_attn_kernel(qi_ref, kb_ref, q_ref, gate_ref, kc_ref, c2s_ref, blk_ref, kv_ref, wkv_ref, o_ref,
                     ocmp, qa, m_s, l_s, a_s, m_w, l_w, a_w, *, tq, KB, n_slc, ncmp):
    step = pl.program_id(1)
    qi, kb = qi_ref[step], kb_ref[step]
    G, KVH, D = NSA_GROUP, NSA_KV_HEADS, HEAD_DIM
    R = G * tq
    pos_r = qi * tq + lax.broadcasted_iota(I32, (R, 1), 0) % tq
    pos_q = qi * tq + lax.broadcasted_iota(I32, (tq, 1), 0)
    kb_last = (qi * tq + tq - 1) // KB

    @pl.when(kb == 0)
    def _():
        for k in range(KVH):
            ocmp[k], sel = _nsa_cmp_select(_nsa_qstack(q_ref, k), kc_ref[0, k], kc_ref[0, KVH + k], c2s_ref[...],
                                           pos_r, pos_q, n_slc, ncmp)
            bias = ((sel - 1.0) * -NEG).astype(BF16)
            qa[k] = jnp.concatenate([_nsa_qstack(q_ref, k, D ** -0.5), jnp.concatenate([bias] * G, axis=0)], axis=1)
        for ref in (m_s, m_w):
            ref[...] = jnp.full(ref.shape, NEG, F32)
        for ref in (l_s, a_s, l_w, a_w):
            ref[...] = jnp.zeros(ref.shape, F32)

    dist = pos_r - (kb * KB + lax.broadcasted_iota(I32, (R, KB), 1))
    ones = jnp.ones((KB, D), BF16)
    for k in range(KVH):
        ks = jnp.concatenate([kv_ref[:, k * D:(k + 1) * D].astype(BF16), blk_ref[...]], axis=1)
        vs = jnp.concatenate([kv_ref[:, (KVH + k) * D:(KVH + k + 1) * D].astype(BF16), ones], axis=1)
        s = _dot_nt(qa[k], ks)

        @pl.when(kb == kb_last)
        def _():
            _flash_update(m_s, l_s, a_s, k, jnp.where(dist >= 0, s, NEG), vs)

        @pl.when(kb != kb_last)
        def _():
            _flash_update(m_s, l_s, a_s, k, s, vs)

    @pl.when(kb >= kb_last - NSA_WINDOW // KB)
    def _():
        valid = (dist >= 0) & (dist <= NSA_WINDOW)
        for k in range(KVH):
            kw = wkv_ref[:, k * D:(k + 1) * D].astype(BF16)
            vw = jnp.concatenate([wkv_ref[:, (KVH + k) * D:(KVH + k + 1) * D].astype(BF16), ones], axis=1)
            _flash_update(m_w, l_w, a_w, k, jnp.where(valid, _dot_nt(qa[k, :, 0:D], kw), NEG), vw)

    @pl.when(kb == kb_last)
    def _():
        gs = _sigmoid(gate_ref[...])
        for k in range(KVH):
            _nsa_gated_out(o_ref, gs, k, tq, ocmp[k], a_s[k] / jnp.maximum(l_s[k], 1e-30),
                           a_w[k] / jnp.maximum(l_w[k], 1e-30))


def _nsa_gated_out(o_ref, gs, k, tq, o_c, o_s, o_w):
    for g in range(NSA_GROUP):
        hd = k * NSA_GROUP + g
        rows = slice(g * tq, (g + 1) * tq)
        o = (gs[:, 3 * hd:3 * hd + 1] * o_c[rows] + gs[:, 3 * hd + 1:3 * hd + 2] * o_s[rows]
             + gs[:, 3 * hd + 2:3 * hd + 3] * o_w[rows])
        o_ref[:, hd * HEAD_DIM:(hd + 1) * HEAD_DIM] = o


def _softmax_two(s1, ok1, v1, s2, ok2, v2):
    s1 = jnp.where(ok1, s1, NEG)
    s2 = jnp.where(ok2, s2, NEG)
    m = jnp.maximum(jnp.max(s1, axis=-1, keepdims=True), jnp.max(s2, axis=-1, keepdims=True))
    p1 = jnp.where(ok1, jnp.exp(s1 - m), 0.0)
    p2 = jnp.where(ok2, jnp.exp(s2 - m), 0.0)
    l = jnp.sum(p1, axis=-1, keepdims=True) + jnp.sum(p2, axis=-1, keepdims=True)
    return (_dot(p1.astype(BF16), v1) + _dot(p2.astype(BF16), v2)) / jnp.maximum(l, 1e-30)


def _nsa_sample_kernel(*refs, Ts, n_pages, Lw, n_slc, ncmp, past):
    (tab_ref, q_ref, gate_ref, xn_ref, xw_ref, wb_ref, pe_ref, w1_ref, w2_ref, c2s_ref, exp_ref) = refs[:11]
    pg_refs = refs[11:11 + n_pages]
    o_ref, nwb_ref, hs_ref = refs[12 + n_pages:]
    del tab_ref
    G, KVH, D = NSA_GROUP, NSA_KV_HEADS, HEAD_DIM
    R = G * Ts
    scale = D ** -0.5
    rpt = 4 * KVH
    hb = PAGE // NSA_CMP_STRIDE
    half = NSA_CMP_STRIDE * D
    pos_r = past + lax.broadcasted_iota(I32, (R, 1), 0) % Ts
    pos_q = past + lax.broadcasted_iota(I32, (Ts, 1), 0)

    def tile_rows(x):
        return jnp.concatenate([x] * G, axis=0)

    for u in range(n_pages):
        for c in range(2 * KVH):
            for j in range(NSA_CMP_STRIDE):
                hs_ref[c, u * hb:(u + 1) * hb, j * D:(j + 1) * D] = (
                    pg_refs[u][0, pl.ds(j * rpt + c, hb, stride=NSA_CMP_STRIDE * rpt), :])
    nrow = n_pages * hb
    rowi = lax.broadcasted_iota(I32, (nrow, D), 0)
    cmp = []
    for c in range(2 * KVH):
        s = c // KVH
        H = hs_ref[c]
        a = _dot((H + pe_ref[s, 0:1, :]).astype(BF16), w1_ref[s, 0:half, :])
        bm = _dot((H + pe_ref[s, 1:2, :]).astype(BF16), w1_ref[s, half:2 * half, :])
        out = _dot(_silu(a + pltpu.roll(bm, nrow - 1, 0)).astype(BF16), w2_ref[s])
        cmp.append(jnp.where(rowi < ncmp, out, 0.0).astype(BF16))

    gs = _sigmoid(gate_ref[...])
    jn = lax.broadcasted_iota(I32, (R, Ts), 1)
    causal = past + jn <= pos_r
    wpos = past - Lw + lax.broadcasted_iota(I32, (R, Lw), 1)
    wdist = pos_r - wpos
    wvalid = (wdist >= 0) & (wdist <= NSA_WINDOW)
    blk = past // NSA_SLC_LEN
    for k in range(KVH):
        qs = _nsa_qstack(q_ref, k)
        o_c, selk = _nsa_cmp_select(qs, cmp[k], cmp[KVH + k], c2s_ref[...], pos_r, pos_q, n_slc, ncmp)
        ks = jnp.concatenate([pg[0, pl.ds(2 * KVH + k, PAGE, stride=rpt), :] for pg in pg_refs],
                             axis=0).astype(BF16)
        vs = jnp.concatenate([pg[0, pl.ds(3 * KVH + k, PAGE, stride=rpt), :] for pg in pg_refs],
                             axis=0).astype(BF16)
        ok_past = tile_rows(_dot(selk.astype(BF16), exp_ref[...])) > 0.5
        ok_new = (tile_rows(selk[:, blk:blk + 1]) > 0.5) & causal
        kn = xn_ref[:, (2 * KVH + k) * D:(2 * KVH + k + 1) * D].astype(BF16)
        vn = xn_ref[:, (3 * KVH + k) * D:(3 * KVH + k + 1) * D].astype(BF16)
        o_s = _softmax_two(_dot_nt(qs, ks) * scale, ok_past, vs, _dot_nt(qs, kn) * scale, ok_new, vn)
        kw = wb_ref[0, pl.ds(k, Lw, stride=2 * KVH), :].astype(BF16)
        vw = wb_ref[0, pl.ds(KVH + k, Lw, stride=2 * KVH), :].astype(BF16)
        kwn = xw_ref[:, k * D:(k + 1) * D]
        vwn = xw_ref[:, (KVH + k) * D:(KVH + k + 1) * D]
        o_w = _softmax_two(_dot_nt(qs, kw) * scale, wvalid, vw,
                           _dot_nt(qs, kwn.astype(BF16)) * scale, causal, vwn.astype(BF16))
        _nsa_gated_out(o_ref, gs, k, Ts, o_c, o_s, o_w)
        nwb_ref[0, pl.ds((Lw - Ts) * 2 * KVH + k, Ts, stride=2 * KVH), :] = kwn
        nwb_ref[0, pl.ds((Lw - Ts) * 2 * KVH + KVH + k, Ts, stride=2 * KVH), :] = vwn
    nwb_ref[0, 0:(Lw - Ts) * 2 * KVH, :] = wb_ref[0, Ts * 2 * KVH:Lw * 2 * KVH, :]


def nsa_sample(q, gate, xnew, xwin, pool, table, wbuf, pe, w1, w2, B, Ts, row0, past, o_prev):
    M, QW = q.shape
    D = HEAD_DIM
    n_pages = past // PAGE
    assert past % PAGE == 0 and Ts < NSA_CMP_STRIDE
    Lw = wbuf.shape[1] // (2 * NSA_KV_HEADS)
    L = past + Ts
    n_slc = -(-L // NSA_SLC_LEN)
    ncmp = (L - NSA_CMP_LEN) // NSA_CMP_STRIDE + 1
    hb = PAGE // NSA_CMP_STRIDE
    half = NSA_CMP_STRIDE * D
    c2s = _cmp_to_slc(n_pages * hb, ncmp, n_slc)
    expand = jnp.asarray(np.arange(LANES)[:, None] == np.arange(past)[None, :] // NSA_SLC_LEN, BF16)
    r0 = row0 // Ts
    rows = lambda w: pl.BlockSpec((Ts, w), lambda b, tab: (r0 + b, 0))
    full = lambda a: pl.BlockSpec(a.shape, lambda b, tab: (0,) * a.ndim)
    page = lambda u: pl.BlockSpec((1,) + pool.shape[1:], lambda b, tab: (tab[b * n_pages + u], 0, 0))
    pe2 = pe.reshape(2, 2, half)
    in_specs = ([rows(QW), rows(LANES), rows(xnew.shape[1]), rows(xwin.shape[1]),
                 pl.BlockSpec((1,) + wbuf.shape[1:], lambda b, tab: (b, 0, 0)),
                 full(pe2), full(w1), full(w2), full(c2s), full(expand)]
                + [page(u) for u in range(n_pages)] + [pl.BlockSpec(memory_space=pl.ANY)])
    grid_spec = pltpu.PrefetchScalarGridSpec(
        num_scalar_prefetch=1,
        grid=(B,),
        in_specs=in_specs,
        out_specs=[rows(QW), pl.BlockSpec((1,) + wbuf.shape[1:], lambda b, tab: (b, 0, 0))],
        scratch_shapes=[pltpu.VMEM((2 * NSA_KV_HEADS, n_pages * hb, half), F32)])
    return pl.pallas_call(
        functools.partial(_nsa_sample_kernel, Ts=Ts, n_pages=n_pages, Lw=Lw, n_slc=n_slc, ncmp=ncmp, past=past),
        grid_spec=grid_spec,
        out_shape=[jax.ShapeDtypeStruct((M, QW), F32), jax.ShapeDtypeStruct(wbuf.shape, F32)],
        input_output_aliases={11 + n_pages: 0},
        compiler_params=_cp("parallel"),
        name="nsa_sample",
    )(table, q, gate, xnew, xwin, wbuf, pe2, w1, w2, c2s, expand, *([pool] * n_pages), o_prev)


def _cmp_to_slc(nrows, ncmp, n_slc):
    i = np.arange(nrows)[:, None] * NSA_CMP_STRIDE
    j = np.arange(LANES)[None, :] * NSA_SLC_LEN
    ov = np.clip(np.minimum(i + NSA_CMP_LEN, j + NSA_SLC_LEN) - np.maximum(i, j), 0, None) / NSA_CMP_LEN
    ov = np.where((np.arange(nrows)[:, None] < ncmp) & (np.arange(LANES)[None, :] < n_slc), ov, 0.0)
    return jnp.asarray(ov, BF16)


def nsa_attn(q, gate, kc, new, win, B, T):
    M = q.shape[0]
    D = HEAD_DIM
    QW = NSA_HEADS * D
    KW = 2 * NSA_KV_HEADS * D
    tq = min(2 * PAGE, T)
    KB = min(NSA_WINDOW, T)
    nqb, nkb = T // tq, T // KB
    n_slc = -(-T // NSA_SLC_LEN)
    ncmp = (T - NSA_CMP_LEN) // NSA_CMP_STRIDE + 1
    R = NSA_GROUP * tq
    c2s = _cmp_to_slc(kc.shape[2], ncmp, n_slc)
    blocks = jnp.asarray(np.arange(T)[:, None] // NSA_SLC_LEN == np.arange(LANES)[None, :], BF16)
    pairs = [(qi, kb) for qi in range(nqb) for kb in range((qi * tq + tq - 1) // KB + 1)]
    qi_tab = jnp.asarray([p[0] for p in pairs], I32)
    kb_tab = jnp.asarray([p[1] for p in pairs], I32)
    qrow = lambda b, s, qt, kt: (b * nqb + qt[s], 0)
    in_specs = [pl.BlockSpec((tq, QW), qrow),
                pl.BlockSpec((tq, LANES), qrow),
                pl.BlockSpec((1,) + kc.shape[1:], lambda b, s, qt, kt: (b, 0, 0, 0)),
                pl.BlockSpec(c2s.shape, lambda b, s, qt, kt: (0, 0)),
                pl.BlockSpec((KB, LANES), lambda b, s, qt, kt: (kt[s], 0)),
                pl.BlockSpec((KB, KW), lambda b, s, qt, kt: (b * nkb + kt[s], 1)),
                pl.BlockSpec((KB, KW), lambda b, s, qt, kt: (b * nkb + kt[s], 0))]
    vec = lambda: pltpu.VMEM((NSA_KV_HEADS, R, 1), F32)
    mat = lambda: pltpu.VMEM((NSA_KV_HEADS, R, D), F32)
    grid_spec = pltpu.PrefetchScalarGridSpec(
        num_scalar_prefetch=2,
        grid=(B, len(pairs)),
        in_specs=in_specs,
        out_specs=pl.BlockSpec((tq, QW), qrow),
        scratch_shapes=[mat(), pltpu.VMEM((NSA_KV_HEADS, R, 2 * D), BF16),
                        vec(), vec(), mat(), vec(), vec(), mat()])
    return pl.pallas_call(
        functools.partial(_nsa_attn_kernel, tq=tq, KB=KB, n_slc=n_slc, ncmp=ncmp),
        grid_spec=grid_spec,
        out_shape=jax.ShapeDtypeStruct((M, QW), F32),
        compiler_params=_cp("parallel", "arbitrary"),
        name="nsa_attn_prompt",
    )(qi_tab, kb_tab, q, gate, kc, c2s, blocks, new, win)


def _router_kernel(x_ref, g_ref, wr_ref, h_ref, e_ref, p_ref):
    xn = _rms(x_ref[...], g_ref[...])
    h_ref[...] = xn
    xh, xm, xl = _split3(xn)
    wh, wm, wl = _split3(wr_ref[...])
    logits = (_dot(xh, wh) + _dot(xh, wm) + _dot(xm, wh)
              + _dot(xh, wl) + _dot(xl, wh) + _dot(xm, wm))
    lane = lax.broadcasted_iota(I32, logits.shape, 1)
    lg = jnp.where(lane < N_EXPERTS, logits, -jnp.inf)
    m1 = jnp.max(lg, axis=-1, keepdims=True)
    i1 = jnp.min(jnp.where(lg == m1, lane, LANES), axis=-1, keepdims=True)
    lg2 = jnp.where(lane == i1, -jnp.inf, lg)
    m2 = jnp.max(lg2, axis=-1, keepdims=True)
    i2 = jnp.min(jnp.where(lg2 == m2, lane, LANES), axis=-1, keepdims=True)
    e = jnp.exp(m2 - m1)
    e_ref[...] = jnp.where(lane == 0, i1, jnp.where(lane == 1, i2, 0))
    p_ref[...] = jnp.where(lane == 0, 1.0 / (1.0 + e), jnp.where(lane == 1, e / (1.0 + e), 0.0))


def moe_router(x, g, router):
    M, D = x.shape
    tm = _tile(M, 512)
    wr = jnp.zeros((D, LANES), F32).at[:, :N_EXPERTS].set(router)
    return pl.pallas_call(
        _router_kernel,
        grid=(M // tm,),
        in_specs=[pl.BlockSpec((tm, D), lambda i: (i, 0)),
                  pl.BlockSpec((1, D), lambda i: (0, 0)),
                  pl.BlockSpec((D, LANES), lambda i: (0, 0))],
        out_specs=[pl.BlockSpec((tm, D), lambda i: (i, 0)),
                   pl.BlockSpec((tm, LANES), lambda i: (i, 0)),
                   pl.BlockSpec((tm, LANES), lambda i: (i, 0))],
        out_shape=[jax.ShapeDtypeStruct((M, D), F32),
                   jax.ShapeDtypeStruct((M, LANES), I32),
                   jax.ShapeDtypeStruct((M, LANES), F32)],
        compiler_params=_cp("parallel"),
        name="moe_router",
    )(x, g.reshape(1, D), wr)


def _row_copy(src_ref, dst_ref, sem, src_row, dst_row):
    return pltpu.make_async_copy(src_ref.at[pl.ds(src_row, 1)], dst_ref.at[pl.ds(dst_row, 1)], sem)


def _moe_ffn_kernel(be_ref, rt_ref, nu_ref, h_ref, wg_ref, wu_ref, wd_ref, y_ref,
                    xbuf, xb16, acc, sems, *, tm, n_f):
    del be_ref
    r, f = pl.program_id(0), pl.program_id(1)
    n_used = nu_ref[0]
    used = r < n_used
    slot = r % 2

    def gather(block, slot_):
        def issue(i, carry):
            _row_copy(h_ref, xbuf.at[slot_], sems.at[slot_], rt_ref[block * tm + i], i).start()
            return carry

        lax.fori_loop(0, tm, issue, 0, unroll=8)

    @pl.when(used & (f == 0) & (r == 0))
    def _():
        gather(0, 0)

    @pl.when(used & (f == 0))
    def _():
        def drain(i, carry):
            _row_copy(h_ref, xbuf.at[slot], sems.at[slot], 0, i).wait()
            return carry

        lax.fori_loop(0, tm, drain, 0, unroll=8)
        xb16[...] = xbuf[slot].astype(BF16)
        acc[...] = jnp.zeros_like(acc)

    @pl.when((r + 1 < n_used) & (f == 0))
    def _():
        gather(r + 1, 1 - slot)

    @pl.when(used)
    def _():
        x = xb16[...]
        hg = _dot(x, wg_ref[0])
        hu = _dot(x, wu_ref[0])
        acc[...] += _dot((_silu(hg) * hu).astype(BF16), wd_ref[0])

    @pl.when(used & (f == n_f - 1))
    def _():
        y_ref[...] = acc[...]

    @pl.when(jnp.logical_not(used) & (f == n_f - 1))
    def _():
        y_ref[...] = jnp.zeros_like(y_ref)


def moe_ffn(h, block_expert, row_tok, n_used, w_up, w_down, layer):
    D = h.shape[1]
    F = w_down.shape[2]
    tm = MOE_ROWS
    nb = block_expert.shape[0]
    tf = F // 2 if (F // 2) % LANES == 0 else F
    n_f = F // tf
    live = lambda r, f, nu: jnp.where(r < nu[0], f, 0)
    grid_spec = pltpu.PrefetchScalarGridSpec(
        num_scalar_prefetch=3,
        grid=(nb, n_f),
        in_specs=[pl.BlockSpec(memory_space=pl.ANY),
                  pl.BlockSpec((None, 1, D, tf), lambda r, f, be, rt, nu: (layer, be[r], 0, live(r, f, nu))),
                  pl.BlockSpec((None, 1, D, tf), lambda r, f, be, rt, nu: (layer, be[r], 0, n_f + live(r, f, nu))),
                  pl.BlockSpec((None, 1, tf, D), lambda r, f, be, rt, nu: (layer, be[r], live(r, f, nu), 0))],
        out_specs=pl.BlockSpec((tm, D), lambda r, f, be, rt, nu: (r, 0)),
        scratch_shapes=[pltpu.VMEM((2, tm, D), F32), pltpu.VMEM((tm, D), BF16), pltpu.VMEM((tm, D), F32),
                        pltpu.SemaphoreType.DMA((2,))])
    return pl.pallas_call(
        functools.partial(_moe_ffn_kernel, tm=tm, n_f=n_f),
        grid_spec=grid_spec,
        out_shape=jax.ShapeDtypeStruct((nb * tm, D), F32),
        compiler_params=_cp("arbitrary", "arbitrary"),
        name="moe_ffn",
    )(block_expert, row_tok, n_used, h, w_up, w_up, w_down)


def _moe_combine_kernel(p0_ref, p1_ref, x_ref, g_ref, y_ref, o_ref, ybuf, sem, *, tm):
    i = pl.program_id(0)

    def issue(t, carry):
        _row_copy(y_ref, ybuf.at[0], sem, p0_ref[i * tm + t], t).start()
        _row_copy(y_ref, ybuf.at[1], sem, p1_ref[i * tm + t], t).start()
        return carry

    lax.fori_loop(0, tm, issue, 0, unroll=8)

    def drain(t, carry):
        _row_copy(y_ref, ybuf.at[0], sem, 0, t).wait()
        _row_copy(y_ref, ybuf.at[1], sem, 0, t).wait()
        return carry

    lax.fori_loop(0, tm, drain, 0, unroll=8)
    g = g_ref[...]
    o_ref[...] = x_ref[...] + (g[:, 0:1] * ybuf[0] + g[:, 1:2] * ybuf[1])


def moe_combine(x, gates, y, pos0, pos1):
    M, D = x.shape
    tm = _tile(M, 256)
    grid_spec = pltpu.PrefetchScalarGridSpec(
        num_scalar_prefetch=2,
        grid=(M // tm,),
        in_specs=[pl.BlockSpec((tm, D), lambda i, a, b: (i, 0)),
                  pl.BlockSpec((tm, LANES), lambda i, a, b: (i, 0)),
                  pl.BlockSpec(memory_space=pl.ANY)],
        out_specs=pl.BlockSpec((tm, D), lambda i, a, b: (i, 0)),
        scratch_shapes=[pltpu.VMEM((2, tm, D), F32), pltpu.SemaphoreType.DMA(())])
    return pl.pallas_call(
        functools.partial(_moe_combine_kernel, tm=tm),
        grid_spec=grid_spec,
        out_shape=jax.ShapeDtypeStruct((M, D), F32),
        compiler_params=_cp("arbitrary"),
        name="moe_combine",
    )(pos0, pos1, x, gates, y)


def moe_layer(x, g, router, w_up, w_down, layer):
    M = x.shape[0]
    h, eidx, gates = moe_router(x, g, router)
    A = 2 * M
    e = eidx[:, :2].reshape(A)
    onehot = (e[:, None] == jnp.arange(N_EXPERTS, dtype=I32)[None, :]).astype(I32)
    csum = jnp.cumsum(onehot, axis=0)
    rank = jnp.sum(csum * onehot, axis=1) - 1
    counts = csum[-1]
    padded = (counts + MOE_ROWS - 1) // MOE_ROWS * MOE_ROWS
    pend = jnp.cumsum(padded)
    dest = ((pend - padded)[e] + rank).astype(I32)
    nb = (A + N_EXPERTS * (MOE_ROWS - 1) + MOE_ROWS - 1) // MOE_ROWS
    row_tok = jnp.zeros((nb * MOE_ROWS,), I32).at[dest].set(jnp.arange(A, dtype=I32) // 2)
    block_expert = jnp.minimum(
        jnp.searchsorted(pend, jnp.arange(nb, dtype=I32) * MOE_ROWS, side='right'), N_EXPERTS - 1).astype(I32)
    n_used = (pend[-1:] // MOE_ROWS).astype(I32)
    y = moe_ffn(h, block_expert, row_tok, n_used, w_up, w_down, layer)
    dest2 = dest.reshape(M, 2)
    return moe_combine(x, gates, y, dest2[:, 0], dest2[:, 1])


def _rope_tables(pos, half):
    inv = ROPE_THETA ** (-jnp.arange(half, dtype=F32) / half)
    ang = pos.astype(F32)[:, None] * inv[None, :]
    return jnp.cos(ang), jnp.sin(ang)


def _row_rope_tables(Bp, T, Bs, Ts, past):
    pos = jnp.concatenate([jnp.tile(jnp.arange(T, dtype=I32), Bp),
                           jnp.tile(past + jnp.arange(Ts, dtype=I32), Bs)])
    cos, sin = _rope_tables(pos, HEAD_DIM // 2)
    return jnp.concatenate([cos, cos], axis=1), jnp.concatenate([-sin, sin], axis=1)


def kernel(x_prompt, x_sample, state_ret, state_hgrn, cache_dil_w128, cache_dil_w512, cache_dil_w2048,
           cache_nsa_kv, cache_nsa_win, page_table, norm_mix, norm_ffn,
           ret_w_in, ret_gn_g, ret_gn_b, ret_w_out,
           hg_w_in, hg_lower_bounds, hg_gn_g, hg_w_out,
           dil_w_in, dil_qn_g, dil_kn_g, dil_w_out,
           nsa_w_in, nsa_w_gate, nsa_qn_g, nsa_kn_g, nsa_cmp_pe, nsa_cmp_w1, nsa_cmp_w2, nsa_w_out,
           ffn_w_up, ffn_w_down, moe_router_w, moe_w_up, moe_w_down):
    Bp, T, D = x_prompt.shape
    Bs, Ts, _ = x_sample.shape
    Np, Ns = Bp * T, Bs * Ts
    M = Np + Ns
    n_pages = page_table.shape[1]
    past = n_pages * PAGE
    bf = lambda w: w.astype(BF16)

    x = jnp.concatenate([x_prompt.reshape(Np, D), x_sample.reshape(Ns, D)], axis=0)
    lb_all = jnp.cumsum(jax.nn.softmax(hg_lower_bounds.astype(F32), axis=0), axis=0)
    lb_all = lb_all - lb_all[0:1]

    proj = norm_proj(x, norm_mix[0], bf(ret_w_in[0]), 1024)
    cos_r, sin_r = _rope_tables(jnp.arange(max(T, past + Ts), dtype=I32), RET_DK // 2)
    y, ret_p = ret_core(proj, Bp, T, 0, cos_r, sin_r, 0, ret_gn_g[0], ret_gn_b[0], None, None)
    y, ret_s = ret_core(proj, Bs, Ts, Np, cos_r, sin_r, past, ret_gn_g[0], ret_gn_b[0], state_ret[0], y)
    x = out_proj(y, bf(ret_w_out[0]), x)
    x = ffn_dense(x, norm_ffn[0], bf(ffn_w_up[0]), bf(ffn_w_down[0]))

    proj = norm_proj(x, norm_mix[1], bf(hg_w_in[0]), 1024)
    y, hg_p = hgrn_core(proj, Bp, T, 0, lb_all[1], hg_gn_g[0], None, None)
    y, hg_s = hgrn_core(proj, Bs, Ts, Np, lb_all[1], hg_gn_g[0], state_hgrn[0], y)
    x = out_proj(y, bf(hg_w_out[0]), x)
    moe_up, moe_down = bf(moe_w_up), bf(moe_w_down)
    x = moe_layer(x, norm_ffn[1], moe_router_w[0], moe_up, moe_down, 0)

    cos_h, sin_h = _row_rope_tables(Bp, T, Bs, Ts, past)
    HW = DIL_HEADS * HEAD_DIM
    dil_in_w = dil_w_in.shape[2]
    ones = jnp.ones((DIL_HEADS, HEAD_DIM), F32)
    gains = jnp.concatenate([jnp.concatenate([ones * dil_qn_g[0, gi], ones * dil_kn_g[0, gi], ones], axis=0)
                             for gi in range(len(DIL_GROUPS))], axis=0)
    flags = ([True] * (2 * DIL_HEADS) + [False] * DIL_HEADS) * len(DIL_GROUPS)
    dilated = [(gi, d) for gi, (_, d) in enumerate(DIL_GROUPS) if d > 1]
    pp, *by_class = proj_heads(x, norm_mix[2], bf(dil_w_in[0]), cos_h, sin_h, gains, flags, [(0, dil_in_w)],
                               [(gi * 3 * HW, (gi + 1) * 3 * HW, d) for gi, d in dilated])
    by_class = dict(zip([gi for gi, _ in dilated], by_class))
    caches = (cache_dil_w128, cache_dil_w512, cache_dil_w2048)
    op, lp, od, ld, dil_p, dil_s = [], [], [], [], [], []
    for gi, (window, _) in enumerate(DIL_GROUPS):
        if gi in by_class:
            o_g, l_g = dil_prompt(by_class[gi], gi, 0, Bp, T)
        else:
            o_g, l_g = dil_prompt(pp.reshape(1, M, dil_in_w), gi, gi * 3, Bp, T)
        op.append(o_g)
        lp.append(l_g)
        buf = caches[gi][0]
        o_g, l_g, nbuf = dil_decode(pp, gi, Bs, Ts, Np, buf.reshape(Bs, -1, HEAD_DIM))
        od.append(o_g)
        ld.append(l_g)
        keep = min(window, T)
        kv = pp[:Np, gi * 3 * HW + HW:(gi + 1) * 3 * HW].reshape(Bp, T, 2, DIL_HEADS, HEAD_DIM)
        dil_p.append(kv[:, T - keep:][None])
        dil_s.append(nbuf.reshape(buf.shape)[None])
    merged = jnp.concatenate([dil_merge(op, lp, Np), dil_merge(od, ld, Ns)], axis=0)
    x = out_proj(merged, bf(dil_w_out[0]), x)
    x = ffn_dense(x, norm_ffn[2], bf(ffn_w_up[1]), bf(ffn_w_down[1]))

    QW = NSA_HEADS * HEAD_DIM
    KW = NSA_KV_HEADS * HEAD_DIM
    w_in = jnp.concatenate([nsa_w_in[0], nsa_w_gate[0],
                            jnp.zeros((D, LANES - nsa_w_gate.shape[2]), F32)], axis=1)
    ones_q = jnp.ones((NSA_HEADS, HEAD_DIM), F32)
    ones_k = jnp.ones((NSA_KV_HEADS, HEAD_DIM), F32)
    gains = jnp.concatenate([ones_q * nsa_qn_g[0]]
                            + [blk for s in range(3) for blk in (ones_k * nsa_kn_g[0, s], ones_k)]
                            + [jnp.ones((1, HEAD_DIM), F32)], axis=0)
    flags = [True] * NSA_HEADS + ([True] * NSA_KV_HEADS + [False] * NSA_KV_HEADS) * 3 + [False]
    q, new, win, gate = proj_heads(x, norm_mix[3], bf(w_in), cos_h, sin_h, gains, flags,
                                   [(0, QW), (QW, QW + 4 * KW), (QW + 4 * KW, QW + 6 * KW),
                                    (QW + 6 * KW, QW + 6 * KW + LANES)])
    w1, w2 = bf(nsa_cmp_w1[0]), bf(nsa_cmp_w2[0])
    ppages = new.reshape(M // PAGE, PAGE, 4 * KW)
    ptab = jnp.arange(Bp * (T // PAGE), dtype=I32)
    kc_p = nsa_compress(ppages, ptab, Bp, T // PAGE, nsa_cmp_pe[0], w1, w2)
    o = nsa_attn(q, gate, kc_p, new, win, Bp, T)
    pool = cache_nsa_kv[0].reshape(cache_nsa_kv.shape[1], -1, HEAD_DIM)
    wbuf = cache_nsa_win[0]
    o, win_s = nsa_sample(q, gate, new, win, pool, page_table.reshape(-1).astype(I32),
                          wbuf.reshape(Bs, -1, HEAD_DIM), nsa_cmp_pe[0], w1, w2, Bs, Ts, Np, past, o)
    x = out_proj(o, bf(nsa_w_out[0]), x)
    x = moe_layer(x, norm_ffn[3], moe_router_w[1], moe_up, moe_down, 1)

    keep = min(NSA_WINDOW, T)
    win_p = win[:Np].reshape(Bp, T, 2, NSA_KV_HEADS, HEAD_DIM)[:, T - keep:]
    win_s = win_s.reshape(wbuf.shape)
    return (x[:Np].reshape(Bp, T, D), x[Np:].reshape(Bs, Ts, D),
            ret_p[None], ret_s[None], hg_p[None], hg_s[None],
            dil_p[0], dil_s[0], dil_p[1], dil_s[1], dil_p[2], dil_s[2],
            new[:Np].reshape(Bp, T, 4, NSA_KV_HEADS, HEAD_DIM)[None],
            new[Np:].reshape(Bs, Ts, 4, NSA_KV_HEADS, HEAD_DIM)[None],
            win_p[None], win_s[None])
```

```python
import functools
import math

import numpy as np
import jax
import jax.numpy as jnp
from jax import lax
from jax.experimental import pallas as pl
from jax.experimental.pallas import tpu as pltpu

F32 = jnp.float32
BF16 = jnp.bfloat16
I32 = jnp.int32

EPS = 1e-6
NEG = -1e30
ROPE_THETA = 10000.0
LANES = 128
VMEM_LIMIT = 56 * 1024 * 1024

HEAD_DIM = 128
RET_HEADS, RET_DK, RET_DV, RET_CHUNK = 4, 256, 512, 256
HG_HEADS, HG_DK, HG_CHUNK = 8, 128, 128
DIL_HEADS = 4
DIL_GROUPS = ((128, 1), (512, 4), (2048, 16))
DIL_QBLOCK = 128
NSA_HEADS, NSA_KV_HEADS, NSA_GROUP = 8, 2, 4
NSA_CMP_LEN, NSA_CMP_STRIDE = 32, 16
NSA_SLC_LEN, NSA_TOP_N, NSA_WINDOW = 64, 16, 512
PAGE = 128
N_EXPERTS = 8
MOE_ROWS = 512


def _cp(*sem):
    return pltpu.CompilerParams(dimension_semantics=sem, vmem_limit_bytes=VMEM_LIMIT)


def _tile(n, pref):
    t = pref
    while n % t:
        t //= 2
    return t


def _dot(a, b):
    return jnp.dot(a, b, preferred_element_type=F32)


def _dot_nt(a, b):
    return lax.dot_general(a, b, (((1,), (1,)), ((), ())), preferred_element_type=F32)


def _dot_tn(a, b):
    return lax.dot_general(a, b, (((0,), (0,)), ((), ())), preferred_element_type=F32)


def _split3(x):
    hi = x.astype(BF16)
    r = x - hi.astype(F32)
    mid = r.astype(BF16)
    lo = (r - mid.astype(F32)).astype(BF16)
    return hi, mid, lo


def _rms(x, g):
    return x * lax.rsqrt(jnp.mean(x * x, axis=-1, keepdims=True) + EPS) * g


def _sigmoid(x):
    return 1.0 / (1.0 + jnp.exp(-x))


def _silu(x):
    return x * _sigmoid(x)


def _norm_proj_kernel(x_ref, g_ref, w_ref, o_ref, xn_ref):
    @pl.when(pl.program_id(1) == 0)
    def _():
        xn_ref[...] = _rms(x_ref[...], g_ref[...]).astype(BF16)

    o_ref[...] = _dot(xn_ref[...], w_ref[...]).astype(o_ref.dtype)


def norm_proj(x, g, w, tn):
    M, D = x.shape
    N = w.shape[1]
    tm = _tile(M, 1024)
    return pl.pallas_call(
        _norm_proj_kernel,
        grid=(M // tm, N // tn),
        in_specs=[pl.BlockSpec((tm, D), lambda i, j: (i, 0)),
                  pl.BlockSpec((1, D), lambda i, j: (0, 0)),
                  pl.BlockSpec((D, tn), lambda i, j: (0, j))],
        out_specs=pl.BlockSpec((tm, tn), lambda i, j: (i, j)),
        out_shape=jax.ShapeDtypeStruct((M, N), F32),
        scratch_shapes=[pltpu.VMEM((tm, D), BF16)],
        compiler_params=_cp("parallel", "arbitrary"),
        name="norm_proj",
    )(x, g.reshape(1, D), w)


def _out_proj_kernel(a_ref, w_ref, x_ref, o_ref):
    o_ref[...] = x_ref[...] + _dot(a_ref[...].astype(BF16), w_ref[...])


def out_proj(a, w, x):
    M, K = a.shape
    D = w.shape[1]
    tm = _tile(M, 512)
    return pl.pallas_call(
        _out_proj_kernel,
        grid=(M // tm,),
        in_specs=[pl.BlockSpec((tm, K), lambda i: (i, 0)),
                  pl.BlockSpec((K, D), lambda i: (0, 0)),
                  pl.BlockSpec((tm, D), lambda i: (i, 0))],
        out_specs=pl.BlockSpec((tm, D), lambda i: (i, 0)),
        out_shape=jax.ShapeDtypeStruct((M, D), F32),
        compiler_params=_cp("parallel"),
        name="out_proj",
    )(a, w, x)


def _ffn_dense_kernel(x_ref, g_ref, wg_ref, wu_ref, wd_ref, o_ref, xn_ref, acc_ref, *, n_f):
    f = pl.program_id(1)

    @pl.when(f == 0)
    def _():
        xn_ref[...] = _rms(x_ref[...], g_ref[...]).astype(BF16)
        acc_ref[...] = jnp.zeros_like(acc_ref)

    xn = xn_ref[...]
    hg = _dot(xn, wg_ref[...])
    hu = _dot(xn, wu_ref[...])
    acc_ref[...] += _dot((_silu(hg) * hu).astype(BF16), wd_ref[...])

    @pl.when(f == n_f - 1)
    def _():
        o_ref[...] = x_ref[...] + acc_ref[...]


def ffn_dense(x, g, w_up, w_down):
    M, D = x.shape
    F = w_down.shape[0]
    tm = _tile(M, 512)
    tf = F // 2 if (F // 2) % LANES == 0 else F
    n_f = F // tf
    return pl.pallas_call(
        functools.partial(_ffn_dense_kernel, n_f=n_f),
        grid=(M // tm, n_f),
        in_specs=[pl.BlockSpec((tm, D), lambda i, f: (i, 0)),
                  pl.BlockSpec((1, D), lambda i, f: (0, 0)),
                  pl.BlockSpec((D, tf), lambda i, f: (0, f)),
                  pl.BlockSpec((D, tf), lambda i, f: (0, n_f + f)),
                  pl.BlockSpec((tf, D), lambda i, f: (f, 0))],
        out_specs=pl.BlockSpec((tm, D), lambda i, f: (i, 0)),
        out_shape=jax.ShapeDtypeStruct((M, D), F32),
        scratch_shapes=[pltpu.VMEM((tm, D), BF16), pltpu.VMEM((tm, D), F32)],
        compiler_params=_cp("parallel", "arbitrary"),
        name="ffn_dense",
    )(x, g.reshape(1, D), w_up, w_up, w_down)


def _ret_kernel(*refs, C, has_s0, has_prev):
    q_ref, k_ref, v_ref, g_ref, cos_ref, sin_ref, gng_ref, gnb_ref = refs[:8]
    n = 8
    s0_ref = None
    if has_s0:
        s0_ref = refs[n]
        n += 1
    if has_prev:
        n += 1
    y_ref, s_ref = refs[n], refs[n + 1]
    c = pl.program_id(1)

    @pl.when(c == 0)
    def _():
        if has_s0:
            s_ref[...] = s0_ref[...]
        else:
            s_ref[...] = jnp.zeros_like(s_ref)

    cos, sin = cos_ref[...], sin_ref[...]
    half = RET_DK // 2
    t = lax.broadcasted_iota(I32, (C, 1), 0).astype(F32)
    diff = (lax.broadcasted_iota(I32, (C, C), 0) - lax.broadcasted_iota(I32, (C, C), 1)).astype(F32)

    def rot(x):
        x1, x2 = x[:, :half], x[:, half:]
        return jnp.concatenate([x1 * cos - x2 * sin, x2 * cos + x1 * sin], axis=1)

    for h in range(RET_HEADS):
        lg = math.log(1.0 - 2.0 ** (-5.0 - h))
        qr = rot(q_ref[:, h * RET_DK:(h + 1) * RET_DK])
        kr = rot(k_ref[:, h * RET_DK:(h + 1) * RET_DK]) * RET_DK ** -0.5
        v = v_ref[:, h * RET_DV:(h + 1) * RET_DV].astype(BF16)
        decay = jnp.where(diff >= 0, jnp.exp(lg * jnp.maximum(diff, 0.0)), 0.0)
        sc = _dot_nt(qr.astype(BF16), kr.astype(BF16)) * decay
        S = s_ref[0, h]
        o = _dot(sc.astype(BF16), v) + _dot((qr * jnp.exp(lg * (t + 1.0))).astype(BF16), S.astype(BF16))
        kd = (kr * jnp.exp(lg * (C - 1.0 - t))).astype(BF16)
        s_ref[0, h] = math.exp(lg * C) * S + _dot_tn(kd, v)
        mu = jnp.mean(o, axis=-1, keepdims=True)
        var = jnp.mean(jnp.square(o - mu), axis=-1, keepdims=True)
        sl = slice(h * RET_DV, (h + 1) * RET_DV)
        on = (o - mu) * lax.rsqrt(var + EPS) * gng_ref[:, sl] + gnb_ref[:, sl]
        y_ref[:, sl] = _silu(g_ref[:, sl]) * on


def ret_core(proj, B, T, row0, cos, sin, tab0, gng, gnb, s0, y_prev):
    M = proj.shape[0]
    C = RET_CHUNK if T % RET_CHUNK == 0 else T
    nC = T // C
    r0, t0 = row0 // C, tab0 // C
    qkw, vw = RET_HEADS * RET_DK, RET_HEADS * RET_DV
    row = lambda b, c: r0 + b * nC + c
    in_specs = [pl.BlockSpec((C, qkw), lambda b, c: (row(b, c), 0)),
                pl.BlockSpec((C, qkw), lambda b, c: (row(b, c), 1)),
                pl.BlockSpec((C, vw), lambda b, c: (row(b, c), 2 * qkw // vw)),
                pl.BlockSpec((C, vw), lambda b, c: (row(b, c), 2 * qkw // vw + 1)),
                pl.BlockSpec((C, RET_DK // 2), lambda b, c: (t0 + c, 0)),
                pl.BlockSpec((C, RET_DK // 2), lambda b, c: (t0 + c, 0)),
                pl.BlockSpec((1, vw), lambda b, c: (0, 0)),
                pl.BlockSpec((1, vw), lambda b, c: (0, 0))]
    args = [proj, proj, proj, proj, cos, sin, gng.reshape(1, vw), gnb.reshape(1, vw)]
    sblock = (1, RET_HEADS, RET_DK, RET_DV)
    if s0 is not None:
        in_specs.append(pl.BlockSpec(sblock, lambda b, c: (b, 0, 0, 0)))
        args.append(s0)
    aliases = {}
    if y_prev is not None:
        in_specs.append(pl.BlockSpec(memory_space=pl.ANY))
        aliases = {len(args): 0}
        args.append(y_prev)
    return pl.pallas_call(
        functools.partial(_ret_kernel, C=C, has_s0=s0 is not None, has_prev=y_prev is not None),
        grid=(B, nC),
        in_specs=in_specs,
        out_specs=[pl.BlockSpec((C, vw), lambda b, c: (row(b, c), 0)),
                   pl.BlockSpec(sblock, lambda b, c: (b, 0, 0, 0))],
        out_shape=[jax.ShapeDtypeStruct((M, vw), F32),
                   jax.ShapeDtypeStruct((B,) + sblock[1:], F32)],
        input_output_aliases=aliases,
        compiler_params=_cp("parallel", "arbitrary"),
        name="ret_core",
    )(*args)


def _hgrn_levels(C):
    out, m = [], C // 2
    while m >= 1:
        out.append(m)
        m //= 2
    return out


def _hgrn_mats(C):
    t = np.arange(C)[:, None]
    u = np.arange(C)[None, :]
    mats = [(u <= t).astype(np.float32)]
    for m in _hgrn_levels(C):
        r = 2 * m * (t // (2 * m)) + m - 1
        a = ((u > r) & (u <= t)).astype(np.float32) - ((u > t) & (u <= r)).astype(np.float32)
        mats.append(a)
    return np.stack(mats)


def _hgrn_kernel(*refs, C, has_s0, has_prev):
    q_ref, f_ref, i_ref, g_ref, lb_ref, gng_ref, a_ref = refs[:7]
    n = 7
    s0_ref = None
    if has_s0:
        s0_ref = refs[n]
        n += 1
    if has_prev:
        n += 1
    y_ref, s_ref = refs[n], refs[n + 1]
    c = pl.program_id(1)

    @pl.when(c == 0)
    def _():
        if has_s0:
            s_ref[...] = s0_ref[...]
        else:
            s_ref[...] = jnp.zeros_like(s_ref)

    lb = lb_ref[...]
    f = lb + (1.0 - lb) * _sigmoid(f_ref[...])
    logf = jnp.log(f)
    kk = 1.0 - f
    q = _silu(q_ref[...])
    v = i_ref[...].astype(BF16)
    parts = _split3(logf)

    def amul(idx):
        a = a_ref[idx]
        return _dot(a, parts[0]) + _dot(a, parts[1]) + _dot(a, parts[2])

    bcum = amul(0)
    blast = bcum[C - 1:C, :]
    qb = (q * jnp.exp(bcum)).astype(BF16)
    kst = (kk * jnp.exp(blast - bcum)).astype(BF16)
    eb = jnp.exp(blast)

    ti = lax.broadcasted_iota(I32, (C, C), 0)
    si = lax.broadcasted_iota(I32, (C, C), 1)
    terms = [(q.astype(BF16), kk.astype(BF16), ti == si)]
    for li, m in enumerate(_hgrn_levels(C)):
        dq = amul(1 + li)
        qt = (q * jnp.exp(jnp.minimum(dq, 0.0))).astype(BF16)
        kt = (kk * jnp.exp(jnp.minimum(-dq, 0.0))).astype(BF16)
        blk = 2 * m
        mask = ((ti // blk) == (si // blk)) & ((ti % blk) >= m) & ((si % blk) < m)
        terms.append((qt, kt, mask))

    eye = (lax.broadcasted_iota(I32, (HG_DK, HG_DK), 0) == lax.broadcasted_iota(I32, (HG_DK, HG_DK), 1))
    for h in range(HG_HEADS):
        sl = slice(h * HG_DK, (h + 1) * HG_DK)
        sc = jnp.zeros((C, C), F32)
        for qt, kt, mask in terms:
            sc = sc + jnp.where(mask, _dot_nt(qt[:, sl], kt[:, sl]), 0.0)
        S = s_ref[0, h]
        vh = v[:, sl]
        o = _dot(sc.astype(BF16), vh) + _dot(qb[:, sl], S.astype(BF16))
        ecol = jnp.sum(jnp.where(eye, eb[:, sl], 0.0), axis=1, keepdims=True)
        s_ref[0, h] = ecol * S + _dot_tn(kst[:, sl], vh)
        on = o * lax.rsqrt(jnp.mean(o * o, axis=-1, keepdims=True) + EPS)
        y_ref[:, sl] = on * gng_ref[:, sl] * _sigmoid(g_ref[:, sl])


def hgrn_core(proj, B, T, row0, lb, gng, s0, y_prev):
    M = proj.shape[0]
    W = HG_HEADS * HG_DK
    C = HG_CHUNK if T % HG_CHUNK == 0 else T
    nC = T // C
    r0 = row0 // C
    row = lambda b, c: r0 + b * nC + c
    amats = jnp.asarray(_hgrn_mats(C), BF16)
    col = lambda j: pl.BlockSpec((C, W), lambda b, c: (row(b, c), j))
    in_specs = [col(j) for j in range(4)]
    in_specs += [pl.BlockSpec((1, W), lambda b, c: (0, 0)),
                 pl.BlockSpec((1, W), lambda b, c: (0, 0)),
                 pl.BlockSpec(amats.shape, lambda b, c: (0, 0, 0))]
    args = [proj, proj, proj, proj, lb.reshape(1, W), gng.reshape(1, W), amats]
    sblock = (1, HG_HEADS, HG_DK, HG_DK)
    if s0 is not None:
        in_specs.append(pl.BlockSpec(sblock, lambda b, c: (b, 0, 0, 0)))
        args.append(s0)
    aliases = {}
    if y_prev is not None:
        in_specs.append(pl.BlockSpec(memory_space=pl.ANY))
        aliases = {len(args): 0}
        args.append(y_prev)
    return pl.pallas_call(
        functools.partial(_hgrn_kernel, C=C, has_s0=s0 is not None, has_prev=y_prev is not None),
        grid=(B, nC),
        in_specs=in_specs,
        out_specs=[pl.BlockSpec((C, W), lambda b, c: (row(b, c), 0)),
                   pl.BlockSpec(sblock, lambda b, c: (b, 0, 0, 0))],
        out_shape=[jax.ShapeDtypeStruct((M, W), F32),
                   jax.ShapeDtypeStruct((B,) + sblock[1:], F32)],
        input_output_aliases=aliases,
        compiler_params=_cp("parallel", "arbitrary"),
        name="hgrn_core",
    )(*args)


def _prep_kernel(xin_ref, g_ref, w_ref, cos_ref, sin_ref, gain_ref, *o_refs, flags, outs, classes):
    cos, sin = cos_ref[...], sin_ref[...]
    tm = xin_ref.shape[0]
    c_refs = o_refs[len(outs):len(outs) + len(classes)]
    x_ref = o_refs[-1]
    stage = o_refs[-2] if classes else None
    x_ref[...] = _dot(_rms(xin_ref[...], g_ref[...]).astype(BF16), w_ref[...])
    for (lo, hi), o_ref in zip(outs, o_refs):
        for j in range(lo // LANES, hi // LANES):
            xj = x_ref[:, j * LANES:(j + 1) * LANES]
            if flags[j]:
                y = _rms(xj, gain_ref[j:j + 1, :])
                xj = y * cos + pltpu.roll(y, LANES // 2, 1) * sin
            o_ref[:, j * LANES - lo:(j + 1) * LANES - lo] = xj.astype(o_ref.dtype)
            if any(clo <= j * LANES < chi for clo, chi, _ in classes):
                stage[j] = xj
    for (lo, hi, d), c_ref in zip(classes, c_refs):
        for r in range(d):
            for j in range(lo // LANES, hi // LANES):
                c_ref[r, :, j * LANES - lo:(j + 1) * LANES - lo] = stage[j, pl.ds(r, tm // d, stride=d), :]


def proj_heads(x, g, w, cos, sin, gains, flags, outs, classes=()):
    M, D = x.shape
    W = w.shape[1]
    tm = _tile(M, 256)
    nb = W // LANES
    return pl.pallas_call(
        functools.partial(_prep_kernel, flags=tuple(flags), outs=tuple(outs), classes=tuple(classes)),
        grid=(M // tm,),
        in_specs=[pl.BlockSpec((tm, D), lambda i: (i, 0)),
                  pl.BlockSpec((1, D), lambda i: (0, 0)),
                  pl.BlockSpec((D, W), lambda i: (0, 0)),
                  pl.BlockSpec((tm, LANES), lambda i: (i, 0)),
                  pl.BlockSpec((tm, LANES), lambda i: (i, 0)),
                  pl.BlockSpec((nb, LANES), lambda i: (0, 0))],
        out_specs=([pl.BlockSpec((tm, hi - lo), lambda i: (i, 0)) for lo, hi in outs]
                   + [pl.BlockSpec((d, tm // d, hi - lo), lambda i: (0, i, 0)) for lo, hi, d in classes]),
        out_shape=([jax.ShapeDtypeStruct((M, hi - lo), F32) for lo, hi in outs]
                   + [jax.ShapeDtypeStruct((d, M // d, hi - lo), F32) for lo, hi, d in classes]),
        scratch_shapes=([pltpu.VMEM((nb, tm, LANES), F32)] if classes else []) + [pltpu.VMEM((tm, W), F32)],
        compiler_params=_cp("parallel"),
        name="proj_heads",
    )(x, g.reshape(1, D), w, cos, sin, gains)


def _softmax_head(s, valid, v):
    s = jnp.where(valid, s, NEG)
    m = jnp.max(s, axis=-1, keepdims=True)
    p = jnp.exp(s - m)
    l = jnp.sum(p, axis=-1, keepdims=True)
    o = _dot(p.astype(BF16), v) / l
    return o, m + jnp.log(l)


def _dil_prompt_kernel(q_ref, kc_ref, kp_ref, vc_ref, vp_ref, o_ref, l_ref, *, tq, reach):
    qi = pl.program_id(2)
    scale = HEAD_DIM ** -0.5
    qpos = qi * tq + lax.broadcasted_iota(I32, (tq, 2 * tq), 0)
    kpos = (qi - 1) * tq + lax.broadcasted_iota(I32, (tq, 2 * tq), 1)
    delta = qpos - kpos
    valid = (delta >= 0) & (delta <= reach) & (kpos >= 0)
    lane = lax.broadcasted_iota(I32, (tq, LANES), 1)
    lse_all = jnp.zeros((tq, LANES), F32)
    for h in range(DIL_HEADS):
        sl = slice(h * HEAD_DIM, (h + 1) * HEAD_DIM)
        q = q_ref[:, sl].astype(BF16)
        k = jnp.concatenate([kp_ref[:, sl], kc_ref[:, sl]], axis=0).astype(BF16)
        v = jnp.concatenate([vp_ref[:, sl], vc_ref[:, sl]], axis=0).astype(BF16)
        o, lse = _softmax_head(_dot_nt(q, k) * scale, valid, v)
        o_ref[h] = o
        lse_all = jnp.where(lane == h, lse, lse_all)
    l_ref[...] = lse_all


def dil_prompt(qkv, gi, col0, B, T):
    window, d = DIL_GROUPS[gi]
    HW = DIL_HEADS * HEAD_DIM
    Tc = T // d
    tq = min(DIL_QBLOCK, Tc)
    nq = Tc // tq

    def blk(off, prev):
        return pl.BlockSpec(
            (None, tq, HW),
            lambda b, r, qi: (r, b * nq + (jnp.maximum(qi - 1, 0) if prev else qi), col0 + off))

    return pl.pallas_call(
        functools.partial(_dil_prompt_kernel, tq=tq, reach=window // d),
        grid=(B, d, nq),
        in_specs=[blk(0, False), blk(1, False), blk(1, True), blk(2, False), blk(2, True)],
        out_specs=[pl.BlockSpec((None, DIL_HEADS, tq, HEAD_DIM), lambda b, r, qi: (r, 0, b * nq + qi, 0)),
                   pl.BlockSpec((None, tq, LANES), lambda b, r, qi: (r, b * nq + qi, 0))],
        out_shape=[jax.ShapeDtypeStruct((d, DIL_HEADS, B * Tc, HEAD_DIM), F32),
                   jax.ShapeDtypeStruct((d, B * Tc, LANES), F32)],
        compiler_params=_cp("parallel", "parallel", "arbitrary"),
        name=f"dil_prompt_{gi}",
    )(qkv, qkv, qkv, qkv, qkv)


def _dil_decode_kernel(q_ref, kn_ref, vn_ref, buf_ref, o_ref, l_ref, nb_ref, *, L, Ts, d, window):
    H = DIL_HEADS
    rpp = 2 * H
    scale = HEAD_DIM ** -0.5
    t = lax.broadcasted_iota(I32, (Ts, L + Ts), 0)
    i = lax.broadcasted_iota(I32, (Ts, L + Ts), 1)
    delta = L + t - i
    valid = (delta >= 0) & (delta % d == 0) & (delta <= window)
    lane = lax.broadcasted_iota(I32, (Ts, LANES), 1)
    lse_all = jnp.zeros((Ts, LANES), F32)
    for h in range(H):
        sl = slice(h * HEAD_DIM, (h + 1) * HEAD_DIM)
        q = q_ref[:, sl].astype(BF16)
        k = jnp.concatenate([buf_ref[0, pl.ds(h, L, stride=rpp), :], kn_ref[:, sl]], axis=0).astype(BF16)
        v = jnp.concatenate([buf_ref[0, pl.ds(H + h, L, stride=rpp), :], vn_ref[:, sl]], axis=0).astype(BF16)
        o, lse = _softmax_head(_dot_nt(q, k) * scale, valid, v)
        o_ref[h] = o
        lse_all = jnp.where(lane == h, lse, lse_all)
        nb_ref[0, pl.ds((L - Ts) * rpp + h, Ts, stride=rpp), :] = kn_ref[:, sl]
        nb_ref[0, pl.ds((L - Ts) * rpp + H + h, Ts, stride=rpp), :] = vn_ref[:, sl]
    l_ref[...] = lse_all
    nb_ref[0, 0:(L - Ts) * rpp, :] = buf_ref[0, Ts * rpp:L * rpp, :]


def dil_decode(pp, gi, B, Ts, row0, buf):
    window, d = DIL_GROUPS[gi]
    HW = DIL_HEADS * HEAD_DIM
    rows = buf.shape[1]
    L = rows // (2 * DIL_HEADS)
    r0 = row0 // Ts
    col = lambda off: pl.BlockSpec((Ts, HW), lambda b: (r0 + b, gi * 3 + off))
    return pl.pallas_call(
        functools.partial(_dil_decode_kernel, L=L, Ts=Ts, d=d, window=window),
        grid=(B,),
        in_specs=[col(0), col(1), col(2),
                  pl.BlockSpec((1, rows, HEAD_DIM), lambda b: (b, 0, 0))],
        out_specs=[pl.BlockSpec((None, DIL_HEADS, Ts, HEAD_DIM), lambda b: (0, 0, b, 0)),
                   pl.BlockSpec((None, Ts, LANES), lambda b: (0, b, 0)),
                   pl.BlockSpec((1, rows, HEAD_DIM), lambda b: (b, 0, 0))],
        out_shape=[jax.ShapeDtypeStruct((1, DIL_HEADS, B * Ts, HEAD_DIM), F32),
                   jax.ShapeDtypeStruct((1, B * Ts, LANES), F32),
                   jax.ShapeDtypeStruct((B, rows, HEAD_DIM), F32)],
        compiler_params=_cp("parallel"),
        name=f"dil_decode_{gi}",
    )(pp, pp, pp, buf)


def _dil_merge_kernel(*refs, ds):
    ng = len(ds)
    o_refs, l_refs, out_ref, stage = refs[:ng], refs[ng:2 * ng], refs[2 * ng], refs[2 * ng + 1]
    dmax = max(ds)
    n = out_ref.shape[0] // dmax
    for r in range(dmax):
        def rows(d):
            return r % d, pl.ds(r // d, n, stride=dmax // d)

        ls = [l_refs[g][rows(ds[g]) + (slice(None),)] for g in range(ng)]
        m = functools.reduce(jnp.maximum, ls)
        es = [jnp.exp(l - m) for l in ls]
        den = functools.reduce(lambda a, b: a + b, es)
        ws = [e / den for e in es]
        for h in range(DIL_HEADS):
            acc = None
            for g in range(ng):
                c, rs = rows(ds[g])
                term = ws[g][:, h:h + 1] * o_refs[g][c, h, rs, :]
                acc = term if acc is None else acc + term
            stage[h, pl.ds(r, n, stride=dmax), :] = acc
    for h in range(DIL_HEADS):
        out_ref[:, h * HEAD_DIM:(h + 1) * HEAD_DIM] = stage[h]


def dil_merge(os_, ls, rows):
    ds = tuple(o.shape[0] for o in os_)
    H, D = os_[0].shape[1], os_[0].shape[3]
    tm = _tile(rows, 256)
    ospec = lambda a: pl.BlockSpec((a.shape[0], H, tm // a.shape[0], D), lambda i: (0, 0, i, 0))
    lspec = lambda a: pl.BlockSpec((a.shape[0], tm // a.shape[0], a.shape[2]), lambda i: (0, i, 0))
    return pl.pallas_call(
        functools.partial(_dil_merge_kernel, ds=ds),
        grid=(rows // tm,),
        in_specs=[ospec(a) for a in os_] + [lspec(a) for a in ls],
        out_specs=pl.BlockSpec((tm, H * D), lambda i: (i, 0)),
        out_shape=jax.ShapeDtypeStruct((rows, H * D), F32),
        scratch_shapes=[pltpu.VMEM((H, tm, D), F32)],
        compiler_params=_cp("parallel"),
        name="dil_merge",
    )(*os_, *ls)


def _nsa_compress_kernel(tab_ref, pg_ref, pe_ref, w1_ref, w2_ref, o_ref, hs_ref, *, n_pages, ncmp):
    del tab_ref
    p = pl.program_id(1)
    hb = PAGE // NSA_CMP_STRIDE
    ncol = 2 * NSA_KV_HEADS
    rowlen = 4 * NSA_KV_HEADS * HEAD_DIM
    for c in range(ncol):
        for j in range(NSA_CMP_STRIDE):
            hs_ref[c, pl.ds(pl.multiple_of(p * hb, hb), hb), j * HEAD_DIM:(j + 1) * HEAD_DIM] = (
                pg_ref[0, :, j * rowlen + c * HEAD_DIM:j * rowlen + (c + 1) * HEAD_DIM])

    @pl.when(p == n_pages - 1)
    def _():
        half = NSA_CMP_STRIDE * HEAD_DIM
        R = n_pages * hb
        rowi = lax.broadcasted_iota(I32, (R, HEAD_DIM), 0)
        for c in range(ncol):
            s = c // NSA_KV_HEADS
            H = hs_ref[c]
            a = _dot((H + pe_ref[s, 0:1, :]).astype(BF16), w1_ref[s, 0:half, :])
            bm = _dot((H + pe_ref[s, 1:2, :]).astype(BF16), w1_ref[s, half:2 * half, :])
            pre = a + pltpu.roll(bm, R - 1, 0)
            out = _dot(_silu(pre).astype(BF16), w2_ref[s])
            o_ref[0, c] = jnp.where(rowi < ncmp, out, 0.0).astype(BF16)


def nsa_compress(pages, table, B, n_pages, pe, w1, w2):
    hb = PAGE // NSA_CMP_STRIDE
    R = n_pages * hb
    ncmp = (n_pages * PAGE - NSA_CMP_LEN) // NSA_CMP_STRIDE + 1
    half = NSA_CMP_STRIDE * HEAD_DIM
    pv = pages.reshape(pages.shape[0], hb, NSA_CMP_STRIDE * pages.shape[2])
    grid_spec = pltpu.PrefetchScalarGridSpec(
        num_scalar_prefetch=1,
        grid=(B, n_pages),
        in_specs=[pl.BlockSpec((1,) + pv.shape[1:], lambda b, p, tab: (tab[b * n_pages + p], 0, 0)),
                  pl.BlockSpec((2, 2, half), lambda b, p, tab: (0, 0, 0)),
                  pl.BlockSpec(w1.shape, lambda b, p, tab: (0, 0, 0)),
                  pl.BlockSpec(w2.shape, lambda b, p, tab: (0, 0, 0))],
        out_specs=pl.BlockSpec((1, 4, R, HEAD_DIM), lambda b, p, tab: (b, 0, 0, 0)),
        scratch_shapes=[pltpu.VMEM((4, R, half), F32)])
    return pl.pallas_call(
        functools.partial(_nsa_compress_kernel, n_pages=n_pages, ncmp=ncmp),
        grid_spec=grid_spec,
        out_shape=jax.ShapeDtypeStruct((B, 4, R, HEAD_DIM), BF16),
        compiler_params=_cp("parallel", "arbitrary"),
        name="nsa_compress",
    )(table, pv, pe.reshape(2, 2, half), w1, w2)


def _flash_update(m_ref, l_ref, a_ref, k, s, v_ones):
    D = HEAD_DIM
    m_old = m_ref[k]
    m_new = jnp.maximum(m_old, jnp.max(s, axis=-1, keepdims=True))
    m_use = jnp.maximum(m_new, -1e20)
    alpha = jnp.exp(m_old - m_use)
    pv = _dot(jnp.exp(s - m_use).astype(BF16), v_ones)
    l_ref[k] = alpha * l_ref[k] + pv[:, D:D + 1]
    a_ref[k] = alpha * a_ref[k] + pv[:, :D]
    m_ref[k] = m_new


def _nsa_qstack(q_ref, k, scale=None):
    G, D = NSA_GROUP, HEAD_DIM
    q = jnp.concatenate([q_ref[:, (k * G + g) * D:(k * G + g + 1) * D] for g in range(G)], axis=0)
    return (q if scale is None else q * scale).astype(BF16)


def _nsa_cmp_select(qs, kc, vc, c2s, pos_r, pos_q, n_slc, ncmp):
    R, tq = qs.shape[0], pos_q.shape[0]
    nidx = lax.broadcasted_iota(I32, (R, kc.shape[0]), 1)
    s = _dot_nt(qs, kc) * HEAD_DIM ** -0.5
    valid = (NSA_CMP_STRIDE * nidx + NSA_CMP_LEN - 1 <= pos_r) & (nidx < ncmp)
    s = jnp.where(valid, s, NEG)
    pr = jnp.where(valid, jnp.exp(s - jnp.max(s, axis=-1, keepdims=True)), 0.0)
    pc = pr / jnp.maximum(jnp.sum(pr, axis=-1, keepdims=True), 1e-30)
    o_cmp = _dot(pc.astype(BF16), vc)
    psum = pc[0:tq]
    for g in range(1, NSA_GROUP):
        psum = psum + pc[g * tq:(g + 1) * tq]
    ph, pm, plo = _split3(psum)
    imp = _dot(ph, c2s) + _dot(pm, c2s) + _dot(plo, c2s)
    j = lax.broadcasted_iota(I32, (tq, LANES), 1)
    cur = pos_q // NSA_SLC_LEN
    forced = (j == 0) | (j == cur) | (j == cur - 1)
    score = jnp.where(forced, 1e4, jnp.where(j <= cur, imp, -1.0))
    score = jnp.where(j < n_slc, score, -2.0)
    cnt = jnp.zeros((tq, LANES), F32)
    for jp in range(n_slc):
        col = score[:, jp:jp + 1]
        cnt = cnt + ((col > score) | ((col == score) & (jp < j))).astype(F32)
    return o_cmp, (cnt < float(min(NSA_TOP_N, n_slc))).astype(F32)


def _nsa_attn_kernel(qi_ref, kb_ref, q_ref, gate_ref, kc_ref, c2s_ref, blk_ref, kv_ref, wkv_ref, o_ref,
                     ocmp, qa, m_s, l_s, a_s, m_w, l_w, a_w, *, tq, KB, n_slc, ncmp):
    step = pl.program_id(1)
    qi, kb = qi_ref[step], kb_ref[step]
    G, KVH, D = NSA_GROUP, NSA_KV_HEADS, HEAD_DIM
    R = G * tq
    pos_r = qi * tq + lax.broadcasted_iota(I32, (R, 1), 0) % tq
    pos_q = qi * tq + lax.broadcasted_iota(I32, (tq, 1), 0)
    kb_last = (qi * tq + tq - 1) // KB

    @pl.when(kb == 0)
    def _():
        for k in range(KVH):
            ocmp[k], sel = _nsa_cmp_select(_nsa_qstack(q_ref, k), kc_ref[0, k], kc_ref[0, KVH + k], c2s_ref[...],
                                           pos_r, pos_q, n_slc, ncmp)
            bias = ((sel - 1.0) * -NEG).astype(BF16)
            qa[k] = jnp.concatenate([_nsa_qstack(q_ref, k, D ** -0.5), jnp.concatenate([bias] * G, axis=0)], axis=1)
        for ref in (m_s, m_w):
            ref[...] = jnp.full(ref.shape, NEG, F32)
        for ref in (l_s, a_s, l_w, a_w):
            ref[...] = jnp.zeros(ref.shape, F32)

    dist = pos_r - (kb * KB + lax.broadcasted_iota(I32, (R, KB), 1))
    ones = jnp.ones((KB, D), BF16)
    for k in range(KVH):
        ks = jnp.concatenate([kv_ref[:, k * D:(k + 1) * D].astype(BF16), blk_ref[...]], axis=1)
        vs = jnp.concatenate([kv_ref[:, (KVH + k) * D:(KVH + k + 1) * D].astype(BF16), ones], axis=1)
        s = _dot_nt(qa[k], ks)

        @pl.when(kb == kb_last)
        def _():
            _flash_update(m_s, l_s, a_s, k, jnp.where(dist >= 0, s, NEG), vs)

        @pl.when(kb != kb_last)
        def _():
            _flash_update(m_s, l_s, a_s, k, s, vs)

    @pl.when(kb >= kb_last - NSA_WINDOW // KB)
    def _():
        valid = (dist >= 0) & (dist <= NSA_WINDOW)
        for k in range(KVH):
            kw = wkv_ref[:, k * D:(k + 1) * D].astype(BF16)
            vw = jnp.concatenate([wkv_ref[:, (KVH + k) * D:(KVH + k + 1) * D].astype(BF16), ones], axis=1)
            _flash_update(m_w, l_w, a_w, k, jnp.where(valid, _dot_nt(qa[k, :, 0:D], kw), NEG), vw)

    @pl.when(kb == kb_last)
    def _():
        gs = _sigmoid(gate_ref[...])
        for k in range(KVH):
            _nsa_gated_out(o_ref, gs, k, tq, ocmp[k], a_s[k] / jnp.maximum(l_s[k], 1e-30),
                           a_w[k] / jnp.maximum(l_w[k], 1e-30))


def _nsa_gated_out(o_ref, gs, k, tq, o_c, o_s, o_w):
    for g in range(NSA_GROUP):
        hd = k * NSA_GROUP + g
        rows = slice(g * tq, (g + 1) * tq)
        o = (gs[:, 3 * hd:3 * hd + 1] * o_c[rows] + gs[:, 3 * hd + 1:3 * hd + 2] * o_s[rows]
             + gs[:, 3 * hd + 2:3 * hd + 3] * o_w[rows])
        o_ref[:, hd * HEAD_DIM:(hd + 1) * HEAD_DIM] = o


def _softmax_two(s1, ok1, v1, s2, ok2, v2):
    s1 = jnp.where(ok1, s1, NEG)
    s2 = jnp.where(ok2, s2, NEG)
    m = jnp.maximum(jnp.max(s1, axis=-1, keepdims=True), jnp.max(s2, axis=-1, keepdims=True))
    p1 = jnp.where(ok1, jnp.exp(s1 - m), 0.0)
    p2 = jnp.where(ok2, jnp.exp(s2 - m), 0.0)
    l = jnp.sum(p1, axis=-1, keepdims=True) + jnp.sum(p2, axis=-1, keepdims=True)
    return (_dot(p1.astype(BF16), v1) + _dot(p2.astype(BF16), v2)) / jnp.maximum(l, 1e-30)


def _nsa_sample_kernel(*refs, Ts, n_pages, Lw, n_slc, ncmp, past):
    (tab_ref, q_ref, gate_ref, xn_ref, xw_ref, wb_ref, pe_ref, w1_ref, w2_ref, c2s_ref, exp_ref) = refs[:11]
    pg_refs = refs[11:11 + n_pages]
    o_ref, nwb_ref, hs_ref = refs[12 + n_pages:]
    del tab_ref
    G, KVH, D = NSA_GROUP, NSA_KV_HEADS, HEAD_DIM
    R = G * Ts
    scale = D ** -0.5
    rpt = 4 * KVH
    hb = PAGE // NSA_CMP_STRIDE
    half = NSA_CMP_STRIDE * D
    pos_r = past + lax.broadcasted_iota(I32, (R, 1), 0) % Ts
    pos_q = past + lax.broadcasted_iota(I32, (Ts, 1), 0)

    def tile_rows(x):
        return jnp.concatenate([x] * G, axis=0)

    for u in range(n_pages):
        for c in range(2 * KVH):
            for j in range(NSA_CMP_STRIDE):
                hs_ref[c, u * hb:(u + 1) * hb, j * D:(j + 1) * D] = (
                    pg_refs[u][0, pl.ds(j * rpt + c, hb, stride=NSA_CMP_STRIDE * rpt), :])
    nrow = n_pages * hb
    rowi = lax.broadcasted_iota(I32, (nrow, D), 0)
    cmp = []
    for c in range(2 * KVH):
        s = c // KVH
        H = hs_ref[c]
        a = _dot((H + pe_ref[s, 0:1, :]).astype(BF16), w1_ref[s, 0:half, :])
        bm = _dot((H + pe_ref[s, 1:2, :]).astype(BF16), w1_ref[s, half:2 * half, :])
        out = _dot(_silu(a + pltpu.roll(bm, nrow - 1, 0)).astype(BF16), w2_ref[s])
        cmp.append(jnp.where(rowi < ncmp, out, 0.0).astype(BF16))

    gs = _sigmoid(gate_ref[...])
    jn = lax.broadcasted_iota(I32, (R, Ts), 1)
    causal = past + jn <= pos_r
    wpos = past - Lw + lax.broadcasted_iota(I32, (R, Lw), 1)
    wdist = pos_r - wpos
    wvalid = (wdist >= 0) & (wdist <= NSA_WINDOW)
    blk = past // NSA_SLC_LEN
    for k in range(KVH):
        qs = _nsa_qstack(q_ref, k)
        o_c, selk = _nsa_cmp_select(qs, cmp[k], cmp[KVH + k], c2s_ref[...], pos_r, pos_q, n_slc, ncmp)
        ks = jnp.concatenate([pg[0, pl.ds(2 * KVH + k, PAGE, stride=rpt), :] for pg in pg_refs],
                             axis=0).astype(BF16)
        vs = jnp.concatenate([pg[0, pl.ds(3 * KVH + k, PAGE, stride=rpt), :] for pg in pg_refs],
                             axis=0).astype(BF16)
        ok_past = tile_rows(_dot(selk.astype(BF16), exp_ref[...])) > 0.5
        ok_new = (tile_rows(selk[:, blk:blk + 1]) > 0.5) & causal
        kn = xn_ref[:, (2 * KVH + k) * D:(2 * KVH + k + 1) * D].astype(BF16)
        vn = xn_ref[:, (3 * KVH + k) * D:(3 * KVH + k + 1) * D].astype(BF16)
        o_s = _softmax_two(_dot_nt(qs, ks) * scale, ok_past, vs, _dot_nt(qs, kn) * scale, ok_new, vn)
        kw = wb_ref[0, pl.ds(k, Lw, stride=2 * KVH), :].astype(BF16)
        vw = wb_ref[0, pl.ds(KVH + k, Lw, stride=2 * KVH), :].astype(BF16)
        kwn = xw_ref[:, k * D:(k + 1) * D]
        vwn = xw_ref[:, (KVH + k) * D:(KVH + k + 1) * D]
        o_w = _softmax_two(_dot_nt(qs, kw) * scale, wvalid, vw,
                           _dot_nt(qs, kwn.astype(BF16)) * scale, causal, vwn.astype(BF16))
        _nsa_gated_out(o_ref, gs, k, Ts, o_c, o_s, o_w)
        nwb_ref[0, pl.ds((Lw - Ts) * 2 * KVH + k, Ts, stride=2 * KVH), :] = kwn
        nwb_ref[0, pl.ds((Lw - Ts) * 2 * KVH + KVH + k, Ts, stride=2 * KVH), :] = vwn
    nwb_ref[0, 0:(Lw - Ts) * 2 * KVH, :] = wb_ref[0, Ts * 2 * KVH:Lw * 2 * KVH, :]


def nsa_sample(q, gate, xnew, xwin, pool, table, wbuf, pe, w1, w2, B, Ts, row0, past, o_prev):
    M, QW = q.shape
    D = HEAD_DIM
    n_pages = past // PAGE
    assert past % PAGE == 0 and Ts < NSA_CMP_STRIDE
    Lw = wbuf.shape[1] // (2 * NSA_KV_HEADS)
    L = past + Ts
    n_slc = -(-L // NSA_SLC_LEN)
    ncmp = (L - NSA_CMP_LEN) // NSA_CMP_STRIDE + 1
    hb = PAGE // NSA_CMP_STRIDE
    half = NSA_CMP_STRIDE * D
    c2s = _cmp_to_slc(n_pages * hb, ncmp, n_slc)
    expand = jnp.asarray(np.arange(LANES)[:, None] == np.arange(past)[None, :] // NSA_SLC_LEN, BF16)
    r0 = row0 // Ts
    rows = lambda w: pl.BlockSpec((Ts, w), lambda b, tab: (r0 + b, 0))
    full = lambda a: pl.BlockSpec(a.shape, lambda b, tab: (0,) * a.ndim)
    page = lambda u: pl.BlockSpec((1,) + pool.shape[1:], lambda b, tab: (tab[b * n_pages + u], 0, 0))
    pe2 = pe.reshape(2, 2, half)
    in_specs = ([rows(QW), rows(LANES), rows(xnew.shape[1]), rows(xwin.shape[1]),
                 pl.BlockSpec((1,) + wbuf.shape[1:], lambda b, tab: (b, 0, 0)),
                 full(pe2), full(w1), full(w2), full(c2s), full(expand)]
                + [page(u) for u in range(n_pages)] + [pl.BlockSpec(memory_space=pl.ANY)])
    grid_spec = pltpu.PrefetchScalarGridSpec(
        num_scalar_prefetch=1,
        grid=(B,),
        in_specs=in_specs,
        out_specs=[rows(QW), pl.BlockSpec((1,) + wbuf.shape[1:], lambda b, tab: (b, 0, 0))],
        scratch_shapes=[pltpu.VMEM((2 * NSA_KV_HEADS, n_pages * hb, half), F32)])
    return pl.pallas_call(
        functools.partial(_nsa_sample_kernel, Ts=Ts, n_pages=n_pages, Lw=Lw, n_slc=n_slc, ncmp=ncmp, past=past),
        grid_spec=grid_spec,
        out_shape=[jax.ShapeDtypeStruct((M, QW), F32), jax.ShapeDtypeStruct(wbuf.shape, F32)],
        input_output_aliases={11 + n_pages: 0},
        compiler_params=_cp("parallel"),
        name="nsa_sample",
    )(table, q, gate, xnew, xwin, wbuf, pe2, w1, w2, c2s, expand, *([pool] * n_pages), o_prev)


def _cmp_to_slc(nrows, ncmp, n_slc):
    i = np.arange(nrows)[:, None] * NSA_CMP_STRIDE
    j = np.arange(LANES)[None, :] * NSA_SLC_LEN
    ov = np.clip(np.minimum(i + NSA_CMP_LEN, j + NSA_SLC_LEN) - np.maximum(i, j), 0, None) / NSA_CMP_LEN
    ov = np.where((np.arange(nrows)[:, None] < ncmp) & (np.arange(LANES)[None, :] < n_slc), ov, 0.0)
    return jnp.asarray(ov, BF16)


def nsa_attn(q, gate, kc, new, win, B, T):
    M = q.shape[0]
    D = HEAD_DIM
    QW = NSA_HEADS * D
    KW = 2 * NSA_KV_HEADS * D
    tq = min(4 * PAGE, T)
    KB = min(NSA_WINDOW, T)
    nqb, nkb = T // tq, T // KB
    n_slc = -(-T // NSA_SLC_LEN)
    ncmp = (T - NSA_CMP_LEN) // NSA_CMP_STRIDE + 1
    R = NSA_GROUP * tq
    c2s = _cmp_to_slc(kc.shape[2], ncmp, n_slc)
    blocks = jnp.asarray(np.arange(T)[:, None] // NSA_SLC_LEN == np.arange(LANES)[None, :], BF16)
    pairs = [(qi, kb) for qi in range(nqb) for kb in range((qi * tq + tq - 1) // KB + 1)]
    qi_tab = jnp.asarray([p[0] for p in pairs], I32)
    kb_tab = jnp.asarray([p[1] for p in pairs], I32)
    qrow = lambda b, s, qt, kt: (b * nqb + qt[s], 0)
    in_specs = [pl.BlockSpec((tq, QW), qrow),
                pl.BlockSpec((tq, LANES), qrow),
                pl.BlockSpec((1,) + kc.shape[1:], lambda b, s, qt, kt: (b, 0, 0, 0)),
                pl.BlockSpec(c2s.shape, lambda b, s, qt, kt: (0, 0)),
                pl.BlockSpec((KB, LANES), lambda b, s, qt, kt: (kt[s], 0)),
                pl.BlockSpec((KB, KW), lambda b, s, qt, kt: (b * nkb + kt[s], 1)),
                pl.BlockSpec((KB, KW), lambda b, s, qt, kt: (b * nkb + kt[s], 0))]
    vec = lambda: pltpu.VMEM((NSA_KV_HEADS, R, 1), F32)
    mat = lambda: pltpu.VMEM((NSA_KV_HEADS, R, D), F32)
    grid_spec = pltpu.PrefetchScalarGridSpec(
        num_scalar_prefetch=2,
        grid=(B, len(pairs)),
        in_specs=in_specs,
        out_specs=pl.BlockSpec((tq, QW), qrow),
        scratch_shapes=[mat(), pltpu.VMEM((NSA_KV_HEADS, R, 2 * D), BF16),
                        vec(), vec(), mat(), vec(), vec(), mat()])
    return pl.pallas_call(
        functools.partial(_nsa_attn_kernel, tq=tq, KB=KB, n_slc=n_slc, ncmp=ncmp),
        grid_spec=grid_spec,
        out_shape=jax.ShapeDtypeStruct((M, QW), F32),
        compiler_params=_cp("parallel", "arbitrary"),
        name="nsa_attn_prompt",
    )(qi_tab, kb_tab, q, gate, kc, c2s, blocks, new, win)


def _router_kernel(x_ref, g_ref, wr_ref, h_ref, e_ref, p_ref):
    xn = _rms(x_ref[...], g_ref[...])
    h_ref[...] = xn
    xh, xm, xl = _split3(xn)
    wh, wm, wl = _split3(wr_ref[...])
    logits = (_dot(xh, wh) + _dot(xh, wm) + _dot(xm, wh)
              + _dot(xh, wl) + _dot(xl, wh) + _dot(xm, wm))
    lane = lax.broadcasted_iota(I32, logits.shape, 1)
    lg = jnp.where(lane < N_EXPERTS, logits, -jnp.inf)
    m1 = jnp.max(lg, axis=-1, keepdims=True)
    i1 = jnp.min(jnp.where(lg == m1, lane, LANES), axis=-1, keepdims=True)
    lg2 = jnp.where(lane == i1, -jnp.inf, lg)
    m2 = jnp.max(lg2, axis=-1, keepdims=True)
    i2 = jnp.min(jnp.where(lg2 == m2, lane, LANES), axis=-1, keepdims=True)
    e = jnp.exp(m2 - m1)
    e_ref[...] = jnp.where(lane == 0, i1, jnp.where(lane == 1, i2, 0))
    p_ref[...] = jnp.where(lane == 0, 1.0 / (1.0 + e), jnp.where(lane == 1, e / (1.0 + e), 0.0))


def moe_router(x, g, router):
    M, D = x.shape
    tm = _tile(M, 512)
    wr = jnp.zeros((D, LANES), F32).at[:, :N_EXPERTS].set(router)
    return pl.pallas_call(
        _router_kernel,
        grid=(M // tm,),
        in_specs=[pl.BlockSpec((tm, D), lambda i: (i, 0)),
                  pl.BlockSpec((1, D), lambda i: (0, 0)),
                  pl.BlockSpec((D, LANES), lambda i: (0, 0))],
        out_specs=[pl.BlockSpec((tm, D), lambda i: (i, 0)),
                   pl.BlockSpec((tm, LANES), lambda i: (i, 0)),
                   pl.BlockSpec((tm, LANES), lambda i: (i, 0))],
        out_shape=[jax.ShapeDtypeStruct((M, D), F32),
                   jax.ShapeDtypeStruct((M, LANES), I32),
                   jax.ShapeDtypeStruct((M, LANES), F32)],
        compiler_params=_cp("parallel"),
        name="moe_router",
    )(x, g.reshape(1, D), wr)


def _row_copy(src_ref, dst_ref, sem, src_row, dst_row):
    return pltpu.make_async_copy(src_ref.at[pl.ds(src_row, 1)], dst_ref.at[pl.ds(dst_row, 1)], sem)


def _moe_ffn_kernel(be_ref, rt_ref, nu_ref, h_ref, wg_ref, wu_ref, wd_ref, y_ref,
                    xbuf, xb16, acc, sems, *, tm, n_f):
    del be_ref
    r, f = pl.program_id(0), pl.program_id(1)
    n_used = nu_ref[0]
    used = r < n_used
    slot = r % 2

    def gather(block, slot_):
        def issue(i, carry):
            _row_copy(h_ref, xbuf.at[slot_], sems.at[slot_], rt_ref[block * tm + i], i).start()
            return carry

        lax.fori_loop(0, tm, issue, 0, unroll=8)

    @pl.when(used & (f == 0) & (r == 0))
    def _():
        gather(0, 0)

    @pl.when(used & (f == 0))
    def _():
        def drain(i, carry):
            _row_copy(h_ref, xbuf.at[slot], sems.at[slot], 0, i).wait()
            return carry

        lax.fori_loop(0, tm, drain, 0, unroll=8)
        xb16[...] = xbuf[slot].astype(BF16)
        acc[...] = jnp.zeros_like(acc)

    @pl.when((r + 1 < n_used) & (f == 0))
    def _():
        gather(r + 1, 1 - slot)

    @pl.when(used)
    def _():
        x = xb16[...]
        hg = _dot(x, wg_ref[0])
        hu = _dot(x, wu_ref[0])
        acc[...] += _dot((_silu(hg) * hu).astype(BF16), wd_ref[0])

    @pl.when(used & (f == n_f - 1))
    def _():
        y_ref[...] = acc[...]

    @pl.when(jnp.logical_not(used) & (f == n_f - 1))
    def _():
        y_ref[...] = jnp.zeros_like(y_ref)


def moe_ffn(h, block_expert, row_tok, n_used, w_up, w_down, layer):
    D = h.shape[1]
    F = w_down.shape[2]
    tm = MOE_ROWS
    nb = block_expert.shape[0]
    tf = F // 2 if (F // 2) % LANES == 0 else F
    n_f = F // tf
    live = lambda r, f, nu: jnp.where(r < nu[0], f, 0)
    grid_spec = pltpu.PrefetchScalarGridSpec(
        num_scalar_prefetch=3,
        grid=(nb, n_f),
        in_specs=[pl.BlockSpec(memory_space=pl.ANY),
                  pl.BlockSpec((None, 1, D, tf), lambda r, f, be, rt, nu: (layer, be[r], 0, live(r, f, nu))),
                  pl.BlockSpec((None, 1, D, tf), lambda r, f, be, rt, nu: (layer, be[r], 0, n_f + live(r, f, nu))),
                  pl.BlockSpec((None, 1, tf, D), lambda r, f, be, rt, nu: (layer, be[r], live(r, f, nu), 0))],
        out_specs=pl.BlockSpec((tm, D), lambda r, f, be, rt, nu: (r, 0)),
        scratch_shapes=[pltpu.VMEM((2, tm, D), F32), pltpu.VMEM((tm, D), BF16), pltpu.VMEM((tm, D), F32),
                        pltpu.SemaphoreType.DMA((2,))])
    return pl.pallas_call(
        functools.partial(_moe_ffn_kernel, tm=tm, n_f=n_f),
        grid_spec=grid_spec,
        out_shape=jax.ShapeDtypeStruct((nb * tm, D), F32),
        compiler_params=_cp("arbitrary", "arbitrary"),
        name="moe_ffn",
    )(block_expert, row_tok, n_used, h, w_up, w_up, w_down)


def _moe_combine_kernel(p0_ref, p1_ref, x_ref, g_ref, y_ref, o_ref, ybuf, sem, *, tm):
    i = pl.program_id(0)

    def issue(t, carry):
        _row_copy(y_ref, ybuf.at[0], sem, p0_ref[i * tm + t], t).start()
        _row_copy(y_ref, ybuf.at[1], sem, p1_ref[i * tm + t], t).start()
        return carry

    lax.fori_loop(0, tm, issue, 0, unroll=8)

    def drain(t, carry):
        _row_copy(y_ref, ybuf.at[0], sem, 0, t).wait()
        _row_copy(y_ref, ybuf.at[1], sem, 0, t).wait()
        return carry

    lax.fori_loop(0, tm, drain, 0, unroll=8)
    g = g_ref[...]
    o_ref[...] = x_ref[...] + (g[:, 0:1] * ybuf[0] + g[:, 1:2] * ybuf[1])


def moe_combine(x, gates, y, pos0, pos1):
    M, D = x.shape
    tm = _tile(M, 256)
    grid_spec = pltpu.PrefetchScalarGridSpec(
        num_scalar_prefetch=2,
        grid=(M // tm,),
        in_specs=[pl.BlockSpec((tm, D), lambda i, a, b: (i, 0)),
                  pl.BlockSpec((tm, LANES), lambda i, a, b: (i, 0)),
                  pl.BlockSpec(memory_space=pl.ANY)],
        out_specs=pl.BlockSpec((tm, D), lambda i, a, b: (i, 0)),
        scratch_shapes=[pltpu.VMEM((2, tm, D), F32), pltpu.SemaphoreType.DMA(())])
    return pl.pallas_call(
        functools.partial(_moe_combine_kernel, tm=tm),
        grid_spec=grid_spec,
        out_shape=jax.ShapeDtypeStruct((M, D), F32),
        compiler_params=_cp("arbitrary"),
        name="moe_combine",
    )(pos0, pos1, x, gates, y)


def moe_layer(x, g, router, w_up, w_down, layer):
    M = x.shape[0]
    h, eidx, gates = moe_router(x, g, router)
    A = 2 * M
    e = eidx[:, :2].reshape(A)
    onehot = (e[:, None] == jnp.arange(N_EXPERTS, dtype=I32)[None, :]).astype(I32)
    csum = jnp.cumsum(onehot, axis=0)
    rank = jnp.sum(csum * onehot, axis=1) - 1
    counts = csum[-1]
    padded = (counts + MOE_ROWS - 1) // MOE_ROWS * MOE_ROWS
    pend = jnp.cumsum(padded)
    dest = ((pend - padded)[e] + rank).astype(I32)
    nb = (A + N_EXPERTS * (MOE_ROWS - 1) + MOE_ROWS - 1) // MOE_ROWS
    row_tok = jnp.zeros((nb * MOE_ROWS,), I32).at[dest].set(jnp.arange(A, dtype=I32) // 2)
    block_expert = jnp.minimum(
        jnp.searchsorted(pend, jnp.arange(nb, dtype=I32) * MOE_ROWS, side='right'), N_EXPERTS - 1).astype(I32)
    n_used = (pend[-1:] // MOE_ROWS).astype(I32)
    y = moe_ffn(h, block_expert, row_tok, n_used, w_up, w_down, layer)
    dest2 = dest.reshape(M, 2)
    return moe_combine(x, gates, y, dest2[:, 0], dest2[:, 1])


def _rope_tables(pos, half):
    inv = ROPE_THETA ** (-jnp.arange(half, dtype=F32) / half)
    ang = pos.astype(F32)[:, None] * inv[None, :]
    return jnp.cos(ang), jnp.sin(ang)


def _row_rope_tables(Bp, T, Bs, Ts, past):
    pos = jnp.concatenate([jnp.tile(jnp.arange(T, dtype=I32), Bp),
                           jnp.tile(past + jnp.arange(Ts, dtype=I32), Bs)])
    cos, sin = _rope_tables(pos, HEAD_DIM // 2)
    return jnp.concatenate([cos, cos], axis=1), jnp.concatenate([-sin, sin], axis=1)


def kernel(x_prompt, x_sample, state_ret, state_hgrn, cache_dil_w128, cache_dil_w512, cache_dil_w2048,
           cache_nsa_kv, cache_nsa_win, page_table, norm_mix, norm_ffn,
           ret_w_in, ret_gn_g, ret_gn_b, ret_w_out,
           hg_w_in, hg_lower_bounds, hg_gn_g, hg_w_out,
           dil_w_in, dil_qn_g, dil_kn_g, dil_w_out,
           nsa_w_in, nsa_w_gate, nsa_qn_g, nsa_kn_g, nsa_cmp_pe, nsa_cmp_w1, nsa_cmp_w2, nsa_w_out,
           ffn_w_up, ffn_w_down, moe_router_w, moe_w_up, moe_w_down):
    Bp, T, D = x_prompt.shape
    Bs, Ts, _ = x_sample.shape
    Np, Ns = Bp * T, Bs * Ts
    M = Np + Ns
    n_pages = page_table.shape[1]
    past = n_pages * PAGE
    bf = lambda w: w.astype(BF16)

    x = jnp.concatenate([x_prompt.reshape(Np, D), x_sample.reshape(Ns, D)], axis=0)
    lb_all = jnp.cumsum(jax.nn.softmax(hg_lower_bounds.astype(F32), axis=0), axis=0)
    lb_all = lb_all - lb_all[0:1]

    proj = norm_proj(x, norm_mix[0], bf(ret_w_in[0]), 1024)
    cos_r, sin_r = _rope_tables(jnp.arange(max(T, past + Ts), dtype=I32), RET_DK // 2)
    y, ret_p = ret_core(proj, Bp, T, 0, cos_r, sin_r, 0, ret_gn_g[0], ret_gn_b[0], None, None)
    y, ret_s = ret_core(proj, Bs, Ts, Np, cos_r, sin_r, past, ret_gn_g[0], ret_gn_b[0], state_ret[0], y)
    x = out_proj(y, bf(ret_w_out[0]), x)
    x = ffn_dense(x, norm_ffn[0], bf(ffn_w_up[0]), bf(ffn_w_down[0]))

    proj = norm_proj(x, norm_mix[1], bf(hg_w_in[0]), 1024)
    y, hg_p = hgrn_core(proj, Bp, T, 0, lb_all[1], hg_gn_g[0], None, None)
    y, hg_s = hgrn_core(proj, Bs, Ts, Np, lb_all[1], hg_gn_g[0], state_hgrn[0], y)
    x = out_proj(y, bf(hg_w_out[0]), x)
    moe_up, moe_down = bf(moe_w_up), bf(moe_w_down)
    x = moe_layer(x, norm_ffn[1], moe_router_w[0], moe_up, moe_down, 0)

    cos_h, sin_h = _row_rope_tables(Bp, T, Bs, Ts, past)
    HW = DIL_HEADS * HEAD_DIM
    dil_in_w = dil_w_in.shape[2]
    ones = jnp.ones((DIL_HEADS, HEAD_DIM), F32)
    gains = jnp.concatenate([jnp.concatenate([ones * dil_qn_g[0, gi], ones * dil_kn_g[0, gi], ones], axis=0)
                             for gi in range(len(DIL_GROUPS))], axis=0)
    flags = ([True] * (2 * DIL_HEADS) + [False] * DIL_HEADS) * len(DIL_GROUPS)
    dilated = [(gi, d) for gi, (_, d) in enumerate(DIL_GROUPS) if d > 1]
    pp, *by_class = proj_heads(x, norm_mix[2], bf(dil_w_in[0]), cos_h, sin_h, gains, flags, [(0, dil_in_w)],
                               [(gi * 3 * HW, (gi + 1) * 3 * HW, d) for gi, d in dilated])
    by_class = dict(zip([gi for gi, _ in dilated], by_class))
    caches = (cache_dil_w128, cache_dil_w512, cache_dil_w2048)
    op, lp, od, ld, dil_p, dil_s = [], [], [], [], [], []
    for gi, (window, _) in enumerate(DIL_GROUPS):
        if gi in by_class:
            o_g, l_g = dil_prompt(by_class[gi], gi, 0, Bp, T)
        else:
            o_g, l_g = dil_prompt(pp.reshape(1, M, dil_in_w), gi, gi * 3, Bp, T)
        op.append(o_g)
        lp.append(l_g)
        buf = caches[gi][0]
        o_g, l_g, nbuf = dil_decode(pp, gi, Bs, Ts, Np, buf.reshape(Bs, -1, HEAD_DIM))
        od.append(o_g)
        ld.append(l_g)
        keep = min(window, T)
        kv = pp[:Np, gi * 3 * HW + HW:(gi + 1) * 3 * HW].reshape(Bp, T, 2, DIL_HEADS, HEAD_DIM)
        dil_p.append(kv[:, T - keep:][None])
        dil_s.append(nbuf.reshape(buf.shape)[None])
    merged = jnp.concatenate([dil_merge(op, lp, Np), dil_merge(od, ld, Ns)], axis=0)
    x = out_proj(merged, bf(dil_w_out[0]), x)
    x = ffn_dense(x, norm_ffn[2], bf(ffn_w_up[1]), bf(ffn_w_down[1]))

    QW = NSA_HEADS * HEAD_DIM
    KW = NSA_KV_HEADS * HEAD_DIM
    w_in = jnp.concatenate([nsa_w_in[0], nsa_w_gate[0],
                            jnp.zeros((D, LANES - nsa_w_gate.shape[2]), F32)], axis=1)
    ones_q = jnp.ones((NSA_HEADS, HEAD_DIM), F32)
    ones_k = jnp.ones((NSA_KV_HEADS, HEAD_DIM), F32)
    gains = jnp.concatenate([ones_q * nsa_qn_g[0]]
                            + [blk for s in range(3) for blk in (ones_k * nsa_kn_g[0, s], ones_k)]
                            + [jnp.ones((1, HEAD_DIM), F32)], axis=0)
    flags = [True] * NSA_HEADS + ([True] * NSA_KV_HEADS + [False] * NSA_KV_HEADS) * 3 + [False]
    q, new, win, gate = proj_heads(x, norm_mix[3], bf(w_in), cos_h, sin_h, gains, flags,
                                   [(0, QW), (QW, QW + 4 * KW), (QW + 4 * KW, QW + 6 * KW),
                                    (QW + 6 * KW, QW + 6 * KW + LANES)])
    w1, w2 = bf(nsa_cmp_w1[0]), bf(nsa_cmp_w2[0])
    ppages = new.reshape(M // PAGE, PAGE, 4 * KW)
    ptab = jnp.arange(Bp * (T // PAGE), dtype=I32)
    kc_p = nsa_compress(ppages, ptab, Bp, T // PAGE, nsa_cmp_pe[0], w1, w2)
    o = nsa_attn(q, gate, kc_p, new, win, Bp, T)
    pool = cache_nsa_kv[0].reshape(cache_nsa_kv.shape[1], -1, HEAD_DIM)
    wbuf = cache_nsa_win[0]
    o, win_s = nsa_sample(q, gate, new, win, pool, page_table.reshape(-1).astype(I32),
                          wbuf.reshape(Bs, -1, HEAD_DIM), nsa_cmp_pe[0], w1, w2, Bs, Ts, Np, past, o)
    x = out_proj(o, bf(nsa_w_out[0]), x)
    x = moe_layer(x, norm_ffn[3], moe_router_w[1], moe_up, moe_down, 1)

    keep = min(NSA_WINDOW, T)
    win_p = win[:Np].reshape(Bp, T, 2, NSA_KV_HEADS, HEAD_DIM)[:, T - keep:]
    win_s = win_s.reshape(wbuf.shape)
    return (x[:Np].reshape(Bp, T, D), x[Np:].reshape(Bs, Ts, D),
            ret_p[None], ret_s[None], hg_p[None], hg_s[None],
            dil_p[0], dil_s[0], dil_p[1], dil_s[1], dil_p[2], dil_s[2],
            new[:Np].reshape(Bp, T, 4, NSA_KV_HEADS, HEAD_DIM)[None],
            new[Np:].reshape(Bs, Ts, 4, NSA_KV_HEADS, HEAD_DIM)[None],
            win_p[None], win_s[None])
```

```python
import functools
import math

import numpy as np
import jax
import jax.numpy as jnp
from jax import lax
from jax.experimental import pallas as pl
from jax.experimental.pallas import tpu as pltpu

F32 = jnp.float32
BF16 = jnp.bfloat16
I32 = jnp.int32

EPS = 1e-6
NEG = -1e30
ROPE_THETA = 10000.0
LANES = 128
VMEM_LIMIT = 56 * 1024 * 1024

HEAD_DIM = 128
RET_HEADS, RET_DK, RET_DV, RET_CHUNK = 4, 256, 512, 256
HG_HEADS, HG_DK, HG_CHUNK = 8, 128, 128
DIL_HEADS = 4
DIL_GROUPS = ((128, 1), (512, 4), (2048, 16))
DIL_QBLOCK = 128
NSA_HEADS, NSA_KV_HEADS, NSA_GROUP = 8, 2, 4
NSA_CMP_LEN, NSA_CMP_STRIDE = 32, 16
NSA_SLC_LEN, NSA_TOP_N, NSA_WINDOW = 64, 16, 512
PAGE = 128
N_EXPERTS = 8
MOE_ROWS = 512


def _cp(*sem):
    return pltpu.CompilerParams(dimension_semantics=sem, vmem_limit_bytes=VMEM_LIMIT)


def _tile(n, pref):
    t = pref
    while n % t:
        t //= 2
    return t


def _dot(a, b):
    return jnp.dot(a, b, preferred_element_type=F32)


def _dot_nt(a, b):
    return lax.dot_general(a, b, (((1,), (1,)), ((), ())), preferred_element_type=F32)


def _dot_tn(a, b):
    return lax.dot_general(a, b, (((0,), (0,)), ((), ())), preferred_element_type=F32)


def _split3(x):
    hi = x.astype(BF16)
    r = x - hi.astype(F32)
    mid = r.astype(BF16)
    lo = (r - mid.astype(F32)).astype(BF16)
    return hi, mid, lo


def _rms(x, g):
    return x * lax.rsqrt(jnp.mean(x * x, axis=-1, keepdims=True) + EPS) * g


def _sigmoid(x):
    return 1.0 / (1.0 + jnp.exp(-x))


def _silu(x):
    return x * _sigmoid(x)


def _norm_proj_kernel(x_ref, g_ref, w_ref, o_ref, xn_ref):
    @pl.when(pl.program_id(1) == 0)
    def _():
        xn_ref[...] = _rms(x_ref[...], g_ref[...]).astype(BF16)

    o_ref[...] = _dot(xn_ref[...], w_ref[...]).astype(o_ref.dtype)


def norm_proj(x, g, w, tn):
    M, D = x.shape
    N = w.shape[1]
    tm = _tile(M, 1024)
    return pl.pallas_call(
        _norm_proj_kernel,
        grid=(M // tm, N // tn),
        in_specs=[pl.BlockSpec((tm, D), lambda i, j: (i, 0)),
                  pl.BlockSpec((1, D), lambda i, j: (0, 0)),
                  pl.BlockSpec((D, tn), lambda i, j: (0, j))],
        out_specs=pl.BlockSpec((tm, tn), lambda i, j: (i, j)),
        out_shape=jax.ShapeDtypeStruct((M, N), F32),
        scratch_shapes=[pltpu.VMEM((tm, D), BF16)],
        compiler_params=_cp("parallel", "arbitrary"),
        name="norm_proj",
    )(x, g.reshape(1, D), w)


def _out_proj_kernel(a_ref, w_ref, x_ref, o_ref):
    o_ref[...] = x_ref[...] + _dot(a_ref[...].astype(BF16), w_ref[...])


def out_proj(a, w, x):
    M, K = a.shape
    D = w.shape[1]
    tm = _tile(M, 512)
    return pl.pallas_call(
        _out_proj_kernel,
        grid=(M // tm,),
        in_specs=[pl.BlockSpec((tm, K), lambda i: (i, 0)),
                  pl.BlockSpec((K, D), lambda i: (0, 0)),
                  pl.BlockSpec((tm, D), lambda i: (i, 0))],
        out_specs=pl.BlockSpec((tm, D), lambda i: (i, 0)),
        out_shape=jax.ShapeDtypeStruct((M, D), F32),
        compiler_params=_cp("parallel"),
        name="out_proj",
    )(a, w, x)


def _ffn_dense_kernel(x_ref, g_ref, wg_ref, wu_ref, wd_ref, o_ref, xn_ref, acc_ref, *, n_f):
    f = pl.program_id(1)

    @pl.when(f == 0)
    def _():
        xn_ref[...] = _rms(x_ref[...], g_ref[...]).astype(BF16)
        acc_ref[...] = jnp.zeros_like(acc_ref)

    xn = xn_ref[...]
    hg = _dot(xn, wg_ref[...])
    hu = _dot(xn, wu_ref[...])
    acc_ref[...] += _dot((_silu(hg) * hu).astype(BF16), wd_ref[...])

    @pl.when(f == n_f - 1)
    def _():
        o_ref[...] = x_ref[...] + acc_ref[...]


def ffn_dense(x, g, w_up, w_down):
    M, D = x.shape
    F = w_down.shape[0]
    tm = _tile(M, 512)
    tf = F // 2 if (F // 2) % LANES == 0 else F
    n_f = F // tf
    return pl.pallas_call(
        functools.partial(_ffn_dense_kernel, n_f=n_f),
        grid=(M // tm, n_f),
        in_specs=[pl.BlockSpec((tm, D), lambda i, f: (i, 0)),
                  pl.BlockSpec((1, D), lambda i, f: (0, 0)),
                  pl.BlockSpec((D, tf), lambda i, f: (0, f)),
                  pl.BlockSpec((D, tf), lambda i, f: (0, n_f + f)),
                  pl.BlockSpec((tf, D), lambda i, f: (f, 0))],
        out_specs=pl.BlockSpec((tm, D), lambda i, f: (i, 0)),
        out_shape=jax.ShapeDtypeStruct((M, D), F32),
        scratch_shapes=[pltpu.VMEM((tm, D), BF16), pltpu.VMEM((tm, D), F32)],
        compiler_params=_cp("parallel", "arbitrary"),
        name="ffn_dense",
    )(x, g.reshape(1, D), w_up, w_up, w_down)


def _ret_kernel(*refs, C, has_s0, has_prev):
    q_ref, k_ref, v_ref, g_ref, cos_ref, sin_ref, gng_ref, gnb_ref = refs[:8]
    n = 8
    s0_ref = None
    if has_s0:
        s0_ref = refs[n]
        n += 1
    if has_prev:
        n += 1
    y_ref, s_ref = refs[n], refs[n + 1]
    c = pl.program_id(1)

    @pl.when(c == 0)
    def _():
        if has_s0:
            s_ref[...] = s0_ref[...]
        else:
            s_ref[...] = jnp.zeros_like(s_ref)

    cos, sin = cos_ref[...], sin_ref[...]
    half = RET_DK // 2
    t = lax.broadcasted_iota(I32, (C, 1), 0).astype(F32)
    diff = (lax.broadcasted_iota(I32, (C, C), 0) - lax.broadcasted_iota(I32, (C, C), 1)).astype(F32)

    def rot(x):
        x1, x2 = x[:, :half], x[:, half:]
        return jnp.concatenate([x1 * cos - x2 * sin, x2 * cos + x1 * sin], axis=1)

    for h in range(RET_HEADS):
        lg = math.log(1.0 - 2.0 ** (-5.0 - h))
        qr = rot(q_ref[:, h * RET_DK:(h + 1) * RET_DK])
        kr = rot(k_ref[:, h * RET_DK:(h + 1) * RET_DK]) * RET_DK ** -0.5
        v = v_ref[:, h * RET_DV:(h + 1) * RET_DV].astype(BF16)
        decay = jnp.where(diff >= 0, jnp.exp(lg * jnp.maximum(diff, 0.0)), 0.0)
        sc = _dot_nt(qr.astype(BF16), kr.astype(BF16)) * decay
        S = s_ref[0, h]
        o = _dot(sc.astype(BF16), v) + _dot((qr * jnp.exp(lg * (t + 1.0))).astype(BF16), S.astype(BF16))
        kd = (kr * jnp.exp(lg * (C - 1.0 - t))).astype(BF16)
        s_ref[0, h] = math.exp(lg * C) * S + _dot_tn(kd, v)
        mu = jnp.mean(o, axis=-1, keepdims=True)
        var = jnp.mean(jnp.square(o - mu), axis=-1, keepdims=True)
        sl = slice(h * RET_DV, (h + 1) * RET_DV)
        on = (o - mu) * lax.rsqrt(var + EPS) * gng_ref[:, sl] + gnb_ref[:, sl]
        y_ref[:, sl] = _silu(g_ref[:, sl]) * on


def ret_core(proj, B, T, row0, cos, sin, tab0, gng, gnb, s0, y_prev):
    M = proj.shape[0]
    C = RET_CHUNK if T % RET_CHUNK == 0 else T
    nC = T // C
    r0, t0 = row0 // C, tab0 // C
    qkw, vw = RET_HEADS * RET_DK, RET_HEADS * RET_DV
    row = lambda b, c: r0 + b * nC + c
    in_specs = [pl.BlockSpec((C, qkw), lambda b, c: (row(b, c), 0)),
                pl.BlockSpec((C, qkw), lambda b, c: (row(b, c), 1)),
                pl.BlockSpec((C, vw), lambda b, c: (row(b, c), 2 * qkw // vw)),
                pl.BlockSpec((C, vw), lambda b, c: (row(b, c), 2 * qkw // vw + 1)),
                pl.BlockSpec((C, RET_DK // 2), lambda b, c: (t0 + c, 0)),
                pl.BlockSpec((C, RET_DK // 2), lambda b, c: (t0 + c, 0)),
                pl.BlockSpec((1, vw), lambda b, c: (0, 0)),
                pl.BlockSpec((1, vw), lambda b, c: (0, 0))]
    args = [proj, proj, proj, proj, cos, sin, gng.reshape(1, vw), gnb.reshape(1, vw)]
    sblock = (1, RET_HEADS, RET_DK, RET_DV)
    if s0 is not None:
        in_specs.append(pl.BlockSpec(sblock, lambda b, c: (b, 0, 0, 0)))
        args.append(s0)
    aliases = {}
    if y_prev is not None:
        in_specs.append(pl.BlockSpec(memory_space=pl.ANY))
        aliases = {len(args): 0}
        args.append(y_prev)
    return pl.pallas_call(
        functools.partial(_ret_kernel, C=C, has_s0=s0 is not None, has_prev=y_prev is not None),
        grid=(B, nC),
        in_specs=in_specs,
        out_specs=[pl.BlockSpec((C, vw), lambda b, c: (row(b, c), 0)),
                   pl.BlockSpec(sblock, lambda b, c: (b, 0, 0, 0))],
        out_shape=[jax.ShapeDtypeStruct((M, vw), F32),
                   jax.ShapeDtypeStruct((B,) + sblock[1:], F32)],
        input_output_aliases=aliases,
        compiler_params=_cp("parallel", "arbitrary"),
        name="ret_core",
    )(*args)


def _hgrn_levels(C):
    out, m = [], C // 2
    while m >= 1:
        out.append(m)
        m //= 2
    return out


def _hgrn_mats(C):
    t = np.arange(C)[:, None]
    u = np.arange(C)[None, :]
    mats = [(u <= t).astype(np.float32)]
    for m in _hgrn_levels(C):
        r = 2 * m * (t // (2 * m)) + m - 1
        a = ((u > r) & (u <= t)).astype(np.float32) - ((u > t) & (u <= r)).astype(np.float32)
        mats.append(a)
    return np.stack(mats)


def _hgrn_kernel(*refs, C, has_s0, has_prev):
    q_ref, f_ref, i_ref, g_ref, lb_ref, gng_ref, a_ref = refs[:7]
    n = 7
    s0_ref = None
    if has_s0:
        s0_ref = refs[n]
        n += 1
    if has_prev:
        n += 1
    y_ref, s_ref = refs[n], refs[n + 1]
    c = pl.program_id(1)

    @pl.when(c == 0)
    def _():
        if has_s0:
            s_ref[...] = s0_ref[...]
        else:
            s_ref[...] = jnp.zeros_like(s_ref)

    lb = lb_ref[...]
    f = lb + (1.0 - lb) * _sigmoid(f_ref[...])
    logf = jnp.log(f)
    kk = 1.0 - f
    q = _silu(q_ref[...])
    v = i_ref[...].astype(BF16)
    parts = _split3(logf)

    def amul(idx):
        a = a_ref[idx]
        return _dot(a, parts[0]) + _dot(a, parts[1]) + _dot(a, parts[2])

    bcum = amul(0)
    blast = bcum[C - 1:C, :]
    qb = (q * jnp.exp(bcum)).astype(BF16)
    kst = (kk * jnp.exp(blast - bcum)).astype(BF16)
    eb = jnp.exp(blast)

    ti = lax.broadcasted_iota(I32, (C, C), 0)
    si = lax.broadcasted_iota(I32, (C, C), 1)
    terms = [(q.astype(BF16), kk.astype(BF16), ti == si)]
    for li, m in enumerate(_hgrn_levels(C)):
        dq = amul(1 + li)
        qt = (q * jnp.exp(jnp.minimum(dq, 0.0))).astype(BF16)
        kt = (kk * jnp.exp(jnp.minimum(-dq, 0.0))).astype(BF16)
        blk = 2 * m
        mask = ((ti // blk) == (si // blk)) & ((ti % blk) >= m) & ((si % blk) < m)
        terms.append((qt, kt, mask))

    eye = (lax.broadcasted_iota(I32, (HG_DK, HG_DK), 0) == lax.broadcasted_iota(I32, (HG_DK, HG_DK), 1))
    for h in range(HG_HEADS):
        sl = slice(h * HG_DK, (h + 1) * HG_DK)
        sc = jnp.zeros((C, C), F32)
        for qt, kt, mask in terms:
            sc = sc + jnp.where(mask, _dot_nt(qt[:, sl], kt[:, sl]), 0.0)
        S = s_ref[0, h]
        vh = v[:, sl]
        o = _dot(sc.astype(BF16), vh) + _dot(qb[:, sl], S.astype(BF16))
        ecol = jnp.sum(jnp.where(eye, eb[:, sl], 0.0), axis=1, keepdims=True)
        s_ref[0, h] = ecol * S + _dot_tn(kst[:, sl], vh)
        on = o * lax.rsqrt(jnp.mean(o * o, axis=-1, keepdims=True) + EPS)
        y_ref[:, sl] = on * gng_ref[:, sl] * _sigmoid(g_ref[:, sl])


def hgrn_core(proj, B, T, row0, lb, gng, s0, y_prev):
    M = proj.shape[0]
    W = HG_HEADS * HG_DK
    C = HG_CHUNK if T % HG_CHUNK == 0 else T
    nC = T // C
    r0 = row0 // C
    row = lambda b, c: r0 + b * nC + c
    amats = jnp.asarray(_hgrn_mats(C), BF16)
    col = lambda j: pl.BlockSpec((C, W), lambda b, c: (row(b, c), j))
    in_specs = [col(j) for j in range(4)]
    in_specs += [pl.BlockSpec((1, W), lambda b, c: (0, 0)),
                 pl.BlockSpec((1, W), lambda b, c: (0, 0)),
                 pl.BlockSpec(amats.shape, lambda b, c: (0, 0, 0))]
    args = [proj, proj, proj, proj, lb.reshape(1, W), gng.reshape(1, W), amats]
    sblock = (1, HG_HEADS, HG_DK, HG_DK)
    if s0 is not None:
        in_specs.append(pl.BlockSpec(sblock, lambda b, c: (b, 0, 0, 0)))
        args.append(s0)
    aliases = {}
    if y_prev is not None:
        in_specs.append(pl.BlockSpec(memory_space=pl.ANY))
        aliases = {len(args): 0}
        args.append(y_prev)
    return pl.pallas_call(
        functools.partial(_hgrn_kernel, C=C, has_s0=s0 is not None, has_prev=y_prev is not None),
        grid=(B, nC),
        in_specs=in_specs,
        out_specs=[pl.BlockSpec((C, W), lambda b, c: (row(b, c), 0)),
                   pl.BlockSpec(sblock, lambda b, c: (b, 0, 0, 0))],
        out_shape=[jax.ShapeDtypeStruct((M, W), F32),
                   jax.ShapeDtypeStruct((B,) + sblock[1:], F32)],
        input_output_aliases=aliases,
        compiler_params=_cp("parallel", "arbitrary"),
        name="hgrn_core",
    )(*args)


def _prep_kernel(xin_ref, g_ref, w_ref, cos_ref, sin_ref, gain_ref, *o_refs, flags, outs, classes):
    cos, sin = cos_ref[...], sin_ref[...]
    tm = xin_ref.shape[0]
    c_refs = o_refs[len(outs):len(outs) + len(classes)]
    x_ref = o_refs[-1]
    stage = o_refs[-2] if classes else None
    x_ref[...] = _dot(_rms(xin_ref[...], g_ref[...]).astype(BF16), w_ref[...])
    for (lo, hi), o_ref in zip(outs, o_refs):
        for j in range(lo // LANES, hi // LANES):
            xj = x_ref[:, j * LANES:(j + 1) * LANES]
            if flags[j]:
                y = _rms(xj, gain_ref[j:j + 1, :])
                xj = y * cos + pltpu.roll(y, LANES // 2, 1) * sin
            o_ref[:, j * LANES - lo:(j + 1) * LANES - lo] = xj.astype(o_ref.dtype)
            if any(clo <= j * LANES < chi for clo, chi, _ in classes):
                stage[j] = xj
    for (lo, hi, d), c_ref in zip(classes, c_refs):
        for r in range(d):
            for j in range(lo // LANES, hi // LANES):
                c_ref[r, :, j * LANES - lo:(j + 1) * LANES - lo] = stage[j, pl.ds(r, tm // d, stride=d), :]


def proj_heads(x, g, w, cos, sin, gains, flags, outs, classes=()):
    M, D = x.shape
    W = w.shape[1]
    tm = _tile(M, 256)
    nb = W // LANES
    return pl.pallas_call(
        functools.partial(_prep_kernel, flags=tuple(flags), outs=tuple(outs), classes=tuple(classes)),
        grid=(M // tm,),
        in_specs=[pl.BlockSpec((tm, D), lambda i: (i, 0)),
                  pl.BlockSpec((1, D), lambda i: (0, 0)),
                  pl.BlockSpec((D, W), lambda i: (0, 0)),
                  pl.BlockSpec((tm, LANES), lambda i: (i, 0)),
                  pl.BlockSpec((tm, LANES), lambda i: (i, 0)),
                  pl.BlockSpec((nb, LANES), lambda i: (0, 0))],
        out_specs=([pl.BlockSpec((tm, hi - lo), lambda i: (i, 0)) for lo, hi in outs]
                   + [pl.BlockSpec((d, tm // d, hi - lo), lambda i: (0, i, 0)) for lo, hi, d in classes]),
        out_shape=([jax.ShapeDtypeStruct((M, hi - lo), F32) for lo, hi in outs]
                   + [jax.ShapeDtypeStruct((d, M // d, hi - lo), F32) for lo, hi, d in classes]),
        scratch_shapes=([pltpu.VMEM((nb, tm, LANES), F32)] if classes else []) + [pltpu.VMEM((tm, W), F32)],
        compiler_params=_cp("parallel"),
        name="proj_heads",
    )(x, g.reshape(1, D), w, cos, sin, gains)


def _softmax_head(s, valid, v):
    s = jnp.where(valid, s, NEG)
    m = jnp.max(s, axis=-1, keepdims=True)
    p = jnp.exp(s - m)
    l = jnp.sum(p, axis=-1, keepdims=True)
    o = _dot(p.astype(BF16), v) / l
    return o, m + jnp.log(l)


def _dil_prompt_kernel(q_ref, kc_ref, kp_ref, vc_ref, vp_ref, o_ref, l_ref, *, tq, reach):
    qi = pl.program_id(2)
    scale = HEAD_DIM ** -0.5
    qpos = qi * tq + lax.broadcasted_iota(I32, (tq, 2 * tq), 0)
    kpos = (qi - 1) * tq + lax.broadcasted_iota(I32, (tq, 2 * tq), 1)
    delta = qpos - kpos
    valid = (delta >= 0) & (delta <= reach) & (kpos >= 0)
    lane = lax.broadcasted_iota(I32, (tq, LANES), 1)
    lse_all = jnp.zeros((tq, LANES), F32)
    for h in range(DIL_HEADS):
        sl = slice(h * HEAD_DIM, (h + 1) * HEAD_DIM)
        q = q_ref[:, sl].astype(BF16)
        k = jnp.concatenate([kp_ref[:, sl], kc_ref[:, sl]], axis=0).astype(BF16)
        v = jnp.concatenate([vp_ref[:, sl], vc_ref[:, sl]], axis=0).astype(BF16)
        o, lse = _softmax_head(_dot_nt(q, k) * scale, valid, v)
        o_ref[h] = o
        lse_all = jnp.where(lane == h, lse, lse_all)
    l_ref[...] = lse_all


def dil_prompt(qkv, gi, col0, B, T):
    window, d = DIL_GROUPS[gi]
    HW = DIL_HEADS * HEAD_DIM
    Tc = T // d
    tq = min(DIL_QBLOCK, Tc)
    nq = Tc // tq

    def blk(off, prev):
        return pl.BlockSpec(
            (None, tq, HW),
            lambda b, r, qi: (r, b * nq + (jnp.maximum(qi - 1, 0) if prev else qi), col0 + off))

    return pl.pallas_call(
        functools.partial(_dil_prompt_kernel, tq=tq, reach=window // d),
        grid=(B, d, nq),
        in_specs=[blk(0, False), blk(1, False), blk(1, True), blk(2, False), blk(2, True)],
        out_specs=[pl.BlockSpec((None, DIL_HEADS, tq, HEAD_DIM), lambda b, r, qi: (r, 0, b * nq + qi, 0)),
                   pl.BlockSpec((None, tq, LANES), lambda b, r, qi: (r, b * nq + qi, 0))],
        out_shape=[jax.ShapeDtypeStruct((d, DIL_HEADS, B * Tc, HEAD_DIM), F32),
                   jax.ShapeDtypeStruct((d, B * Tc, LANES), F32)],
        compiler_params=_cp("parallel", "parallel", "arbitrary"),
        name=f"dil_prompt_{gi}",
    )(qkv, qkv, qkv, qkv, qkv)


def _dil_decode_kernel(q_ref, kn_ref, vn_ref, buf_ref, o_ref, l_ref, nb_ref, *, L, Ts, d, window):
    H = DIL_HEADS
    rpp = 2 * H
    scale = HEAD_DIM ** -0.5
    t = lax.broadcasted_iota(I32, (Ts, L + Ts), 0)
    i = lax.broadcasted_iota(I32, (Ts, L + Ts), 1)
    delta = L + t - i
    valid = (delta >= 0) & (delta % d == 0) & (delta <= window)
    lane = lax.broadcasted_iota(I32, (Ts, LANES), 1)
    lse_all = jnp.zeros((Ts, LANES), F32)
    for h in range(H):
        sl = slice(h * HEAD_DIM, (h + 1) * HEAD_DIM)
        q = q_ref[:, sl].astype(BF16)
        k = jnp.concatenate([buf_ref[0, pl.ds(h, L, stride=rpp), :], kn_ref[:, sl]], axis=0).astype(BF16)
        v = jnp.concatenate([buf_ref[0, pl.ds(H + h, L, stride=rpp), :], vn_ref[:, sl]], axis=0).astype(BF16)
        o, lse = _softmax_head(_dot_nt(q, k) * scale, valid, v)
        o_ref[h] = o
        lse_all = jnp.where(lane == h, lse, lse_all)
        nb_ref[0, pl.ds((L - Ts) * rpp + h, Ts, stride=rpp), :] = kn_ref[:, sl]
        nb_ref[0, pl.ds((L - Ts) * rpp + H + h, Ts, stride=rpp), :] = vn_ref[:, sl]
    l_ref[...] = lse_all
    nb_ref[0, 0:(L - Ts) * rpp, :] = buf_ref[0, Ts * rpp:L * rpp, :]


def dil_decode(pp, gi, B, Ts, row0, buf):
    window, d = DIL_GROUPS[gi]
    HW = DIL_HEADS * HEAD_DIM
    rows = buf.shape[1]
    L = rows // (2 * DIL_HEADS)
    r0 = row0 // Ts
    col = lambda off: pl.BlockSpec((Ts, HW), lambda b: (r0 + b, gi * 3 + off))
    return pl.pallas_call(
        functools.partial(_dil_decode_kernel, L=L, Ts=Ts, d=d, window=window),
        grid=(B,),
        in_specs=[col(0), col(1), col(2),
                  pl.BlockSpec((1, rows, HEAD_DIM), lambda b: (b, 0, 0))],
        out_specs=[pl.BlockSpec((None, DIL_HEADS, Ts, HEAD_DIM), lambda b: (0, 0, b, 0)),
                   pl.BlockSpec((None, Ts, LANES), lambda b: (0, b, 0)),
                   pl.BlockSpec((1, rows, HEAD_DIM), lambda b: (b, 0, 0))],
        out_shape=[jax.ShapeDtypeStruct((1, DIL_HEADS, B * Ts, HEAD_DIM), F32),
                   jax.ShapeDtypeStruct((1, B * Ts, LANES), F32),
                   jax.ShapeDtypeStruct((B, rows, HEAD_DIM), F32)],
        compiler_params=_cp("parallel"),
        name=f"dil_decode_{gi}",
    )(pp, pp, pp, buf)


def _dil_merge_kernel(*refs, ds):
    ng = len(ds)
    o_refs, l_refs, out_ref, stage = refs[:ng], refs[ng:2 * ng], refs[2 * ng], refs[2 * ng + 1]
    dmax = max(ds)
    n = out_ref.shape[0] // dmax
    for r in range(dmax):
        def rows(d):
            return r % d, pl.ds(r // d, n, stride=dmax // d)

        ls = [l_refs[g][rows(ds[g]) + (slice(None),)] for g in range(ng)]
        m = functools.reduce(jnp.maximum, ls)
        es = [jnp.exp(l - m) for l in ls]
        den = functools.reduce(lambda a, b: a + b, es)
        ws = [e / den for e in es]
        for h in range(DIL_HEADS):
            acc = None
            for g in range(ng):
                c, rs = rows(ds[g])
                term = ws[g][:, h:h + 1] * o_refs[g][c, h, rs, :]
                acc = term if acc is None else acc + term
            stage[h, pl.ds(r, n, stride=dmax), :] = acc
    for h in range(DIL_HEADS):
        out_ref[:, h * HEAD_DIM:(h + 1) * HEAD_DIM] = stage[h]


def dil_merge(os_, ls, rows):
    ds = tuple(o.shape[0] for o in os_)
    H, D = os_[0].shape[1], os_[0].shape[3]
    tm = _tile(rows, 256)
    ospec = lambda a: pl.BlockSpec((a.shape[0], H, tm // a.shape[0], D), lambda i: (0, 0, i, 0))
    lspec = lambda a: pl.BlockSpec((a.shape[0], tm // a.shape[0], a.shape[2]), lambda i: (0, i, 0))
    return pl.pallas_call(
        functools.partial(_dil_merge_kernel, ds=ds),
        grid=(rows // tm,),
        in_specs=[ospec(a) for a in os_] + [lspec(a) for a in ls],
        out_specs=pl.BlockSpec((tm, H * D), lambda i: (i, 0)),
        out_shape=jax.ShapeDtypeStruct((rows, H * D), F32),
        scratch_shapes=[pltpu.VMEM((H, tm, D), F32)],
        compiler_params=_cp("parallel"),
        name="dil_merge",
    )(*os_, *ls)


def _nsa_compress_kernel(tab_ref, pg_ref, pe_ref, w1_ref, w2_ref, o_ref, hs_ref, *, n_pages, ncmp):
    del tab_ref
    p = pl.program_id(1)
    hb = PAGE // NSA_CMP_STRIDE
    ncol = 2 * NSA_KV_HEADS
    rowlen = 4 * NSA_KV_HEADS * HEAD_DIM
    for c in range(ncol):
        for j in range(NSA_CMP_STRIDE):
            hs_ref[c, pl.ds(pl.multiple_of(p * hb, hb), hb), j * HEAD_DIM:(j + 1) * HEAD_DIM] = (
                pg_ref[0, :, j * rowlen + c * HEAD_DIM:j * rowlen + (c + 1) * HEAD_DIM])

    @pl.when(p == n_pages - 1)
    def _():
        half = NSA_CMP_STRIDE * HEAD_DIM
        R = n_pages * hb
        rowi = lax.broadcasted_iota(I32, (R, HEAD_DIM), 0)
        for c in range(ncol):
            s = c // NSA_KV_HEADS
            H = hs_ref[c]
            a = _dot((H + pe_ref[s, 0:1, :]).astype(BF16), w1_ref[s, 0:half, :])
            bm = _dot((H + pe_ref[s, 1:2, :]).astype(BF16), w1_ref[s, half:2 * half, :])
            pre = a + pltpu.roll(bm, R - 1, 0)
            out = _dot(_silu(pre).astype(BF16), w2_ref[s])
            o_ref[0, c] = jnp.where(rowi < ncmp, out, 0.0).astype(BF16)


def nsa_compress(pages, table, B, n_pages, pe, w1, w2):
    hb = PAGE // NSA_CMP_STRIDE
    R = n_pages * hb
    ncmp = (n_pages * PAGE - NSA_CMP_LEN) // NSA_CMP_STRIDE + 1
    half = NSA_CMP_STRIDE * HEAD_DIM
    pv = pages.reshape(pages.shape[0], hb, NSA_CMP_STRIDE * pages.shape[2])
    grid_spec = pltpu.PrefetchScalarGridSpec(
        num_scalar_prefetch=1,
        grid=(B, n_pages),
        in_specs=[pl.BlockSpec((1,) + pv.shape[1:], lambda b, p, tab: (tab[b * n_pages + p], 0, 0)),
                  pl.BlockSpec((2, 2, half), lambda b, p, tab: (0, 0, 0)),
                  pl.BlockSpec(w1.shape, lambda b, p, tab: (0, 0, 0)),
                  pl.BlockSpec(w2.shape, lambda b, p, tab: (0, 0, 0))],
        out_specs=pl.BlockSpec((1, 4, R, HEAD_DIM), lambda b, p, tab: (b, 0, 0, 0)),
        scratch_shapes=[pltpu.VMEM((4, R, half), F32)])
    return pl.pallas_call(
        functools.partial(_nsa_compress_kernel, n_pages=n_pages, ncmp=ncmp),
        grid_spec=grid_spec,
        out_shape=jax.ShapeDtypeStruct((B, 4, R, HEAD_DIM), BF16),
        compiler_params=_cp("parallel", "arbitrary"),
        name="nsa_compress",
    )(table, pv, pe.reshape(2, 2, half), w1, w2)


def _flash_update(m_ref, l_ref, a_ref, k, s, v_ones):
    D = HEAD_DIM
    m_old = m_ref[k]
    m_new = jnp.maximum(m_old, jnp.max(s, axis=-1, keepdims=True))
    m_use = jnp.maximum(m_new, -1e20)
    alpha = jnp.exp(m_old - m_use)
    pv = _dot(jnp.exp(s - m_use).astype(BF16), v_ones)
    l_ref[k] = alpha * l_ref[k] + pv[:, D:D + 1]
    a_ref[k] = alpha * a_ref[k] + pv[:, :D]
    m_ref[k] = m_new


def _nsa_qstack(q_ref, k, scale=None):
    G, D = NSA_GROUP, HEAD_DIM
    q = jnp.concatenate([q_ref[:, (k * G + g) * D:(k * G + g + 1) * D] for g in range(G)], axis=0)
    return (q if scale is None else q * scale).astype(BF16)


def _nsa_cmp_select(qs, kc, vc, c2s, pos_r, pos_q, n_slc, ncmp):
    R, tq = qs.shape[0], pos_q.shape[0]
    nidx = lax.broadcasted_iota(I32, (R, kc.shape[0]), 1)
    s = _dot_nt(qs, kc) * HEAD_DIM ** -0.5
    valid = (NSA_CMP_STRIDE * nidx + NSA_CMP_LEN - 1 <= pos_r) & (nidx < ncmp)
    s = jnp.where(valid, s, NEG)
    pr = jnp.where(valid, jnp.exp(s - jnp.max(s, axis=-1, keepdims=True)), 0.0)
    pc = pr / jnp.maximum(jnp.sum(pr, axis=-1, keepdims=True), 1e-30)
    o_cmp = _dot(pc.astype(BF16), vc)
    psum = pc[0:tq]
    for g in range(1, NSA_GROUP):
        psum = psum + pc[g * tq:(g + 1) * tq]
    ph, pm, plo = _split3(psum)
    imp = _dot(ph, c2s) + _dot(pm, c2s) + _dot(plo, c2s)
    j = lax.broadcasted_iota(I32, (tq, LANES), 1)
    cur = pos_q // NSA_SLC_LEN
    forced = (j == 0) | (j == cur) | (j == cur - 1)
    score = jnp.where(forced, 1e4, jnp.where(j <= cur, imp, -1.0))
    score = jnp.where(j < n_slc, score, -2.0)
    cnt = jnp.zeros((tq, LANES), F32)
    for jp in range(n_slc):
        col = score[:, jp:jp + 1]
        cnt = cnt + ((col > score) | ((col == score) & (jp < j))).astype(F32)
    return o_cmp, (cnt < float(min(NSA_TOP_N, n_slc))).astype(F32)


def _nsa_attn_kernel(qi_ref, kb_ref, q_ref, gate_ref, kc_ref, c2s_ref, blk_ref, kv_ref, wkv_ref, o_ref,
                     ocmp, qa, m_s, l_s, a_s, m_w, l_w, a_w, *, tq, KB, n_slc, ncmp):
    step = pl.program_id(1)
    qi, kb = qi_ref[step], kb_ref[step]
    G, KVH, D = NSA_GROUP, NSA_KV_HEADS, HEAD_DIM
    R = G * tq
    pos_r = qi * tq + lax.broadcasted_iota(I32, (R, 1), 0) % tq
    pos_q = qi * tq + lax.broadcasted_iota(I32, (tq, 1), 0)
    kb_last = (qi * tq + tq - 1) // KB

    @pl.when(kb == 0)
    def _():
        for k in range(KVH):
            ocmp[k], sel = _nsa_cmp_select(_nsa_qstack(q_ref, k), kc_ref[0, k], kc_ref[0, KVH + k], c2s_ref[...],
                                           pos_r, pos_q, n_slc, ncmp)
            bias = ((sel - 1.0) * -NEG).astype(BF16)
            qa[k] = jnp.concatenate([_nsa_qstack(q_ref, k, D ** -0.5), jnp.concatenate([bias] * G, axis=0)], axis=1)
        for ref in (m_s, m_w):
            ref[...] = jnp.full(ref.shape, NEG, F32)
        for ref in (l_s, a_s, l_w, a_w):
            ref[...] = jnp.zeros(ref.shape, F32)

    dist = pos_r - (kb * KB + lax.broadcasted_iota(I32, (R, KB), 1))
    ones = jnp.ones((KB, D), BF16)
    for k in range(KVH):
        ks = jnp.concatenate([kv_ref[:, k * D:(k + 1) * D].astype(BF16), blk_ref[...]], axis=1)
        vs = jnp.concatenate([kv_ref[:, (KVH + k) * D:(KVH + k + 1) * D].astype(BF16), ones], axis=1)
        s = _dot_nt(qa[k], ks)

        @pl.when(kb == kb_last)
        def _():
            _flash_update(m_s, l_s, a_s, k, jnp.where(dist >= 0, s, NEG), vs)

        @pl.when(kb != kb_last)
        def _():
            _flash_update(m_s, l_s, a_s, k, s, vs)

    @pl.when(kb >= kb_last - NSA_WINDOW // KB)
    def _():
        valid = (dist >= 0) & (dist <= NSA_WINDOW)
        for k in range(KVH):
            kw = wkv_ref[:, k * D:(k + 1) * D].astype(BF16)
            vw = jnp.concatenate([wkv_ref[:, (KVH + k) * D:(KVH + k + 1) * D].astype(BF16), ones], axis=1)
            _flash_update(m_w, l_w, a_w, k, jnp.where(valid, _dot_nt(qa[k, :, 0:D], kw), NEG), vw)

    @pl.when(kb == kb_last)
    def _():
        gs = _sigmoid(gate_ref[...])
        for k in range(KVH):
            _nsa_gated_out(o_ref, gs, k, tq, ocmp[k], a_s[k] / jnp.maximum(l_s[k], 1e-30),
                           a_w[k] / jnp.maximum(l_w[k], 1e-30))


def _nsa_gated_out(o_ref, gs, k, tq, o_c, o_s, o_w):
    for g in range(NSA_GROUP):
        hd = k * NSA_GROUP + g
        rows = slice(g * tq, (g + 1) * tq)
        o = (gs[:, 3 * hd:3 * hd + 1] * o_c[rows] + gs[:, 3 * hd + 1:3 * hd + 2] * o_s[rows]
             + gs[:, 3 * hd + 2:3 * hd + 3] * o_w[rows])
        o_ref[:, hd * HEAD_DIM:(hd + 1) * HEAD_DIM] = o


def _softmax_two(s1, ok1, v1, s2, ok2, v2):
    s1 = jnp.where(ok1, s1, NEG)
    s2 = jnp.where(ok2, s2, NEG)
    m = jnp.maximum(jnp.max(s1, axis=-1, keepdims=True), jnp.max(s2, axis=-1, keepdims=True))
    p1 = jnp.where(ok1, jnp.exp(s1 - m), 0.0)
    p2 = jnp.where(ok2, jnp.exp(s2 - m), 0.0)
    l = jnp.sum(p1, axis=-1, keepdims=True) + jnp.sum(p2, axis=-1, keepdims=True)
    return (_dot(p1.astype(BF16), v1) + _dot(p2.astype(BF16), v2)) / jnp.maximum(l, 1e-30)


def _nsa_sample_kernel(*refs, Ts, n_pages, Lw, n_slc, ncmp, past):
    (tab_ref, q_ref, gate_ref, xn_ref, xw_ref, wb_ref, pe_ref, w1_ref, w2_ref, c2s_ref, exp_ref) = refs[:11]
    pg_refs = refs[11:11 + n_pages]
    o_ref, nwb_ref, hs_ref = refs[12 + n_pages:]
    del tab_ref
    G, KVH, D = NSA_GROUP, NSA_KV_HEADS, HEAD_DIM
    R = G * Ts
    scale = D ** -0.5
    rpt = 4 * KVH
    hb = PAGE // NSA_CMP_STRIDE
    half = NSA_CMP_STRIDE * D
    pos_r = past + lax.broadcasted_iota(I32, (R, 1), 0) % Ts
    pos_q = past + lax.broadcasted_iota(I32, (Ts, 1), 0)

    def tile_rows(x):
        return jnp.concatenate([x] * G, axis=0)

    for u in range(n_pages):
        for c in range(2 * KVH):
            for j in range(NSA_CMP_STRIDE):
                hs_ref[c, u * hb:(u + 1) * hb, j * D:(j + 1) * D] = (
                    pg_refs[u][0, pl.ds(j * rpt + c, hb, stride=NSA_CMP_STRIDE * rpt), :])
    nrow = n_pages * hb
    rowi = lax.broadcasted_iota(I32, (nrow, D), 0)
    cmp = []
    for c in range(2 * KVH):
        s = c // KVH
        H = hs_ref[c]
        a = _dot((H + pe_ref[s, 0:1, :]).astype(BF16), w1_ref[s, 0:half, :])
        bm = _dot((H + pe_ref[s, 1:2, :]).astype(BF16), w1_ref[s, half:2 * half, :])
        out = _dot(_silu(a + pltpu.roll(bm, nrow - 1, 0)).astype(BF16), w2_ref[s])
        cmp.append(jnp.where(rowi < ncmp, out, 0.0).astype(BF16))

    gs = _sigmoid(gate_ref[...])
    jn = lax.broadcasted_iota(I32, (R, Ts), 1)
    causal = past + jn <= pos_r
    wpos = past - Lw + lax.broadcasted_iota(I32, (R, Lw), 1)
    wdist = pos_r - wpos
    wvalid = (wdist >= 0) & (wdist <= NSA_WINDOW)
    blk = past // NSA_SLC_LEN
    for k in range(KVH):
        qs = _nsa_qstack(q_ref, k)
        o_c, selk = _nsa_cmp_select(qs, cmp[k], cmp[KVH + k], c2s_ref[...], pos_r, pos_q, n_slc, ncmp)
        ks = jnp.concatenate([pg[0, pl.ds(2 * KVH + k, PAGE, stride=rpt), :] for pg in pg_refs],
                             axis=0).astype(BF16)
        vs = jnp.concatenate([pg[0, pl.ds(3 * KVH + k, PAGE, stride=rpt), :] for pg in pg_refs],
                             axis=0).astype(BF16)
        ok_past = tile_rows(_dot(selk.astype(BF16), exp_ref[...])) > 0.5
        ok_new = (tile_rows(selk[:, blk:blk + 1]) > 0.5) & causal
        kn = xn_ref[:, (2 * KVH + k) * D:(2 * KVH + k + 1) * D].astype(BF16)
        vn = xn_ref[:, (3 * KVH + k) * D:(3 * KVH + k + 1) * D].astype(BF16)
        o_s = _softmax_two(_dot_nt(qs, ks) * scale, ok_past, vs, _dot_nt(qs, kn) * scale, ok_new, vn)
        kw = wb_ref[0, pl.ds(k, Lw, stride=2 * KVH), :].astype(BF16)
        vw = wb_ref[0, pl.ds(KVH + k, Lw, stride=2 * KVH), :].astype(BF16)
        kwn = xw_ref[:, k * D:(k + 1) * D]
        vwn = xw_ref[:, (KVH + k) * D:(KVH + k + 1) * D]
        o_w = _softmax_two(_dot_nt(qs, kw) * scale, wvalid, vw,
                           _dot_nt(qs, kwn.astype(BF16)) * scale, causal, vwn.astype(BF16))
        _nsa_gated_out(o_ref, gs, k, Ts, o_c, o_s, o_w)
        nwb_ref[0, pl.ds((Lw - Ts) * 2 * KVH + k, Ts, stride=2 * KVH), :] = kwn
        nwb_ref[0, pl.ds((Lw - Ts) * 2 * KVH + KVH + k, Ts, stride=2 * KVH), :] = vwn
    nwb_ref[0, 0:(Lw - Ts) * 2 * KVH, :] = wb_ref[0, Ts * 2 * KVH:Lw * 2 * KVH, :]


def nsa_sample(q, gate, xnew, xwin, pool, table, wbuf, pe, w1, w2, B, Ts, row0, past, o_prev):
    M, QW = q.shape
    D = HEAD_DIM
    n_pages = past // PAGE
    assert past % PAGE == 0 and Ts < NSA_CMP_STRIDE
    Lw = wbuf.shape[1] // (2 * NSA_KV_HEADS)
    L = past + Ts
    n_slc = -(-L // NSA_SLC_LEN)
    ncmp = (L - NSA_CMP_LEN) // NSA_CMP_STRIDE + 1
    hb = PAGE // NSA_CMP_STRIDE
    half = NSA_CMP_STRIDE * D
    c2s = _cmp_to_slc(n_pages * hb, ncmp, n_slc)
    expand = jnp.asarray(np.arange(LANES)[:, None] == np.arange(past)[None, :] // NSA_SLC_LEN, BF16)
    r0 = row0 // Ts
    rows = lambda w: pl.BlockSpec((Ts, w), lambda b, tab: (r0 + b, 0))
    full = lambda a: pl.BlockSpec(a.shape, lambda b, tab: (0,) * a.ndim)
    page = lambda u: pl.BlockSpec((1,) + pool.shape[1:], lambda b, tab: (tab[b * n_pages + u], 0, 0))
    pe2 = pe.reshape(2, 2, half)
    in_specs = ([rows(QW), rows(LANES), rows(xnew.shape[1]), rows(xwin.shape[1]),
                 pl.BlockSpec((1,) + wbuf.shape[1:], lambda b, tab: (b, 0, 0)),
                 full(pe2), full(w1), full(w2), full(c2s), full(expand)]
                + [page(u) for u in range(n_pages)] + [pl.BlockSpec(memory_space=pl.ANY)])
    grid_spec = pltpu.PrefetchScalarGridSpec(
        num_scalar_prefetch=1,
        grid=(B,),
        in_specs=in_specs,
        out_specs=[rows(QW), pl.BlockSpec((1,) + wbuf.shape[1:], lambda b, tab: (b, 0, 0))],
        scratch_shapes=[pltpu.VMEM((2 * NSA_KV_HEADS, n_pages * hb, half), F32)])
    return pl.pallas_call(
        functools.partial(_nsa_sample_kernel, Ts=Ts, n_pages=n_pages, Lw=Lw, n_slc=n_slc, ncmp=ncmp, past=past),
        grid_spec=grid_spec,
        out_shape=[jax.ShapeDtypeStruct((M, QW), F32), jax.ShapeDtypeStruct(wbuf.shape, F32)],
        input_output_aliases={11 + n_pages: 0},
        compiler_params=_cp("parallel"),
        name="nsa_sample",
    )(table, q, gate, xnew, xwin, wbuf, pe2, w1, w2, c2s, expand, *([pool] * n_pages), o_prev)


def _cmp_to_slc(nrows, ncmp, n_slc):
    i = np.arange(nrows)[:, None] * NSA_CMP_STRIDE
    j = np.arange(LANES)[None, :] * NSA_SLC_LEN
    ov = np.clip(np.minimum(i + NSA_CMP_LEN, j + NSA_SLC_LEN) - np.maximum(i, j), 0, None) / NSA_CMP_LEN
    ov = np.where((np.arange(nrows)[:, None] < ncmp) & (np.arange(LANES)[None, :] < n_slc), ov, 0.0)
    return jnp.asarray(ov, BF16)


def nsa_attn(q, gate, kc, new, win, B, T):
    M = q.shape[0]
    D = HEAD_DIM
    QW = NSA_HEADS * D
    KW = 2 * NSA_KV_HEADS * D
    tq = min(4 * PAGE, T)
    KB = min(NSA_WINDOW, T)
    nqb, nkb = T // tq, T // KB
    n_slc = -(-T // NSA_SLC_LEN)
    ncmp = (T - NSA_CMP_LEN) // NSA_CMP_STRIDE + 1
    R = NSA_GROUP * tq
    c2s = _cmp_to_slc(kc.shape[2], ncmp, n_slc)
    blocks = jnp.asarray(np.arange(T)[:, None] // NSA_SLC_LEN == np.arange(LANES)[None, :], BF16)
    pairs = [(qi, kb) for qi in range(nqb) for kb in range((qi * tq + tq - 1) // KB + 1)]
    qi_tab = jnp.asarray([p[0] for p in pairs], I32)
    kb_tab = jnp.asarray([p[1] for p in pairs], I32)
    qrow = lambda b, s, qt, kt: (b * nqb + qt[s], 0)
    in_specs = [pl.BlockSpec((tq, QW), qrow),
                pl.BlockSpec((tq, LANES), qrow),
                pl.BlockSpec((1,) + kc.shape[1:], lambda b, s, qt, kt: (b, 0, 0, 0)),
                pl.BlockSpec(c2s.shape, lambda b, s, qt, kt: (0, 0)),
                pl.BlockSpec((KB, LANES), lambda b, s, qt, kt: (kt[s], 0)),
                pl.BlockSpec((KB, KW), lambda b, s, qt, kt: (b * nkb + kt[s], 1)),
                pl.BlockSpec((KB, KW), lambda b, s, qt, kt: (b * nkb + kt[s], 0))]
    vec = lambda: pltpu.VMEM((NSA_KV_HEADS, R, 1), F32)
    mat = lambda: pltpu.VMEM((NSA_KV_HEADS, R, D), F32)
    grid_spec = pltpu.PrefetchScalarGridSpec(
        num_scalar_prefetch=2,
        grid=(B, len(pairs)),
        in_specs=in_specs,
        out_specs=pl.BlockSpec((tq, QW), qrow),
        scratch_shapes=[mat(), pltpu.VMEM((NSA_KV_HEADS, R, 2 * D), BF16),
                        vec(), vec(), mat(), vec(), vec(), mat()])
    return pl.pallas_call(
        functools.partial(_nsa_attn_kernel, tq=tq, KB=KB, n_slc=n_slc, ncmp=ncmp),
        grid_spec=grid_spec,
        out_shape=jax.ShapeDtypeStruct((M, QW), F32),
        compiler_params=_cp("parallel", "arbitrary"),
        name="nsa_attn_prompt",
    )(qi_tab, kb_tab, q, gate, kc, c2s, blocks, new, win)


def _router_kernel(x_ref, g_ref, wr_ref, h_ref, e_ref, p_ref):
    xn = _rms(x_ref[...], g_ref[...])
    h_ref[...] = xn
    xh, xm, xl = _split3(xn)
    wh, wm, wl = _split3(wr_ref[...])
    logits = (_dot(xh, wh) + _dot(xh, wm) + _dot(xm, wh)
              + _dot(xh, wl) + _dot(xl, wh) + _dot(xm, wm))
    lane = lax.broadcasted_iota(I32, logits.shape, 1)
    lg = jnp.where(lane < N_EXPERTS, logits, -jnp.inf)
    m1 = jnp.max(lg, axis=-1, keepdims=True)
    i1 = jnp.min(jnp.where(lg == m1, lane, LANES), axis=-1, keepdims=True)
    lg2 = jnp.where(lane == i1, -jnp.inf, lg)
    m2 = jnp.max(lg2, axis=-1, keepdims=True)
    i2 = jnp.min(jnp.where(lg2 == m2, lane, LANES), axis=-1, keepdims=True)
    e = jnp.exp(m2 - m1)
    e_ref[...] = jnp.where(lane == 0, i1, jnp.where(lane == 1, i2, 0))
    p_ref[...] = jnp.where(lane == 0, 1.0 / (1.0 + e), jnp.where(lane == 1, e / (1.0 + e), 0.0))


def moe_router(x, g, router):
    M, D = x.shape
    tm = _tile(M, 512)
    wr = jnp.zeros((D, LANES), F32).at[:, :N_EXPERTS].set(router)
    return pl.pallas_call(
        _router_kernel,
        grid=(M // tm,),
        in_specs=[pl.BlockSpec((tm, D), lambda i: (i, 0)),
                  pl.BlockSpec((1, D), lambda i: (0, 0)),
                  pl.BlockSpec((D, LANES), lambda i: (0, 0))],
        out_specs=[pl.BlockSpec((tm, D), lambda i: (i, 0)),
                   pl.BlockSpec((tm, LANES), lambda i: (i, 0)),
                   pl.BlockSpec((tm, LANES), lambda i: (i, 0))],
        out_shape=[jax.ShapeDtypeStruct((M, D), F32),
                   jax.ShapeDtypeStruct((M, LANES), I32),
                   jax.ShapeDtypeStruct((M, LANES), F32)],
        compiler_params=_cp("parallel"),
        name="moe_router",
    )(x, g.reshape(1, D), wr)


def _row_copy(src_ref, dst_ref, sem, src_row, dst_row):
    return pltpu.make_async_copy(src_ref.at[pl.ds(src_row, 1)], dst_ref.at[pl.ds(dst_row, 1)], sem)


def _moe_ffn_kernel(be_ref, rt_ref, nu_ref, h_ref, wg_ref, wu_ref, wd_ref, y_ref,
                    xbuf, xb16, acc, sems, *, tm, n_f):
    del be_ref
    r, f = pl.program_id(0), pl.program_id(1)
    n_used = nu_ref[0]
    used = r < n_used
    slot = r % 2

    def gather(block, slot_):
        def issue(j, carry):
            for p in range(2):
                i = 2 * j + p
                _row_copy(h_ref, xbuf.at[slot_], sems.at[slot_], rt_ref[block * tm + i], i).start(priority=p)
            return carry

        lax.fori_loop(0, tm // 2, issue, 0, unroll=4)

    @pl.when(used & (f == 0) & (r == 0))
    def _():
        gather(0, 0)

    @pl.when(used & (f == 0))
    def _():
        def drain(i, carry):
            _row_copy(h_ref, xbuf.at[slot], sems.at[slot], 0, i).wait()
            return carry

        lax.fori_loop(0, tm, drain, 0, unroll=8)
        xb16[...] = xbuf[slot].astype(BF16)
        acc[...] = jnp.zeros_like(acc)

    @pl.when((r + 1 < n_used) & (f == 0))
    def _():
        gather(r + 1, 1 - slot)

    @pl.when(used)
    def _():
        x = xb16[...]
        hg = _dot(x, wg_ref[0])
        hu = _dot(x, wu_ref[0])
        acc[...] += _dot((_silu(hg) * hu).astype(BF16), wd_ref[0])

    @pl.when(used & (f == n_f - 1))
    def _():
        y_ref[...] = acc[...]

    @pl.when(jnp.logical_not(used) & (f == n_f - 1))
    def _():
        y_ref[...] = jnp.zeros_like(y_ref)


def moe_ffn(h, block_expert, row_tok, n_used, w_up, w_down, layer):
    D = h.shape[1]
    F = w_down.shape[2]
    tm = MOE_ROWS
    nb = block_expert.shape[0]
    tf = F // 2 if (F // 2) % LANES == 0 else F
    n_f = F // tf
    live = lambda r, f, nu: jnp.where(r < nu[0], f, 0)
    grid_spec = pltpu.PrefetchScalarGridSpec(
        num_scalar_prefetch=3,
        grid=(nb, n_f),
        in_specs=[pl.BlockSpec(memory_space=pl.ANY),
                  pl.BlockSpec((None, 1, D, tf), lambda r, f, be, rt, nu: (layer, be[r], 0, live(r, f, nu))),
                  pl.BlockSpec((None, 1, D, tf), lambda r, f, be, rt, nu: (layer, be[r], 0, n_f + live(r, f, nu))),
                  pl.BlockSpec((None, 1, tf, D), lambda r, f, be, rt, nu: (layer, be[r], live(r, f, nu), 0))],
        out_specs=pl.BlockSpec((tm, D), lambda r, f, be, rt, nu: (r, 0)),
        scratch_shapes=[pltpu.VMEM((2, tm, D), F32), pltpu.VMEM((tm, D), BF16), pltpu.VMEM((tm, D), F32),
                        pltpu.SemaphoreType.DMA((2,))])
    return pl.pallas_call(
        functools.partial(_moe_ffn_kernel, tm=tm, n_f=n_f),
        grid_spec=grid_spec,
        out_shape=jax.ShapeDtypeStruct((nb * tm, D), F32),
        compiler_params=_cp("arbitrary", "arbitrary"),
        name="moe_ffn",
    )(block_expert, row_tok, n_used, h, w_up, w_up, w_down)


def _moe_combine_kernel(p0_ref, p1_ref, x_ref, g_ref, y_ref, o_ref, ybuf, sem, *, tm):
    i = pl.program_id(0)

    def issue(t, carry):
        _row_copy(y_ref, ybuf.at[0], sem, p0_ref[i * tm + t], t).start(priority=0)
        _row_copy(y_ref, ybuf.at[1], sem, p1_ref[i * tm + t], t).start(priority=1)
        return carry

    lax.fori_loop(0, tm, issue, 0, unroll=8)

    def drain(t, carry):
        _row_copy(y_ref, ybuf.at[0], sem, 0, t).wait()
        _row_copy(y_ref, ybuf.at[1], sem, 0, t).wait()
        return carry

    lax.fori_loop(0, tm, drain, 0, unroll=8)
    g = g_ref[...]
    o_ref[...] = x_ref[...] + (g[:, 0:1] * ybuf[0] + g[:, 1:2] * ybuf[1])


def moe_combine(x, gates, y, pos0, pos1):
    M, D = x.shape
    tm = _tile(M, 256)
    grid_spec = pltpu.PrefetchScalarGridSpec(
        num_scalar_prefetch=2,
        grid=(M // tm,),
        in_specs=[pl.BlockSpec((tm, D), lambda i, a, b: (i, 0)),
                  pl.BlockSpec((tm, LANES), lambda i, a, b: (i, 0)),
                  pl.BlockSpec(memory_space=pl.ANY)],
        out_specs=pl.BlockSpec((tm, D), lambda i, a, b: (i, 0)),
        scratch_shapes=[pltpu.VMEM((2, tm, D), F32), pltpu.SemaphoreType.DMA(())])
    return pl.pallas_call(
        functools.partial(_moe_combine_kernel, tm=tm),
        grid_spec=grid_spec,
        out_shape=jax.ShapeDtypeStruct((M, D), F32),
        compiler_params=_cp("arbitrary"),
        name="moe_combine",
    )(pos0, pos1, x, gates, y)


def moe_layer(x, g, router, w_up, w_down, layer):
    M = x.shape[0]
    h, eidx, gates = moe_router(x, g, router)
    A = 2 * M
    e = eidx[:, :2].reshape(A)
    onehot = (e[:, None] == jnp.arange(N_EXPERTS, dtype=I32)[None, :]).astype(I32)
    csum = jnp.cumsum(onehot, axis=0)
    rank = jnp.sum(csum * onehot, axis=1) - 1
    counts = csum[-1]
    padded = (counts + MOE_ROWS - 1) // MOE_ROWS * MOE_ROWS
    pend = jnp.cumsum(padded)
    dest = ((pend - padded)[e] + rank).astype(I32)
    nb = (A + N_EXPERTS * (MOE_ROWS - 1) + MOE_ROWS - 1) // MOE_ROWS
    row_tok = jnp.zeros((nb * MOE_ROWS,), I32).at[dest].set(jnp.arange(A, dtype=I32) // 2)
    block_expert = jnp.minimum(
        jnp.searchsorted(pend, jnp.arange(nb, dtype=I32) * MOE_ROWS, side='right'), N_EXPERTS - 1).astype(I32)
    n_used = (pend[-1:] // MOE_ROWS).astype(I32)
    y = moe_ffn(h, block_expert, row_tok, n_used, w_up, w_down, layer)
    dest2 = dest.reshape(M, 2)
    return moe_combine(x, gates, y, dest2[:, 0], dest2[:, 1])


def _rope_tables(pos, half):
    inv = ROPE_THETA ** (-jnp.arange(half, dtype=F32) / half)
    ang = pos.astype(F32)[:, None] * inv[None, :]
    return jnp.cos(ang), jnp.sin(ang)


def _row_rope_tables(Bp, T, Bs, Ts, past):
    pos = jnp.concatenate([jnp.tile(jnp.arange(T, dtype=I32), Bp),
                           jnp.tile(past + jnp.arange(Ts, dtype=I32), Bs)])
    cos, sin = _rope_tables(pos, HEAD_DIM // 2)
    return jnp.concatenate([cos, cos], axis=1), jnp.concatenate([-sin, sin], axis=1)


def kernel(x_prompt, x_sample, state_ret, state_hgrn, cache_dil_w128, cache_dil_w512, cache_dil_w2048,
           cache_nsa_kv, cache_nsa_win, page_table, norm_mix, norm_ffn,
           ret_w_in, ret_gn_g, ret_gn_b, ret_w_out,
           hg_w_in, hg_lower_bounds, hg_gn_g, hg_w_out,
           dil_w_in, dil_qn_g, dil_kn_g, dil_w_out,
           nsa_w_in, nsa_w_gate, nsa_qn_g, nsa_kn_g, nsa_cmp_pe, nsa_cmp_w1, nsa_cmp_w2, nsa_w_out,
           ffn_w_up, ffn_w_down, moe_router_w, moe_w_up, moe_w_down):
    Bp, T, D = x_prompt.shape
    Bs, Ts, _ = x_sample.shape
    Np, Ns = Bp * T, Bs * Ts
    M = Np + Ns
    n_pages = page_table.shape[1]
    past = n_pages * PAGE
    bf = lambda w: w.astype(BF16)

    x = jnp.concatenate([x_prompt.reshape(Np, D), x_sample.reshape(Ns, D)], axis=0)
    lb_all = jnp.cumsum(jax.nn.softmax(hg_lower_bounds.astype(F32), axis=0), axis=0)
    lb_all = lb_all - lb_all[0:1]

    proj = norm_proj(x, norm_mix[0], bf(ret_w_in[0]), 1024)
    cos_r, sin_r = _rope_tables(jnp.arange(max(T, past + Ts), dtype=I32), RET_DK // 2)
    y, ret_p = ret_core(proj, Bp, T, 0, cos_r, sin_r, 0, ret_gn_g[0], ret_gn_b[0], None, None)
    y, ret_s = ret_core(proj, Bs, Ts, Np, cos_r, sin_r, past, ret_gn_g[0], ret_gn_b[0], state_ret[0], y)
    x = out_proj(y, bf(ret_w_out[0]), x)
    x = ffn_dense(x, norm_ffn[0], bf(ffn_w_up[0]), bf(ffn_w_down[0]))

    proj = norm_proj(x, norm_mix[1], bf(hg_w_in[0]), 1024)
    y, hg_p = hgrn_core(proj, Bp, T, 0, lb_all[1], hg_gn_g[0], None, None)
    y, hg_s = hgrn_core(proj, Bs, Ts, Np, lb_all[1], hg_gn_g[0], state_hgrn[0], y)
    x = out_proj(y, bf(hg_w_out[0]), x)
    moe_up, moe_down = bf(moe_w_up), bf(moe_w_down)
    x = moe_layer(x, norm_ffn[1], moe_router_w[0], moe_up, moe_down, 0)

    cos_h, sin_h = _row_rope_tables(Bp, T, Bs, Ts, past)
    HW = DIL_HEADS * HEAD_DIM
    dil_in_w = dil_w_in.shape[2]
    ones = jnp.ones((DIL_HEADS, HEAD_DIM), F32)
    gains = jnp.concatenate([jnp.concatenate([ones * dil_qn_g[0, gi], ones * dil_kn_g[0, gi], ones], axis=0)
                             for gi in range(len(DIL_GROUPS))], axis=0)
    flags = ([True] * (2 * DIL_HEADS) + [False] * DIL_HEADS) * len(DIL_GROUPS)
    dilated = [(gi, d) for gi, (_, d) in enumerate(DIL_GROUPS) if d > 1]
    pp, *by_class = proj_heads(x, norm_mix[2], bf(dil_w_in[0]), cos_h, sin_h, gains, flags, [(0, dil_in_w)],
                               [(gi * 3 * HW, (gi + 1) * 3 * HW, d) for gi, d in dilated])
    by_class = dict(zip([gi for gi, _ in dilated], by_class))
    caches = (cache_dil_w128, cache_dil_w512, cache_dil_w2048)
    op, lp, od, ld, dil_p, dil_s = [], [], [], [], [], []
    for gi, (window, _) in enumerate(DIL_GROUPS):
        if gi in by_class:
            o_g, l_g = dil_prompt(by_class[gi], gi, 0, Bp, T)
        else:
            o_g, l_g = dil_prompt(pp.reshape(1, M, dil_in_w), gi, gi * 3, Bp, T)
        op.append(o_g)
        lp.append(l_g)
        buf = caches[gi][0]
        o_g, l_g, nbuf = dil_decode(pp, gi, Bs, Ts, Np, buf.reshape(Bs, -1, HEAD_DIM))
        od.append(o_g)
        ld.append(l_g)
        keep = min(window, T)
        kv = pp[:Np, gi * 3 * HW + HW:(gi + 1) * 3 * HW].reshape(Bp, T, 2, DIL_HEADS, HEAD_DIM)
        dil_p.append(kv[:, T - keep:][None])
        dil_s.append(nbuf.reshape(buf.shape)[None])
    merged = jnp.concatenate([dil_merge(op, lp, Np), dil_merge(od, ld, Ns)], axis=0)
    x = out_proj(merged, bf(dil_w_out[0]), x)
    x = ffn_dense(x, norm_ffn[2], bf(ffn_w_up[1]), bf(ffn_w_down[1]))

    QW = NSA_HEADS * HEAD_DIM
    KW = NSA_KV_HEADS * HEAD_DIM
    w_in = jnp.concatenate([nsa_w_in[0], nsa_w_gate[0],
                            jnp.zeros((D, LANES - nsa_w_gate.shape[2]), F32)], axis=1)
    ones_q = jnp.ones((NSA_HEADS, HEAD_DIM), F32)
    ones_k = jnp.ones((NSA_KV_HEADS, HEAD_DIM), F32)
    gains = jnp.concatenate([ones_q * nsa_qn_g[0]]
                            + [blk for s in range(3) for blk in (ones_k * nsa_kn_g[0, s], ones_k)]
                            + [jnp.ones((1, HEAD_DIM), F32)], axis=0)
    flags = [True] * NSA_HEADS + ([True] * NSA_KV_HEADS + [False] * NSA_KV_HEADS) * 3 + [False]
    q, new, win, gate = proj_heads(x, norm_mix[3], bf(w_in), cos_h, sin_h, gains, flags,
                                   [(0, QW), (QW, QW + 4 * KW), (QW + 4 * KW, QW + 6 * KW),
                                    (QW + 6 * KW, QW + 6 * KW + LANES)])
    w1, w2 = bf(nsa_cmp_w1[0]), bf(nsa_cmp_w2[0])
    ppages = new.reshape(M // PAGE, PAGE, 4 * KW)
    ptab = jnp.arange(Bp * (T // PAGE), dtype=I32)
    kc_p = nsa_compress(ppages, ptab, Bp, T // PAGE, nsa_cmp_pe[0], w1, w2)
    o = nsa_attn(q, gate, kc_p, new, win, Bp, T)
    pool = cache_nsa_kv[0].reshape(cache_nsa_kv.shape[1], -1, HEAD_DIM)
    wbuf = cache_nsa_win[0]
    o, win_s = nsa_sample(q, gate, new, win, pool, page_table.reshape(-1).astype(I32),
                          wbuf.reshape(Bs, -1, HEAD_DIM), nsa_cmp_pe[0], w1, w2, Bs, Ts, Np, past, o)
    x = out_proj(o, bf(nsa_w_out[0]), x)
    x = moe_layer(x, norm_ffn[3], moe_router_w[1], moe_up, moe_down, 1)

    keep = min(NSA_WINDOW, T)
    win_p = win[:Np].reshape(Bp, T, 2, NSA_KV_HEADS, HEAD_DIM)[:, T - keep:]
    win_s = win_s.reshape(wbuf.shape)
    return (x[:Np].reshape(Bp, T, D), x[Np:].reshape(Bs, Ts, D),
            ret_p[None], ret_s[None], hg_p[None], hg_s[None],
            dil_p[0], dil_s[0], dil_p[1], dil_s[1], dil_p[2], dil_s[2],
            new[:Np].reshape(Bp, T, 4, NSA_KV_HEADS, HEAD_DIM)[None],
            new[Np:].reshape(Bs, Ts, 4, NSA_KV_HEADS, HEAD_DIM)[None],
            win_p[None], win_s[None])
```
